```python
import jax, jax.numpy as jnp
from jax import lax
import numpy as np

D_MODEL = 1024
BATCH = 4
SEQ = 4096
DEPTH = 2

D_MIX = D_MODEL
D_CONV = D_MIX // 2
CONV_WIDTH = 31
CONV_GROUPS = 8
SB_HEADS = 8
SB_HEAD_DIM = (D_MIX // 2) // SB_HEADS
SB_BLOCK = 128
GLA_HEADS = 4
GLA_DV = (D_MIX // 2) // GLA_HEADS
GLA_DK = GLA_DV // 2
GLA_RANK = 16
GLA_TAU = 16.0
GLA_CHUNK = 64
D_POOL = D_MIX // 2
POOL_WINDOWS = (2, 4, 8, 16)
POOL_GROUPS = 4
N_EXPERTS = 16
N_EXPERT_GROUPS = 4
TOP_K = 2
D_EXPERT = D_MODEL // 2
RMS_EPS = 1e-6
LN_EPS = 1e-5

D_IN_EVEN = 2 * D_CONV + 3 * SB_HEADS * SB_HEAD_DIM
D_CAT_EVEN = D_CONV + SB_HEADS * SB_HEAD_DIM
D_IN_ODD = 2 * GLA_HEADS * GLA_DK + 2 * GLA_HEADS * GLA_DV + GLA_RANK + D_POOL
D_CAT_ODD = GLA_HEADS * GLA_DV + D_POOL

kernel_name = "hybrid_conv_stickbreak_gla_pool_moe"


def rmsnorm(x, g):
    xf = x.astype(jnp.float32)
    y = xf * lax.rsqrt(jnp.mean(xf * xf, axis=-1, keepdims=True) + RMS_EPS)
    return (y * g.astype(jnp.float32)).astype(x.dtype)


def conformer_conv(u, conv_w, conv_b, norm_g, norm_b):
    val, gate = jnp.split(u, 2, axis=-1)
    y = val * jax.nn.sigmoid(gate)
    y = lax.conv_general_dilated(
        y, conv_w[:, None, :], window_strides=(1,),
        padding=((CONV_WIDTH - 1, 0),),
        dimension_numbers=('NWC', 'WIO', 'NWC'),
        feature_group_count=D_CONV) + conv_b
    bsz, s, _ = y.shape
    yf = y.astype(jnp.float32).reshape(bsz, s, CONV_GROUPS, D_CONV // CONV_GROUPS)
    mu = jnp.mean(yf, axis=-1, keepdims=True)
    var = jnp.mean(jnp.square(yf - mu), axis=-1, keepdims=True)
    yn = ((yf - mu) * lax.rsqrt(var + LN_EPS)).reshape(bsz, s, D_CONV) * norm_g + norm_b
    return jax.nn.silu(yn).astype(u.dtype)


def stick_breaking_attention(q, k, v):
    bsz, s_len, h, d = q.shape
    q = q.transpose(0, 2, 1, 3) * (d ** -0.5)
    k = k.transpose(0, 2, 1, 3)
    v = v.transpose(0, 2, 1, 3)
    outs = []
    for blk in range(s_len // SB_BLOCK):
        start = blk * SB_BLOCK
        end = start + SB_BLOCK
        z = jnp.einsum('bhqd,bhkd->bhqk', q[:, :, start:end], k[:, :, :end]).astype(jnp.float32)
        t_pos = start + jnp.arange(SB_BLOCK)[:, None]
        s_pos = jnp.arange(end)[None, :]
        strict = s_pos < t_pos
        log_keep = jnp.where(strict, jax.nn.log_sigmoid(-z), 0.0)
        later = lax.cumsum(log_keep, axis=3, reverse=True) - log_keep
        a = jnp.where(strict, jnp.exp(jax.nn.log_sigmoid(z) + later), 0.0)
        outs.append(jnp.einsum('bhqk,bhkd->bhqd', a.astype(v.dtype), v[:, :, :end]))
    o = jnp.concatenate(outs, axis=2)
    return o.transpose(0, 2, 1, 3).reshape(bsz, s_len, h * d)


def gla_chunked(q, k, v, log_a):
    bsz, s_len, h, dk = q.shape
    dv = v.shape[-1]
    n = s_len // GLA_CHUNK

    def chunks(t):
        return t.reshape(bsz, n, GLA_CHUNK, h, t.shape[-1]).transpose(1, 0, 3, 2, 4)

    q, k, v, log_a = chunks(q), chunks(k), chunks(v), chunks(log_a)
    b = jnp.cumsum(log_a, axis=3)
    b_last = b[:, :, :, -1:, :]
    q_in = q * jnp.exp(b)
    k_in = k * jnp.exp(-b)
    k_dec = k * jnp.exp(b_last - b)
    causal = jnp.tril(jnp.ones((GLA_CHUNK, GLA_CHUNK), dtype=bool))
    scores = jnp.where(causal, jnp.einsum('nbhid,nbhjd->nbhij', q_in, k_in), 0.0)
    o_intra = jnp.einsum('nbhij,nbhjv->nbhiv', scores, v)

    def step(state, inp):
        q_n, k_n, v_n, decay_n = inp
        o_n = jnp.einsum('bhid,bhdv->bhiv', q_n, state)
        state = state * jnp.exp(decay_n)[..., None] + jnp.einsum('bhjd,bhjv->bhdv', k_n, v_n)
        return state, o_n

    state0 = jnp.zeros((bsz, h, dk, dv), jnp.float32)
    _, o_inter = lax.scan(step, state0, (q_in, k_dec, v, b_last[:, :, :, 0, :]))
    o = o_intra + o_inter
    return o.transpose(1, 0, 3, 2, 4).reshape(bsz, s_len, h, dv)


def multiscale_pool(u, pool_w, pool_b, pool_scale):
    bsz, s_len, _ = u.shape
    uf = u.astype(jnp.float32).reshape(bsz, s_len, POOL_GROUPS, D_POOL // POOL_GROUPS)
    cs = jnp.cumsum(uf, axis=1)
    pos = jnp.arange(1, s_len + 1, dtype=jnp.float32)
    pooled = []
    for gi, w in enumerate(POOL_WINDOWS):
        cs_g = cs[:, :, gi]
        prev = jnp.pad(cs_g, ((0, 0), (w, 0), (0, 0)))[:, :s_len]
        mean = (cs_g - prev) / jnp.minimum(pos, w)[None, :, None]
        pooled.append(mean - uf[:, :, gi])
    p = jnp.stack(pooled, axis=2)
    y = jnp.einsum('bsgc,gcd->bsgd', p, pool_w.astype(jnp.float32)) + pool_b
    return (y.reshape(bsz, s_len, D_POOL) * pool_scale).astype(u.dtype)


def even_mixer(h, w_in, w_out, conv_w, conv_b, conv_norm_g, conv_norm_b):
    bsz, s_len, _ = h.shape
    u = h @ w_in
    u_conv, u_sb = jnp.split(u, [2 * D_CONV], axis=-1)
    a_out = conformer_conv(u_conv, conv_w, conv_b, conv_norm_g, conv_norm_b)
    q, k, v = [t.reshape(bsz, s_len, SB_HEADS, SB_HEAD_DIM) for t in jnp.split(u_sb, 3, axis=-1)]
    b_out = stick_breaking_attention(q, k, v)
    return jnp.concatenate([a_out, b_out], axis=-1) @ w_out


def odd_mixer(h, w_in, w_out, gla_gate_w, gla_gate_b, gla_norm_g, pool_w, pool_b, pool_scale):
    bsz, s_len, _ = h.shape
    u = h @ w_in
    dkt = GLA_HEADS * GLA_DK
    dvt = GLA_HEADS * GLA_DV
    q, k, v, r, a_low, u_pool = jnp.split(
        u, [dkt, 2 * dkt, 2 * dkt + dvt, 2 * dkt + 2 * dvt, 2 * dkt + 2 * dvt + GLA_RANK], axis=-1)
    log_a = jax.nn.log_sigmoid((a_low @ gla_gate_w + gla_gate_b).astype(jnp.float32)) / GLA_TAU
    qh = q.astype(jnp.float32).reshape(bsz, s_len, GLA_HEADS, GLA_DK) * (GLA_DK ** -0.5)
    kh = k.astype(jnp.float32).reshape(bsz, s_len, GLA_HEADS, GLA_DK)
    vh = v.astype(jnp.float32).reshape(bsz, s_len, GLA_HEADS, GLA_DV)
    ah = log_a.reshape(bsz, s_len, GLA_HEADS, GLA_DK)
    o = gla_chunked(qh, kh, vh, ah)
    o = o * lax.rsqrt(jnp.mean(o * o, axis=-1, keepdims=True) + RMS_EPS)
    o = o * gla_norm_g.astype(jnp.float32).reshape(GLA_HEADS, GLA_DV)
    c_out = (o.reshape(bsz, s_len, dvt) * jax.nn.silu(r.astype(jnp.float32))).astype(h.dtype)
    d_out = multiscale_pool(u_pool, pool_w, pool_b, pool_scale)
    return jnp.concatenate([c_out, d_out], axis=-1) @ w_out


def grouped_moe(h, router_w, router_bias, w_gate, w_up, w_down):
    bsz, s_len, d = h.shape
    n_tok = bsz * s_len
    hf = h.reshape(n_tok, d)
    scores = jax.nn.softmax((hf @ router_w).astype(jnp.float32), axis=-1)
    sel = scores + router_bias.astype(jnp.float32)
    per_group = N_EXPERTS // N_EXPERT_GROUPS
    sel_g = sel.reshape(n_tok, N_EXPERT_GROUPS, per_group)
    group_score = lax.top_k(sel_g, TOP_K)[0].sum(axis=-1)
    g_idx = jnp.argmax(group_score, axis=-1)
    in_group = sel_g[jnp.arange(n_tok), g_idx]
    _, local = lax.top_k(in_group, TOP_K)
    expert_idx = g_idx[:, None] * per_group + local
    w = jnp.take_along_axis(scores, expert_idx, axis=-1)
    w = w / jnp.sum(w, axis=-1, keepdims=True)
    combine = jnp.sum(jax.nn.one_hot(expert_idx, N_EXPERTS, dtype=jnp.float32) * w[..., None], axis=1)
    y = jnp.zeros_like(hf)
    for e in range(N_EXPERTS):
        act = jax.nn.silu(hf @ w_gate[e]) * (hf @ w_up[e])
        y = y + combine[:, e:e + 1].astype(hf.dtype) * (act @ w_down[e])
    return y.reshape(bsz, s_len, d)


def setup_inputs(seed: int = 0) -> dict:
    key = jax.random.key(seed)
    ks = iter(jax.random.split(key, 32))

    def nrm(shape, scale):
        return jax.random.normal(next(ks), shape, jnp.float32) * scale

    n_even = (DEPTH + 1) // 2
    n_odd = DEPTH // 2
    d = D_MODEL
    return {
        'x': nrm((BATCH, SEQ, d), 1.0),
        'c': nrm((BATCH, d), 1.0),
        'ada_w': nrm((DEPTH, d, 6 * d), 0.5 * d ** -0.5),
        'ada_b': nrm((DEPTH, 6 * d), 0.02),
        'norm_mix': 1.0 + nrm((DEPTH, d), 0.02),
        'norm_ffn': 1.0 + nrm((DEPTH, d), 0.02),
        'w_in_even': nrm((n_even, d, D_IN_EVEN), d ** -0.5),
        'w_out_even': nrm((n_even, D_CAT_EVEN, d), D_CAT_EVEN ** -0.5),
        'conv_w': nrm((n_even, CONV_WIDTH, D_CONV), CONV_WIDTH ** -0.5),
        'conv_b': nrm((n_even, D_CONV), 0.02),
        'conv_norm_g': 1.0 + nrm((n_even, D_CONV), 0.02),
        'conv_norm_b': nrm((n_even, D_CONV), 0.02),
        'w_in_odd': nrm((n_odd, d, D_IN_ODD), d ** -0.5),
        'w_out_odd': nrm((n_odd, D_CAT_ODD, d), D_CAT_ODD ** -0.5),
        'gla_gate_w': nrm((n_odd, GLA_RANK, GLA_HEADS * GLA_DK), GLA_RANK ** -0.5),
        'gla_gate_b': nrm((n_odd, GLA_HEADS * GLA_DK), 0.1),
        'gla_norm_g': 1.0 + nrm((n_odd, GLA_HEADS * GLA_DV), 0.02),
        'pool_w': nrm((n_odd, POOL_GROUPS, D_POOL // POOL_GROUPS, D_POOL // POOL_GROUPS), (D_POOL // POOL_GROUPS) ** -0.5),
        'pool_b': nrm((n_odd, POOL_GROUPS, D_POOL // POOL_GROUPS), 0.02),
        'pool_scale': 1.0 + nrm((n_odd, D_POOL), 0.02),
        'router_w': nrm((d, N_EXPERTS), d ** -0.5),
        'router_bias': nrm((N_EXPERTS,), 0.01),
        'moe_w_gate': nrm((DEPTH, N_EXPERTS, d, D_EXPERT), d ** -0.5),
        'moe_w_up': nrm((DEPTH, N_EXPERTS, d, D_EXPERT), d ** -0.5),
        'moe_w_down': nrm((DEPTH, N_EXPERTS, D_EXPERT, d), D_EXPERT ** -0.5),
        'final_norm': 1.0 + nrm((d,), 0.02),
    }


def reference(x, c, ada_w, ada_b, norm_mix, norm_ffn,
              w_in_even, w_out_even, conv_w, conv_b, conv_norm_g, conv_norm_b,
              w_in_odd, w_out_odd, gla_gate_w, gla_gate_b, gla_norm_g,
              pool_w, pool_b, pool_scale,
              router_w, router_bias, moe_w_gate, moe_w_up, moe_w_down, final_norm):
    cond = jax.nn.silu(c)
    for l in range(DEPTH):
        mod = (cond @ ada_w[l] + ada_b[l])[:, None, :]
        sh1, sc1, g1, sh2, sc2, g2 = jnp.split(mod, 6, axis=-1)
        h = rmsnorm(x, norm_mix[l]) * (1.0 + sc1) + sh1
        i = l // 2
        if l % 2 == 0:
            mix = even_mixer(h, w_in_even[i], w_out_even[i], conv_w[i], conv_b[i],
                             conv_norm_g[i], conv_norm_b[i])
        else:
            mix = odd_mixer(h, w_in_odd[i], w_out_odd[i], gla_gate_w[i], gla_gate_b[i],
                            gla_norm_g[i], pool_w[i], pool_b[i], pool_scale[i])
        x = x + g1 * mix
        h = rmsnorm(x, norm_ffn[l]) * (1.0 + sc2) + sh2
        x = x + g2 * grouped_moe(h, router_w, router_bias, moe_w_gate[l], moe_w_up[l], moe_w_down[l])
    return rmsnorm(x, final_norm)
```

```python
import functools

import jax
import jax.numpy as jnp
from jax import lax
from jax.experimental import pallas as pl
from jax.experimental.pallas import tpu as pltpu

F32 = jnp.float32
BF16 = jnp.bfloat16

D_MODEL = 1024
D_CONV = 512
CONV_WIDTH = 31
CONV_GROUP_SIZE = 64
SB_HEADS = 8
SB_HEAD_DIM = 64
GLA_HEADS = 4
GLA_DK = 64
GLA_DV = 128
GLA_RANK = 16
GLA_TAU = 16.0
GLA_CHUNK = 64
D_POOL = 512
POOL_WINDOWS = (2, 4, 8, 16)
N_EXPERTS = 16
N_EXPERT_GROUPS = 4
EXPERTS_PER_GROUP = 4
D_EXPERT = 512
RMS_EPS = 1e-6
LN_EPS = 1e-5

LANES = 128
VMEM_LIMIT = 56 * 1024 * 1024

TM = 512
TS_CONV = 256
CONV_HALO = 32
CONV_ROWS = 64
TS_POOL = 512
POOL_HALO = 16
TQ = 128
TM_MOE = 1024
SB_DEAD = -120.0


def _cparams(*sem):
    return pltpu.CompilerParams(dimension_semantics=sem, vmem_limit_bytes=VMEM_LIMIT)


def _sigmoid(x):
    return 1.0 / (1.0 + jnp.exp(-x))


def _dot(a, b):
    return jnp.dot(a, b, preferred_element_type=F32)


def _dot_nt(a, b, **kw):
    return lax.dot_general(a, b, (((1,), (1,)), ((), ())), preferred_element_type=F32, **kw)


def _dot_tn(a, b):
    return lax.dot_general(a, b, (((0,), (0,)), ((), ())), preferred_element_type=F32)


def _split_dot(x, m):
    hi = x.astype(BF16)
    lo = (x - hi.astype(F32)).astype(BF16)
    return _dot(hi, m) + _dot(lo, m)


def _split_dot_left(m, x):
    hi = x.astype(BF16)
    lo = (x - hi.astype(F32)).astype(BF16)
    return _dot(m, hi) + _dot(m, lo)


def _modulated_rmsnorm(x, g, sc, sh):
    ms = jnp.mean(x * x, axis=-1, keepdims=True)
    return (x * lax.rsqrt(ms + RMS_EPS) * g) * (1.0 + sc) + sh


def _ada_kernel(c_ref, w_ref, b_ref, o_ref):
    c = c_ref[...]
    cond = c * _sigmoid(c)
    o_ref[0] = jnp.dot(cond, w_ref[0], preferred_element_type=F32,
                       precision=lax.Precision.HIGHEST) + b_ref[0]


def _ada(c, ada_w, ada_b):
    depth, d, n = ada_w.shape
    bsz = c.shape[0]
    tn = 1536
    return pl.pallas_call(
        _ada_kernel,
        grid=(depth, n // tn),
        in_specs=[
            pl.BlockSpec((bsz, d), lambda l, j: (0, 0)),
            pl.BlockSpec((1, d, tn), lambda l, j: (l, 0, j)),
            pl.BlockSpec((1, 1, tn), lambda l, j: (l, 0, j)),
        ],
        out_specs=pl.BlockSpec((1, bsz, tn), lambda l, j: (l, 0, j)),
        out_shape=jax.ShapeDtypeStruct((depth, bsz, n), F32),
        compiler_params=_cparams("arbitrary", "arbitrary"),
        name="ada_mod",
    )(c, ada_w, ada_b.reshape(depth, 1, n))


def _even_in_kernel(x_ref, g_ref, sc_ref, sh_ref, w_ref, y_ref, q_ref, k_ref, v_ref):
    h = _modulated_rmsnorm(x_ref[...], g_ref[...], sc_ref[0], sh_ref[0]).astype(BF16)
    val = _dot(h, w_ref[:, 0:512])
    gate = _dot(h, w_ref[:, 512:1024])
    y_ref[...] = val * _sigmoid(gate)
    q_ref[...] = (_dot(h, w_ref[:, 1024:1536]) * (SB_HEAD_DIM ** -0.5)).astype(BF16)
    k_ref[...] = _dot(h, w_ref[:, 1536:2048]).astype(BF16)
    v_ref[...] = _dot(h, w_ref[:, 2048:2560]).astype(BF16)


def _even_in(x2d, g, sc, sh, w_in, seq):
    t, d = x2d.shape
    n = w_in.shape[1]
    per_b = seq // TM
    vec = pl.BlockSpec((1, 1, d), lambda i: (i // per_b, 0, 0))
    out = lambda dt: jax.ShapeDtypeStruct((t, 512), dt)
    return pl.pallas_call(
        _even_in_kernel,
        grid=(t // TM,),
        in_specs=[
            pl.BlockSpec((TM, d), lambda i: (i, 0)),
            pl.BlockSpec((1, d), lambda i: (0, 0)),
            vec, vec,
            pl.BlockSpec((d, n), lambda i: (0, 0)),
        ],
        out_specs=[pl.BlockSpec((TM, 512), lambda i: (i, 0))] * 4,
        out_shape=[out(F32), out(BF16), out(BF16), out(BF16)],
        compiler_params=_cparams("arbitrary"),
        name="even_in",
    )(x2d, g, sc, sh, w_in)


def _conv_kernel(y_ref, halo_ref, w_ref, b_ref, ng_ref, nb_ref, avg_ref, o_ref, buf, cv):
    i = pl.program_id(1)
    buf[0:CONV_HALO, :] = jnp.where(i > 0, halo_ref[...], 0.0)
    buf[CONV_HALO:, :] = y_ref[...]
    first = CONV_HALO - (CONV_WIDTH - 1)
    for rc in range(TS_CONV // CONV_ROWS):
        r0 = rc * CONV_ROWS
        for lc in range(D_CONV // LANES):
            ls = slice(lc * LANES, (lc + 1) * LANES)
            acc = jnp.zeros((CONV_ROWS, LANES), F32)
            for j in range(CONV_WIDTH):
                s = r0 + first + j
                acc = acc + w_ref[j:j + 1, ls] * buf[s:s + CONV_ROWS, ls]
            cv[r0:r0 + CONV_ROWS, ls] = acc + b_ref[:, ls]
    y = cv[...]
    avg = avg_ref[...]
    d = y - _split_dot(y, avg)
    var = _split_dot(d * d, avg)
    yn = d * lax.rsqrt(var + LN_EPS) * ng_ref[...] + nb_ref[...]
    o_ref[...] = (yn * _sigmoid(yn)).astype(BF16)


def _conv(y2d, conv_w, conv_b, norm_g, norm_b, bsz, seq):
    t = y2d.shape[0]
    per_b = seq // TS_CONV
    ratio = TS_CONV // CONV_HALO
    gid = jnp.arange(D_CONV) // CONV_GROUP_SIZE
    avg = jnp.where(gid[:, None] == gid[None, :], 1.0 / CONV_GROUP_SIZE, 0.0).astype(BF16)
    w_pad = jnp.pad(conv_w, ((0, 32 - CONV_WIDTH), (0, 0)))
    row = lambda a: a.reshape(1, D_CONV)
    full = lambda shape: pl.BlockSpec(shape, lambda b, i: (0, 0))
    return pl.pallas_call(
        _conv_kernel,
        grid=(bsz, per_b),
        in_specs=[
            pl.BlockSpec((TS_CONV, D_CONV), lambda b, i: (b * per_b + i, 0)),
            pl.BlockSpec((CONV_HALO, D_CONV),
                         lambda b, i: (jnp.maximum((b * per_b + i) * ratio - 1, 0), 0)),
            full((32, D_CONV)), full((1, D_CONV)), full((1, D_CONV)), full((1, D_CONV)),
            full((D_CONV, D_CONV)),
        ],
        out_specs=pl.BlockSpec((TS_CONV, D_CONV), lambda b, i: (b * per_b + i, 0)),
        out_shape=jax.ShapeDtypeStruct((t, D_CONV), BF16),
        scratch_shapes=[pltpu.VMEM((TS_CONV + CONV_HALO, D_CONV), F32),
                        pltpu.VMEM((TS_CONV, D_CONV), F32)],
        compiler_params=_cparams("arbitrary", "arbitrary"),
        name="conv_module",
    )(y2d, y2d, w_pad, row(conv_b), row(norm_g), row(norm_b), avg)


def _sb_kernel(q_ref, k_ref, v_ref, u_ref, o_ref):
    qi = pl.program_id(2)
    q = q_ref[...]
    lane = lax.broadcasted_iota(jnp.int32, (TQ, LANES), 1)
    row = lax.broadcasted_iota(jnp.int32, (TQ, TQ), 0)
    col = lax.broadcasted_iota(jnp.int32, (TQ, TQ), 1)
    upper = u_ref[...]
    heads = []
    for hd in range(2):
        in_head = (lane < SB_HEAD_DIM) if hd == 0 else (lane >= SB_HEAD_DIM)
        qm = jnp.where(in_head, q, jnp.zeros_like(q))

        def cond(carry):
            kb, alive, _, _ = carry
            return jnp.logical_and(kb >= 0, alive > 0)

        def body(carry):
            kb, _, cr, acc = carry
            ks = pl.multiple_of(kb * TQ, TQ)
            z = _dot_nt(qm, k_ref[pl.ds(ks, TQ), :])
            l1p = jnp.log1p(jnp.exp(-jnp.abs(z)))
            strict = (col + (kb - qi) * TQ) < row
            log_keep = jnp.where(strict, -jnp.maximum(z, 0.0) - l1p, 0.0)
            later = _split_dot(log_keep, upper) + cr
            a = jnp.where(strict, jnp.exp(jnp.minimum(z, 0.0) - l1p + later), 0.0)
            acc = acc + _dot(a.astype(BF16), v_ref[pl.ds(ks, TQ), :])
            cr = cr + jnp.sum(log_keep, axis=-1, keepdims=True)
            alive = (jnp.max(cr) > SB_DEAD).astype(jnp.int32)
            return kb - 1, alive, cr, acc

        init = (qi, jnp.int32(1), jnp.zeros((TQ, 1), F32), jnp.zeros((TQ, LANES), F32))
        heads.append(lax.while_loop(cond, body, init)[3])
    o_ref[...] = jnp.where(lane < SB_HEAD_DIM, heads[0], heads[1]).astype(BF16)


def _sb_attention(q, k, v, bsz, seq):
    t = q.shape[0]
    nq = seq // TQ
    idx = jnp.arange(TQ)
    upper = (idx[:, None] > idx[None, :]).astype(BF16)
    kv = pl.BlockSpec((seq, LANES), lambda b, p, i: (b, p))
    return pl.pallas_call(
        _sb_kernel,
        grid=(bsz, SB_HEADS // 2, nq),
        in_specs=[
            pl.BlockSpec((TQ, LANES), lambda b, p, i: (b * nq + i, p)),
            kv, kv,
            pl.BlockSpec((TQ, TQ), lambda b, p, i: (0, 0)),
        ],
        out_specs=pl.BlockSpec((TQ, LANES), lambda b, p, i: (b * nq + i, p)),
        out_shape=jax.ShapeDtypeStruct((t, SB_HEADS * SB_HEAD_DIM), BF16),
        compiler_params=_cparams("arbitrary", "arbitrary", "arbitrary"),
        name="stick_breaking",
    )(q, k, v, upper)


def _first_of(vals, target):
    idx = jnp.full(target.shape, len(vals) - 1, jnp.int32)
    for j in range(len(vals) - 2, -1, -1):
        idx = jnp.where(vals[j] == target, j, idx)
    return idx


def _top2(vals):
    m1 = functools.reduce(jnp.maximum, vals)
    i1 = _first_of(vals, m1)
    rest = [jnp.where(i1 == j, -jnp.inf, v) for j, v in enumerate(vals)]
    m2 = functools.reduce(jnp.maximum, rest)
    i2 = _first_of(rest, m2)
    return m1, m2, i1, i2


def _pick(idx, vals):
    out = vals[-1]
    for j in range(len(vals) - 2, -1, -1):
        out = jnp.where(idx == j, vals[j], out)
    return out


def _out_router_kernel(x_ref, a_ref, b_ref, w_ref, g1_ref, ng_ref, sc_ref, sh_ref,
                       rw_ref, rb_ref, x1_ref, h_ref, comb_ref):
    half = a_ref.shape[1]
    mix = _dot(a_ref[...], w_ref[0:half, :]) + _dot(b_ref[...], w_ref[half:, :])
    x1 = x_ref[...] + g1_ref[0] * mix
    x1_ref[...] = x1
    h = _modulated_rmsnorm(x1, ng_ref[...], sc_ref[0], sh_ref[0])
    h_ref[...] = h.astype(BF16)

    logits = _dot_nt(rw_ref[...], h, precision=lax.Precision.HIGHEST)
    ex = jnp.exp(logits - jnp.max(logits, axis=0, keepdims=True))
    scores = ex / jnp.sum(ex, axis=0, keepdims=True)
    sel = scores + rb_ref[...]
    srow = [scores[e:e + 1, :] for e in range(N_EXPERTS)]
    lrow = [sel[e:e + 1, :] for e in range(N_EXPERTS)]
    tops = [_top2(lrow[g * EXPERTS_PER_GROUP:(g + 1) * EXPERTS_PER_GROUP])
            for g in range(N_EXPERT_GROUPS)]
    gscore = [tp[0] + tp[1] for tp in tops]
    gidx = _first_of(gscore, functools.reduce(jnp.maximum, gscore))
    e1 = gidx * EXPERTS_PER_GROUP + _pick(gidx, [tp[2] for tp in tops])
    e2 = gidx * EXPERTS_PER_GROUP + _pick(gidx, [tp[3] for tp in tops])
    w1 = _pick(e1, srow)
    w2 = _pick(e2, srow)
    tot = w1 + w2
    w1 = w1 / tot
    w2 = w2 / tot
    for e in range(N_EXPERTS):
        comb_ref[e:e + 1, :] = jnp.where(e1 == e, w1, 0.0) + jnp.where(e2 == e, w2, 0.0)


def _out_router(x2d, a, b, w_out, g1, ng, sc, sh, rw_t, rb, seq):
    t, d = x2d.shape
    per_b = seq // TM
    vec = pl.BlockSpec((1, 1, d), lambda i: (i // per_b, 0, 0))
    full = lambda shape: pl.BlockSpec(shape, lambda i: (0, 0))
    half = a.shape[1]
    return pl.pallas_call(
        _out_router_kernel,
        grid=(t // TM,),
        in_specs=[
            pl.BlockSpec((TM, d), lambda i: (i, 0)),
            pl.BlockSpec((TM, half), lambda i: (i, 0)),
            pl.BlockSpec((TM, half), lambda i: (i, 0)),
            full((2 * half, d)),
            vec, full((1, d)), vec, vec,
            full((N_EXPERTS, d)), full((N_EXPERTS, 1)),
        ],
        out_specs=[
            pl.BlockSpec((TM, d), lambda i: (i, 0)),
            pl.BlockSpec((TM, d), lambda i: (i, 0)),
            pl.BlockSpec((N_EXPERTS, TM), lambda i: (0, i)),
        ],
        out_shape=[
            jax.ShapeDtypeStruct((t, d), F32),
            jax.ShapeDtypeStruct((t, d), BF16),
            jax.ShapeDtypeStruct((N_EXPERTS, t), F32),
        ],
        compiler_params=_cparams("arbitrary"),
        name="out_router",
    )(x2d, a, b, w_out, g1, ng, sc, sh, rw_t, rb)


def _moe_kernel(final, x1_ref, h_ref, comb_ref, wg_ref, wu_ref, wd_ref, g2_ref, fn_ref,
                o_ref, acc):
    e = pl.program_id(1)

    @pl.when(e == 0)
    def _():
        acc[...] = jnp.zeros_like(acc)

    h = h_ref[...]
    gate = _dot(h, wg_ref[0])
    up = _dot(h, wu_ref[0])
    act = (gate * _sigmoid(gate) * up).astype(BF16)
    comb = comb_ref[...]
    lane = lax.broadcasted_iota(jnp.int32, comb.shape, 1)
    scale = jnp.sum(jnp.where(lane == e, comb, 0.0), axis=-1, keepdims=True)
    acc[...] += scale * _dot(act, wd_ref[0])

    @pl.when(e == N_EXPERTS - 1)
    def _():
        x2 = x1_ref[...] + g2_ref[0] * acc[...]
        if final:
            ms = jnp.mean(x2 * x2, axis=-1, keepdims=True)
            x2 = x2 * lax.rsqrt(ms + RMS_EPS) * fn_ref[...]
        o_ref[...] = x2


def _moe(x1, h, comb, wg, wu, wd, g2, fn, seq, final):
    t, d = x1.shape
    per_b = seq // TM_MOE
    wspec = lambda shape: pl.BlockSpec(shape, lambda i, e: (e, 0, 0))
    return pl.pallas_call(
        functools.partial(_moe_kernel, final),
        grid=(t // TM_MOE, N_EXPERTS),
        in_specs=[
            pl.BlockSpec((TM_MOE, d), lambda i, e: (i, 0)),
            pl.BlockSpec((TM_MOE, d), lambda i, e: (i, 0)),
            pl.BlockSpec((TM_MOE, N_EXPERTS), lambda i, e: (i, 0)),
            wspec((1, d, D_EXPERT)), wspec((1, d, D_EXPERT)), wspec((1, D_EXPERT, d)),
            pl.BlockSpec((1, 1, d), lambda i, e: (i // per_b, 0, 0)),
            pl.BlockSpec((1, d), lambda i, e: (0, 0)),
        ],
        out_specs=pl.BlockSpec((TM_MOE, d), lambda i, e: (i, 0)),
        out_shape=jax.ShapeDtypeStruct((t, d), F32),
        scratch_shapes=[pltpu.VMEM((TM_MOE, d), F32)],
        compiler_params=_cparams("arbitrary", "arbitrary"),
        name="moe_final" if final else "moe",
    )(x1, h, comb, wg, wu, wd, g2, fn)


def _log_sigmoid(x):
    return jnp.minimum(x, 0.0) - jnp.log1p(jnp.exp(-jnp.abs(x)))


def _odd_in_kernel(x_ref, g_ref, sc_ref, sh_ref, w_ref, gw_ref, gb_ref,
                   q_ref, k_ref, v_ref, sr_ref, la_ref, up_ref):
    h = _modulated_rmsnorm(x_ref[...], g_ref[...], sc_ref[0], sh_ref[0]).astype(BF16)
    q_ref[...] = _dot(h, w_ref[:, 0:512]) * (GLA_DK ** -0.5)
    k_ref[...] = _dot(h, w_ref[:, 512:1024])
    v_ref[...] = _dot(h, w_ref[:, 1024:1536]).astype(BF16)
    r = _dot(h, w_ref[:, 1536:2048])
    sr_ref[...] = r * _sigmoid(r)
    up_ref[...] = _dot(h, w_ref[:, 2048:2560])
    a_low = _dot(h, w_ref[:, 2560:2688]).astype(BF16)
    la_ref[...] = _log_sigmoid(_dot(a_low, gw_ref[...]) + gb_ref[...]) * (1.0 / GLA_TAU)


def _odd_in(x2d, g, sc, sh, w_cat, gate_w, gate_b, seq):
    t, d = x2d.shape
    n = w_cat.shape[1]
    per_b = seq // TM
    vec = pl.BlockSpec((1, 1, d), lambda i: (i // per_b, 0, 0))
    full = lambda shape: pl.BlockSpec(shape, lambda i: (0, 0))
    out = lambda dt: jax.ShapeDtypeStruct((t, 512), dt)
    return pl.pallas_call(
        _odd_in_kernel,
        grid=(t // TM,),
        in_specs=[
            pl.BlockSpec((TM, d), lambda i: (i, 0)),
            full((1, d)), vec, vec, full((d, n)),
            full((LANES, 512)), full((1, 512)),
        ],
        out_specs=[pl.BlockSpec((TM, 512), lambda i: (i, 0))] * 6,
        out_shape=[out(F32), out(F32), out(BF16), out(F32), out(F32), out(F32)],
        compiler_params=_cparams("arbitrary"),
        name="odd_in",
    )(x2d, g, sc, sh, w_cat, gate_w, gate_b)


def _gla_kernel(q_ref, k_ref, la_ref, v_ref, sr_ref, g_ref, tri_ref, o_ref):
    seq = q_ref.shape[0]
    tri = tri_ref[...]
    row = lax.broadcasted_iota(jnp.int32, (GLA_CHUNK, GLA_CHUNK), 0)
    col = lax.broadcasted_iota(jnp.int32, (GLA_CHUNK, GLA_CHUNK), 1)
    causal = col <= row

    def chunk(n, state_t):
        rows = pl.ds(pl.multiple_of(n * GLA_CHUNK, GLA_CHUNK), GLA_CHUNK)
        b = _split_dot_left(tri, la_ref[rows, :])
        b_last = b[GLA_CHUNK - 1:GLA_CHUNK, :]
        k = k_ref[rows, :]
        v = v_ref[rows, :]
        q_in = (q_ref[rows, :] * jnp.exp(b)).astype(BF16)
        k_in = (k * jnp.exp(-b)).astype(BF16)
        k_dec = (k * jnp.exp(b_last - b)).astype(BF16)
        scores = jnp.where(causal, _dot_nt(q_in, k_in), 0.0).astype(BF16)
        o = _dot(scores, v) + _dot_nt(q_in, state_t.astype(BF16))
        o = o * lax.rsqrt(jnp.mean(o * o, axis=-1, keepdims=True) + RMS_EPS)
        o_ref[rows, :] = (o * g_ref[...] * sr_ref[rows, :]).astype(BF16)
        return state_t * jnp.exp(b_last) + _dot_tn(v, k_dec)

    lax.fori_loop(0, seq // GLA_CHUNK, chunk, jnp.zeros((GLA_DV, LANES), F32))


def _gla(q, k, la, v, sr, norm_g, bsz, seq):
    t = q.shape[0]
    idx = jnp.arange(GLA_CHUNK)
    tri = (idx[:, None] >= idx[None, :]).astype(BF16)
    blk = pl.BlockSpec((seq, LANES), lambda b, h: (b, h))
    return pl.pallas_call(
        _gla_kernel,
        grid=(bsz, GLA_HEADS),
        in_specs=[blk, blk, blk, blk, blk,
                  pl.BlockSpec((1, LANES), lambda b, h: (0, h)),
                  pl.BlockSpec((GLA_CHUNK, GLA_CHUNK), lambda b, h: (0, 0))],
        out_specs=blk,
        out_shape=jax.ShapeDtypeStruct((t, GLA_HEADS * GLA_DV), BF16),
        compiler_params=_cparams("arbitrary", "arbitrary"),
        name="gla",
    )(q, k, la, v, sr, norm_g.reshape(1, -1), tri)


def _pool_kernel(u_ref, halo_ref, w_ref, b_ref, s_ref, o_ref, buf):
    i = pl.program_id(1)
    buf[0:POOL_HALO, :] = jnp.where(i > 0, halo_ref[...], 0.0)
    buf[POOL_HALO:, :] = u_ref[...]
    pos = (i * TS_POOL + 1 + lax.broadcasted_iota(jnp.int32, (TS_POOL, 1), 0)).astype(F32)
    for gi, win in enumerate(POOL_WINDOWS):
        ls = slice(gi * LANES, (gi + 1) * LANES)
        tok = buf[POOL_HALO:, ls]
        tot = tok
        for back in range(1, win):
            tot = tot + buf[POOL_HALO - back:POOL_HALO - back + TS_POOL, ls]
        p = tot / jnp.minimum(pos, float(win)) - tok
        y = _dot(p.astype(BF16), w_ref[gi]) + b_ref[:, ls]
        o_ref[:, ls] = (y * s_ref[:, ls]).astype(BF16)


def _pool(u2d, pool_w, pool_b, pool_scale, bsz, seq):
    t = u2d.shape[0]
    per_b = seq // TS_POOL
    ratio = TS_POOL // POOL_HALO
    full = lambda shape: pl.BlockSpec(shape, lambda b, i: (0,) * len(shape))
    return pl.pallas_call(
        _pool_kernel,
        grid=(bsz, per_b),
        in_specs=[
            pl.BlockSpec((TS_POOL, D_POOL), lambda b, i: (b * per_b + i, 0)),
            pl.BlockSpec((POOL_HALO, D_POOL),
                         lambda b, i: (jnp.maximum((b * per_b + i) * ratio - 1, 0), 0)),
            full((len(POOL_WINDOWS), LANES, LANES)), full((1, D_POOL)), full((1, D_POOL)),
        ],
        out_specs=pl.BlockSpec((TS_POOL, D_POOL), lambda b, i: (b * per_b + i, 0)),
        out_shape=jax.ShapeDtypeStruct((t, D_POOL), BF16),
        scratch_shapes=[pltpu.VMEM((TS_POOL + POOL_HALO, D_POOL), F32)],
        compiler_params=_cparams("arbitrary", "arbitrary"),
        name="pool_mixer",
    )(u2d, u2d, pool_w.astype(BF16), pool_b.reshape(1, D_POOL), pool_scale.reshape(1, D_POOL))


def _pad_heads(w, heads, dim):
    lead = w.shape[:-1]
    w = w.reshape(lead + (heads, dim))
    w = jnp.pad(w, [(0, 0)] * len(lead) + [(0, 0), (0, LANES - dim)])
    return w.reshape(lead + (heads * LANES,))


def _odd_weights(w_in, gate_w, gate_b):
    dkt = GLA_HEADS * GLA_DK
    dvt = GLA_HEADS * GLA_DV
    o = [0, dkt, 2 * dkt, 2 * dkt + dvt, 2 * dkt + 2 * dvt, 2 * dkt + 2 * dvt + GLA_RANK]
    wq, wk, wv, wr, wa, wp = [w_in[:, o[j]:(o + [w_in.shape[1]])[j + 1]] for j in range(6)]
    w_cat = jnp.concatenate([
        _pad_heads(wq, GLA_HEADS, GLA_DK), _pad_heads(wk, GLA_HEADS, GLA_DK), wv, wr, wp,
        jnp.pad(wa, ((0, 0), (0, LANES - GLA_RANK)))], axis=1).astype(BF16)
    gw = jnp.pad(_pad_heads(gate_w, GLA_HEADS, GLA_DK), ((0, LANES - GLA_RANK), (0, 0))).astype(BF16)
    gb = _pad_heads(gate_b, GLA_HEADS, GLA_DK).reshape(1, -1)
    return w_cat, gw, gb


def kernel(x, c, ada_w, ada_b, norm_mix, norm_ffn, w_in_even, w_out_even, conv_w, conv_b, conv_norm_g, conv_norm_b, w_in_odd, w_out_odd, gla_gate_w, gla_gate_b, gla_norm_g, pool_w, pool_b, pool_scale, router_w, router_bias, moe_w_gate, moe_w_up, moe_w_down, final_norm):
    bsz, seq, d = x.shape
    t = bsz * seq
    depth = ada_w.shape[0]
    mod = _ada(c, ada_w, ada_b)
    rw_t = router_w.T
    rb = router_bias.reshape(N_EXPERTS, 1)
    fn = final_norm.reshape(1, d)
    x2d = x.reshape(t, d)
    for l in range(depth):
        sh1, sc1, g1, sh2, sc2, g2 = [mod[l, :, j * d:(j + 1) * d].reshape(bsz, 1, d)
                                      for j in range(6)]
        i = l // 2
        nm = norm_mix[l].reshape(1, d)
        if l % 2 == 0:
            y, q, k, v = _even_in(x2d, nm, sc1, sh1, w_in_even[i].astype(BF16), seq)
            a = _conv(y, conv_w[i], conv_b[i], conv_norm_g[i], conv_norm_b[i], bsz, seq)
            b = _sb_attention(q, k, v, bsz, seq)
            w_out = w_out_even[i]
        else:
            w_cat, gw, gb = _odd_weights(w_in_odd[i], gla_gate_w[i], gla_gate_b[i])
            q, k, v, sr, la, up = _odd_in(x2d, nm, sc1, sh1, w_cat, gw, gb, seq)
            a = _gla(q, k, la, v, sr, gla_norm_g[i], bsz, seq)
            b = _pool(up, pool_w[i], pool_b[i], pool_scale[i], bsz, seq)
            w_out = w_out_odd[i]
        x1, h, comb_t = _out_router(x2d, a, b, w_out.astype(BF16), g1,
                                    norm_ffn[l].reshape(1, d), sc2, sh2, rw_t, rb, seq)
        x2d = _moe(x1, h, comb_t.T, moe_w_gate[l].astype(BF16), moe_w_up[l].astype(BF16),
                   moe_w_down[l].astype(BF16), g2, fn, seq, final=(l == depth - 1))
    return x2d.reshape(bsz, seq, d)
```

```python
import functools

import jax
import jax.numpy as jnp
from jax import lax
from jax.experimental import pallas as pl
from jax.experimental.pallas import tpu as pltpu

F32 = jnp.float32
BF16 = jnp.bfloat16

D_MODEL = 1024
D_CONV = 512
CONV_WIDTH = 31
CONV_GROUP_SIZE = 64
SB_HEADS = 8
SB_HEAD_DIM = 64
GLA_HEADS = 4
GLA_DK = 64
GLA_DV = 128
GLA_RANK = 16
GLA_TAU = 16.0
GLA_CHUNK = 64
D_POOL = 512
POOL_WINDOWS = (2, 4, 8, 16)
N_EXPERTS = 16
N_EXPERT_GROUPS = 4
EXPERTS_PER_GROUP = 4
D_EXPERT = 512
RMS_EPS = 1e-6
LN_EPS = 1e-5

LANES = 128
VMEM_LIMIT = 56 * 1024 * 1024

TM = 512
TS_CONV = 256
CONV_HALO = 32
CONV_ROWS = 64
TS_GLA = 1024
TS_POOL = 512
POOL_HALO = 16
TQ = 256
TM_MOE = 1024
SB_DEAD = -120.0


def _cparams(*sem):
    return pltpu.CompilerParams(dimension_semantics=sem, vmem_limit_bytes=VMEM_LIMIT)


def _sigmoid(x):
    return 1.0 / (1.0 + jnp.exp(-x))


def _dot(a, b):
    return jnp.dot(a, b, preferred_element_type=F32)


def _dot_nt(a, b, **kw):
    return lax.dot_general(a, b, (((1,), (1,)), ((), ())), preferred_element_type=F32, **kw)


def _dot_tn(a, b):
    return lax.dot_general(a, b, (((0,), (0,)), ((), ())), preferred_element_type=F32)


def _split_dot(x, m):
    hi = x.astype(BF16)
    lo = (x - hi.astype(F32)).astype(BF16)
    return _dot(hi, m) + _dot(lo, m)


def _split_dot_left(m, x):
    hi = x.astype(BF16)
    lo = (x - hi.astype(F32)).astype(BF16)
    return _dot(m, hi) + _dot(m, lo)


def _modulated_rmsnorm(x, g, sc, sh):
    ms = jnp.mean(x * x, axis=-1, keepdims=True)
    return (x * lax.rsqrt(ms + RMS_EPS) * g) * (1.0 + sc) + sh


def _ada_kernel(c_ref, w_ref, b_ref, o_ref):
    c = c_ref[...]
    cond = c * _sigmoid(c)
    o_ref[0] = jnp.dot(cond, w_ref[0], preferred_element_type=F32,
                       precision=lax.Precision.HIGHEST) + b_ref[0]


def _ada(c, ada_w, ada_b):
    depth, d, n = ada_w.shape
    bsz = c.shape[0]
    tn = 1536
    return pl.pallas_call(
        _ada_kernel,
        grid=(depth, n // tn),
        in_specs=[
            pl.BlockSpec((bsz, d), lambda l, j: (0, 0)),
            pl.BlockSpec((1, d, tn), lambda l, j: (l, 0, j)),
            pl.BlockSpec((1, 1, tn), lambda l, j: (l, 0, j)),
        ],
        out_specs=pl.BlockSpec((1, bsz, tn), lambda l, j: (l, 0, j)),
        out_shape=jax.ShapeDtypeStruct((depth, bsz, n), F32),
        compiler_params=_cparams("arbitrary", "arbitrary"),
        name="ada_mod",
    )(c, ada_w, ada_b.reshape(depth, 1, n))


def _even_in_kernel(x_ref, g_ref, sc_ref, sh_ref, w_ref, y_ref, q_ref, k_ref, v_ref):
    h = _modulated_rmsnorm(x_ref[...], g_ref[...], sc_ref[0], sh_ref[0]).astype(BF16)
    val = _dot(h, w_ref[:, 0:512])
    gate = _dot(h, w_ref[:, 512:1024])
    y_ref[...] = val * _sigmoid(gate)
    q_ref[...] = (_dot(h, w_ref[:, 1024:1536]) * (SB_HEAD_DIM ** -0.5)).astype(BF16)
    k_ref[...] = _dot(h, w_ref[:, 1536:2048]).astype(BF16)
    v_ref[...] = _dot(h, w_ref[:, 2048:2560]).astype(BF16)


def _even_in(x2d, g, sc, sh, w_in, seq):
    t, d = x2d.shape
    n = w_in.shape[1]
    per_b = seq // TM
    vec = pl.BlockSpec((1, 1, d), lambda i: (i // per_b, 0, 0))
    out = lambda dt: jax.ShapeDtypeStruct((t, 512), dt)
    return pl.pallas_call(
        _even_in_kernel,
        grid=(t // TM,),
        in_specs=[
            pl.BlockSpec((TM, d), lambda i: (i, 0)),
            pl.BlockSpec((1, d), lambda i: (0, 0)),
            vec, vec,
            pl.BlockSpec((d, n), lambda i: (0, 0)),
        ],
        out_specs=[pl.BlockSpec((TM, 512), lambda i: (i, 0))] * 4,
        out_shape=[out(F32), out(BF16), out(BF16), out(BF16)],
        compiler_params=_cparams("arbitrary"),
        name="even_in",
    )(x2d, g, sc, sh, w_in)


def _conv_kernel(y_ref, halo_ref, w_ref, b_ref, ng_ref, nb_ref, avg_ref, o_ref, buf, cv):
    i = pl.program_id(1)
    buf[0:CONV_HALO, :] = jnp.where(i > 0, halo_ref[...], 0.0)
    buf[CONV_HALO:, :] = y_ref[...]
    first = CONV_HALO - (CONV_WIDTH - 1)
    for rc in range(TS_CONV // CONV_ROWS):
        r0 = rc * CONV_ROWS
        for lc in range(D_CONV // LANES):
            ls = slice(lc * LANES, (lc + 1) * LANES)
            acc = jnp.zeros((CONV_ROWS, LANES), F32)
            for j in range(CONV_WIDTH):
                s = r0 + first + j
                acc = acc + w_ref[j:j + 1, ls] * buf[s:s + CONV_ROWS, ls]
            cv[r0:r0 + CONV_ROWS, ls] = acc + b_ref[:, ls]
    y = cv[...]
    avg = avg_ref[...]
    d = y - _split_dot(y, avg)
    var = _split_dot(d * d, avg)
    yn = d * lax.rsqrt(var + LN_EPS) * ng_ref[...] + nb_ref[...]
    o_ref[...] = (yn * _sigmoid(yn)).astype(BF16)


def _conv(y2d, conv_w, conv_b, norm_g, norm_b, bsz, seq):
    t = y2d.shape[0]
    per_b = seq // TS_CONV
    ratio = TS_CONV // CONV_HALO
    gid = jnp.arange(D_CONV) // CONV_GROUP_SIZE
    avg = jnp.where(gid[:, None] == gid[None, :], 1.0 / CONV_GROUP_SIZE, 0.0).astype(BF16)
    w_pad = jnp.pad(conv_w, ((0, 32 - CONV_WIDTH), (0, 0)))
    row = lambda a: a.reshape(1, D_CONV)
    full = lambda shape: pl.BlockSpec(shape, lambda b, i: (0, 0))
    return pl.pallas_call(
        _conv_kernel,
        grid=(bsz, per_b),
        in_specs=[
            pl.BlockSpec((TS_CONV, D_CONV), lambda b, i: (b * per_b + i, 0)),
            pl.BlockSpec((CONV_HALO, D_CONV),
                         lambda b, i: (jnp.maximum((b * per_b + i) * ratio - 1, 0), 0)),
            full((32, D_CONV)), full((1, D_CONV)), full((1, D_CONV)), full((1, D_CONV)),
            full((D_CONV, D_CONV)),
        ],
        out_specs=pl.BlockSpec((TS_CONV, D_CONV), lambda b, i: (b * per_b + i, 0)),
        out_shape=jax.ShapeDtypeStruct((t, D_CONV), BF16),
        scratch_shapes=[pltpu.VMEM((TS_CONV + CONV_HALO, D_CONV), F32),
                        pltpu.VMEM((TS_CONV, D_CONV), F32)],
        compiler_params=_cparams("arbitrary", "arbitrary"),
        name="conv_module",
    )(y2d, y2d, w_pad, row(conv_b), row(norm_g), row(norm_b), avg)


def _sb_kernel(q_ref, k_ref, v_ref, u_ref, o_ref, acc_ref, cr_ref):
    qi = pl.program_id(2)
    q = q_ref[...]
    lane = lax.broadcasted_iota(jnp.int32, (TQ, LANES), 1)
    row = lax.broadcasted_iota(jnp.int32, (TQ, TQ), 0)
    col = lax.broadcasted_iota(jnp.int32, (TQ, TQ), 1)
    qm = [jnp.where(lane < SB_HEAD_DIM, q, jnp.zeros_like(q)),
          jnp.where(lane >= SB_HEAD_DIM, q, jnp.zeros_like(q))]
    acc_ref[...] = jnp.zeros_like(acc_ref)
    cr_ref[...] = jnp.zeros_like(cr_ref)

    def key_blocks(first_kb, count):
        for hd in range(2):
            cr = cr_ref[hd]
            acc = acc_ref[hd]
            for step in range(count):
                kb = first_kb - step
                ks = pl.multiple_of(kb * TQ, TQ)
                z = _dot_nt(qm[hd], k_ref[pl.ds(ks, TQ), :])
                l1p = jnp.log(1.0 + jnp.exp(-jnp.abs(z)))
                strict = (col + (kb - qi) * TQ) < row
                log_keep = jnp.where(strict, -jnp.maximum(z, 0.0) - l1p, 0.0)
                later = _split_dot(log_keep, u_ref[...]) + cr
                a = jnp.where(strict, jnp.exp(jnp.minimum(z, 0.0) - l1p + later), 0.0)
                acc = acc + _dot(a.astype(BF16), v_ref[pl.ds(ks, TQ), :])
                cr = cr + jnp.sum(log_keep, axis=-1, keepdims=True)
            cr_ref[hd] = cr
            acc_ref[hd] = acc

    def alive():
        return (jnp.max(cr_ref[...]) > SB_DEAD).astype(jnp.int32)

    @pl.when(qi == 0)
    def _():
        key_blocks(qi, 1)

    @pl.when(qi > 0)
    def _():
        key_blocks(qi, 2)

    def cond(carry):
        kb, live = carry
        return jnp.logical_and(kb >= 0, live > 0)

    def body(carry):
        kb, _ = carry
        key_blocks(kb, 1)
        return kb - 1, alive()

    lax.while_loop(cond, body, (qi - 2, alive()))
    o_ref[...] = jnp.where(lane < SB_HEAD_DIM, acc_ref[0], acc_ref[1]).astype(BF16)


def _sb_attention(q, k, v, bsz, seq):
    t = q.shape[0]
    nq = seq // TQ
    idx = jnp.arange(TQ)
    upper = (idx[:, None] > idx[None, :]).astype(BF16)
    kv = pl.BlockSpec((seq, LANES), lambda b, p, i: (b, p))
    return pl.pallas_call(
        _sb_kernel,
        grid=(bsz, SB_HEADS // 2, nq),
        in_specs=[
            pl.BlockSpec((TQ, LANES), lambda b, p, i: (b * nq + i, p)),
            kv, kv,
            pl.BlockSpec((TQ, TQ), lambda b, p, i: (0, 0)),
        ],
        out_specs=pl.BlockSpec((TQ, LANES), lambda b, p, i: (b * nq + i, p)),
        out_shape=jax.ShapeDtypeStruct((t, SB_HEADS * SB_HEAD_DIM), BF16),
        scratch_shapes=[pltpu.VMEM((2, TQ, LANES), F32), pltpu.VMEM((2, TQ, 1), F32)],
        compiler_params=_cparams("arbitrary", "arbitrary", "arbitrary"),
        name="stick_breaking",
    )(q, k, v, upper)


def _first_of(vals, target):
    idx = jnp.full(target.shape, len(vals) - 1, jnp.int32)
    for j in range(len(vals) - 2, -1, -1):
        idx = jnp.where(vals[j] == target, j, idx)
    return idx


def _top2(vals):
    m1 = functools.reduce(jnp.maximum, vals)
    i1 = _first_of(vals, m1)
    rest = [jnp.where(i1 == j, -jnp.inf, v) for j, v in enumerate(vals)]
    m2 = functools.reduce(jnp.maximum, rest)
    i2 = _first_of(rest, m2)
    return m1, m2, i1, i2


def _pick(idx, vals):
    out = vals[-1]
    for j in range(len(vals) - 2, -1, -1):
        out = jnp.where(idx == j, vals[j], out)
    return out


def _out_router_kernel(x_ref, a_ref, b_ref, w_ref, g1_ref, ng_ref, sc_ref, sh_ref,
                       rwh_ref, rwl_ref, rb_ref, x1_ref, h_ref, comb_ref):
    half = a_ref.shape[1]
    mix = _dot(a_ref[...], w_ref[0:half, :]) + _dot(b_ref[...], w_ref[half:, :])
    x1 = x_ref[...] + g1_ref[0] * mix
    x1_ref[...] = x1
    h = _modulated_rmsnorm(x1, ng_ref[...], sc_ref[0], sh_ref[0])
    hb = h.astype(BF16)
    h_ref[...] = hb

    hl = (h - hb.astype(F32)).astype(BF16)
    rwh = rwh_ref[...]
    logits = (_dot(hb, rwh) + _dot(hl, rwh) + _dot(hb, rwl_ref[...])).T[0:N_EXPERTS, :]
    ex = jnp.exp(logits - jnp.max(logits, axis=0, keepdims=True))
    scores = ex / jnp.sum(ex, axis=0, keepdims=True)
    sel = scores + rb_ref[...]
    srow = [scores[e:e + 1, :] for e in range(N_EXPERTS)]
    lrow = [sel[e:e + 1, :] for e in range(N_EXPERTS)]
    tops = [_top2(lrow[g * EXPERTS_PER_GROUP:(g + 1) * EXPERTS_PER_GROUP])
            for g in range(N_EXPERT_GROUPS)]
    gscore = [tp[0] + tp[1] for tp in tops]
    gidx = _first_of(gscore, functools.reduce(jnp.maximum, gscore))
    e1 = gidx * EXPERTS_PER_GROUP + _pick(gidx, [tp[2] for tp in tops])
    e2 = gidx * EXPERTS_PER_GROUP + _pick(gidx, [tp[3] for tp in tops])
    w1 = _pick(e1, srow)
    w2 = _pick(e2, srow)
    tot = w1 + w2
    w1 = w1 / tot
    w2 = w2 / tot
    for e in range(N_EXPERTS):
        comb_ref[e:e + 1, :] = jnp.where(e1 == e, w1, 0.0) + jnp.where(e2 == e, w2, 0.0)


def _out_router(x2d, a, b, w_out, g1, ng, sc, sh, rw_hi, rw_lo, rb, seq):
    t, d = x2d.shape
    per_b = seq // TM
    vec = pl.BlockSpec((1, 1, d), lambda i: (i // per_b, 0, 0))
    full = lambda shape: pl.BlockSpec(shape, lambda i: (0, 0))
    half = a.shape[1]
    return pl.pallas_call(
        _out_router_kernel,
        grid=(t // TM,),
        in_specs=[
            pl.BlockSpec((TM, d), lambda i: (i, 0)),
            pl.BlockSpec((TM, half), lambda i: (i, 0)),
            pl.BlockSpec((TM, half), lambda i: (i, 0)),
            full((2 * half, d)),
            vec, full((1, d)), vec, vec,
            full((d, LANES)), full((d, LANES)), full((N_EXPERTS, 1)),
        ],
        out_specs=[
            pl.BlockSpec((TM, d), lambda i: (i, 0)),
            pl.BlockSpec((TM, d), lambda i: (i, 0)),
            pl.BlockSpec((N_EXPERTS, TM), lambda i: (0, i)),
        ],
        out_shape=[
            jax.ShapeDtypeStruct((t, d), F32),
            jax.ShapeDtypeStruct((t, d), BF16),
            jax.ShapeDtypeStruct((N_EXPERTS, t), F32),
        ],
        compiler_params=_cparams("arbitrary"),
        name="out_router",
    )(x2d, a, b, w_out, g1, ng, sc, sh, rw_hi, rw_lo, rb)


def _moe_kernel(final, x1_ref, h_ref, comb_ref, wg_ref, wu_ref, wd_ref, g2_ref, fn_ref,
                o_ref, acc):
    e = pl.program_id(1)

    @pl.when(e == 0)
    def _():
        acc[...] = jnp.zeros_like(acc)

    h = h_ref[...]
    gate = _dot(h, wg_ref[0])
    up = _dot(h, wu_ref[0])
    act = (gate * _sigmoid(gate) * up).astype(BF16)
    comb = comb_ref[...]
    lane = lax.broadcasted_iota(jnp.int32, comb.shape, 1)
    scale = jnp.sum(jnp.where(lane == e, comb, 0.0), axis=-1, keepdims=True)
    acc[...] += scale * _dot(act, wd_ref[0])

    @pl.when(e == N_EXPERTS - 1)
    def _():
        x2 = x1_ref[...] + g2_ref[0] * acc[...]
        if final:
            ms = jnp.mean(x2 * x2, axis=-1, keepdims=True)
            x2 = x2 * lax.rsqrt(ms + RMS_EPS) * fn_ref[...]
        o_ref[...] = x2


def _moe(x1, h, comb, wg, wu, wd, g2, fn, seq, final):
    t, d = x1.shape
    per_b = seq // TM_MOE
    wspec = lambda shape: pl.BlockSpec(shape, lambda i, e: (e, 0, 0))
    return pl.pallas_call(
        functools.partial(_moe_kernel, final),
        grid=(t // TM_MOE, N_EXPERTS),
        in_specs=[
            pl.BlockSpec((TM_MOE, d), lambda i, e: (i, 0)),
            pl.BlockSpec((TM_MOE, d), lambda i, e: (i, 0)),
            pl.BlockSpec((TM_MOE, N_EXPERTS), lambda i, e: (i, 0)),
            wspec((1, d, D_EXPERT)), wspec((1, d, D_EXPERT)), wspec((1, D_EXPERT, d)),
            pl.BlockSpec((1, 1, d), lambda i, e: (i // per_b, 0, 0)),
            pl.BlockSpec((1, d), lambda i, e: (0, 0)),
        ],
        out_specs=pl.BlockSpec((TM_MOE, d), lambda i, e: (i, 0)),
        out_shape=jax.ShapeDtypeStruct((t, d), F32),
        scratch_shapes=[pltpu.VMEM((TM_MOE, d), F32)],
        compiler_params=_cparams("arbitrary", "arbitrary"),
        name="moe_final" if final else "moe",
    )(x1, h, comb, wg, wu, wd, g2, fn)


def _log_sigmoid(x):
    return jnp.minimum(x, 0.0) - jnp.log1p(jnp.exp(-jnp.abs(x)))


def _odd_in_kernel(x_ref, g_ref, sc_ref, sh_ref, w_ref, gw_ref, gb_ref,
                   q_ref, k_ref, v_ref, sr_ref, la_ref, up_ref):
    h = _modulated_rmsnorm(x_ref[...], g_ref[...], sc_ref[0], sh_ref[0]).astype(BF16)
    q_ref[...] = _dot(h, w_ref[:, 0:512]) * (GLA_DK ** -0.5)
    k_ref[...] = _dot(h, w_ref[:, 512:1024])
    v_ref[...] = _dot(h, w_ref[:, 1024:1536]).astype(BF16)
    r = _dot(h, w_ref[:, 1536:2048])
    sr_ref[...] = r * _sigmoid(r)
    up_ref[...] = _dot(h, w_ref[:, 2048:2560])
    a_low = _dot(h, w_ref[:, 2560:2688]).astype(BF16)
    la_ref[...] = _log_sigmoid(_dot(a_low, gw_ref[...]) + gb_ref[...]) * (1.0 / GLA_TAU)


def _odd_in(x2d, g, sc, sh, w_cat, gate_w, gate_b, seq):
    t, d = x2d.shape
    n = w_cat.shape[1]
    per_b = seq // TM
    vec = pl.BlockSpec((1, 1, d), lambda i: (i // per_b, 0, 0))
    full = lambda shape: pl.BlockSpec(shape, lambda i: (0, 0))
    out = lambda dt: jax.ShapeDtypeStruct((t, 512), dt)
    return pl.pallas_call(
        _odd_in_kernel,
        grid=(t // TM,),
        in_specs=[
            pl.BlockSpec((TM, d), lambda i: (i, 0)),
            full((1, d)), vec, vec, full((d, n)),
            full((LANES, 512)), full((1, 512)),
        ],
        out_specs=[pl.BlockSpec((TM, 512), lambda i: (i, 0))] * 6,
        out_shape=[out(F32), out(F32), out(BF16), out(F32), out(F32), out(F32)],
        compiler_params=_cparams("arbitrary"),
        name="odd_in",
    )(x2d, g, sc, sh, w_cat, gate_w, gate_b)


def _gla_kernel(q_ref, k_ref, la_ref, v_ref, sr_ref, g_ref, tri_ref, o_ref, st_ref):
    tri = tri_ref[...]
    row = lax.broadcasted_iota(jnp.int32, (GLA_CHUNK, GLA_CHUNK), 0)
    col = lax.broadcasted_iota(jnp.int32, (GLA_CHUNK, GLA_CHUNK), 1)
    causal = col <= row

    @pl.when(pl.program_id(1) == 0)
    def _():
        st_ref[...] = jnp.zeros_like(st_ref)

    def chunk(n, carry):
        rows = pl.ds(pl.multiple_of(n * GLA_CHUNK, GLA_CHUNK), GLA_CHUNK)
        for hd in range(GLA_HEADS):
            ls = slice(hd * LANES, (hd + 1) * LANES)
            b = _split_dot_left(tri, la_ref[rows, ls])
            b_last = b[GLA_CHUNK - 1:GLA_CHUNK, :]
            k = k_ref[rows, ls]
            v = v_ref[rows, ls]
            state_t = st_ref[hd]
            q_in = (q_ref[rows, ls] * jnp.exp(b)).astype(BF16)
            k_in = (k * jnp.exp(-b)).astype(BF16)
            k_dec = (k * jnp.exp(b_last - b)).astype(BF16)
            scores = jnp.where(causal, _dot_nt(q_in, k_in), 0.0).astype(BF16)
            o = _dot(scores, v) + _dot_nt(q_in, state_t.astype(BF16))
            o = o * lax.rsqrt(jnp.mean(o * o, axis=-1, keepdims=True) + RMS_EPS)
            o_ref[rows, ls] = (o * g_ref[:, ls] * sr_ref[rows, ls]).astype(BF16)
            st_ref[hd] = state_t * jnp.exp(b_last) + _dot_tn(v, k_dec)
        return carry

    lax.fori_loop(0, TS_GLA // GLA_CHUNK, chunk, 0)


def _gla(q, k, la, v, sr, norm_g, bsz, seq):
    t, width = q.shape
    per_b = seq // TS_GLA
    idx = jnp.arange(GLA_CHUNK)
    tri = (idx[:, None] >= idx[None, :]).astype(BF16)
    blk = pl.BlockSpec((TS_GLA, width), lambda b, i: (b * per_b + i, 0))
    return pl.pallas_call(
        _gla_kernel,
        grid=(bsz, per_b),
        in_specs=[blk, blk, blk, blk, blk,
                  pl.BlockSpec((1, width), lambda b, i: (0, 0)),
                  pl.BlockSpec((GLA_CHUNK, GLA_CHUNK), lambda b, i: (0, 0))],
        out_specs=blk,
        out_shape=jax.ShapeDtypeStruct((t, width), BF16),
        scratch_shapes=[pltpu.VMEM((GLA_HEADS, GLA_DV, LANES), F32)],
        compiler_params=_cparams("arbitrary", "arbitrary"),
        name="gla",
    )(q, k, la, v, sr, norm_g.reshape(1, -1), tri)


def _pool_kernel(u_ref, halo_ref, w_ref, b_ref, s_ref, o_ref, buf):
    i = pl.program_id(1)
    buf[0:POOL_HALO, :] = jnp.where(i > 0, halo_ref[...], 0.0)
    buf[POOL_HALO:, :] = u_ref[...]
    pos = (i * TS_POOL + 1 + lax.broadcasted_iota(jnp.int32, (TS_POOL, 1), 0)).astype(F32)
    for gi, win in enumerate(POOL_WINDOWS):
        ls = slice(gi * LANES, (gi + 1) * LANES)
        tok = buf[POOL_HALO:, ls]
        tot = tok
        for back in range(1, win):
            tot = tot + buf[POOL_HALO - back:POOL_HALO - back + TS_POOL, ls]
        p = tot / jnp.minimum(pos, float(win)) - tok
        y = _dot(p.astype(BF16), w_ref[gi]) + b_ref[:, ls]
        o_ref[:, ls] = (y * s_ref[:, ls]).astype(BF16)


def _pool(u2d, pool_w, pool_b, pool_scale, bsz, seq):
    t = u2d.shape[0]
    per_b = seq // TS_POOL
    ratio = TS_POOL // POOL_HALO
    full = lambda shape: pl.BlockSpec(shape, lambda b, i: (0,) * len(shape))
    return pl.pallas_call(
        _pool_kernel,
        grid=(bsz, per_b),
        in_specs=[
            pl.BlockSpec((TS_POOL, D_POOL), lambda b, i: (b * per_b + i, 0)),
            pl.BlockSpec((POOL_HALO, D_POOL),
                         lambda b, i: (jnp.maximum((b * per_b + i) * ratio - 1, 0), 0)),
            full((len(POOL_WINDOWS), LANES, LANES)), full((1, D_POOL)), full((1, D_POOL)),
        ],
        out_specs=pl.BlockSpec((TS_POOL, D_POOL), lambda b, i: (b * per_b + i, 0)),
        out_shape=jax.ShapeDtypeStruct((t, D_POOL), BF16),
        scratch_shapes=[pltpu.VMEM((TS_POOL + POOL_HALO, D_POOL), F32)],
        compiler_params=_cparams("arbitrary", "arbitrary"),
        name="pool_mixer",
    )(u2d, u2d, pool_w.astype(BF16), pool_b.reshape(1, D_POOL), pool_scale.reshape(1, D_POOL))


def _pad_heads(w, heads, dim):
    lead = w.shape[:-1]
    w = w.reshape(lead + (heads, dim))
    w = jnp.pad(w, [(0, 0)] * len(lead) + [(0, 0), (0, LANES - dim)])
    return w.reshape(lead + (heads * LANES,))


def _odd_weights(w_in, gate_w, gate_b):
    dkt = GLA_HEADS * GLA_DK
    dvt = GLA_HEADS * GLA_DV
    o = [0, dkt, 2 * dkt, 2 * dkt + dvt, 2 * dkt + 2 * dvt, 2 * dkt + 2 * dvt + GLA_RANK]
    wq, wk, wv, wr, wa, wp = [w_in[:, o[j]:(o + [w_in.shape[1]])[j + 1]] for j in range(6)]
    w_cat = jnp.concatenate([
        _pad_heads(wq, GLA_HEADS, GLA_DK), _pad_heads(wk, GLA_HEADS, GLA_DK), wv, wr, wp,
        jnp.pad(wa, ((0, 0), (0, LANES - GLA_RANK)))], axis=1).astype(BF16)
    gw = jnp.pad(_pad_heads(gate_w, GLA_HEADS, GLA_DK), ((0, LANES - GLA_RANK), (0, 0))).astype(BF16)
    gb = _pad_heads(gate_b, GLA_HEADS, GLA_DK).reshape(1, -1)
    return w_cat, gw, gb


def kernel(x, c, ada_w, ada_b, norm_mix, norm_ffn, w_in_even, w_out_even, conv_w, conv_b, conv_norm_g, conv_norm_b, w_in_odd, w_out_odd, gla_gate_w, gla_gate_b, gla_norm_g, pool_w, pool_b, pool_scale, router_w, router_bias, moe_w_gate, moe_w_up, moe_w_down, final_norm):
    bsz, seq, d = x.shape
    t = bsz * seq
    depth = ada_w.shape[0]
    mod = _ada(c, ada_w, ada_b)
    rw_pad = jnp.pad(router_w, ((0, 0), (0, LANES - N_EXPERTS)))
    rw_hi = rw_pad.astype(BF16)
    rw_lo = (rw_pad - rw_hi.astype(F32)).astype(BF16)
    rb =router_bias.reshape(N_EXPERTS, 1)
    fn = final_norm.reshape(1, d)
    x2d = x.reshape(t, d)
    for l in range(depth):
        sh1, sc1, g1, sh2, sc2, g2 = [mod[l, :, j * d:(j + 1) * d].reshape(bsz, 1, d)
                                      for j in range(6)]
        i = l // 2
        nm = norm_mix[l].reshape(1, d)
        if l % 2 == 0:
            y, q, k, v = _even_in(x2d, nm, sc1, sh1, w_in_even[i].astype(BF16), seq)
            a = _conv(y, conv_w[i], conv_b[i], conv_norm_g[i], conv_norm_b[i], bsz, seq)
            b = _sb_attention(q, k, v, bsz, seq)
            w_out = w_out_even[i]
        else:
            w_cat, gw, gb = _odd_weights(w_in_odd[i], gla_gate_w[i], gla_gate_b[i])
            q, k, v, sr, la, up = _odd_in(x2d, nm, sc1, sh1, w_cat, gw, gb, seq)
            a = _gla(q, k, la, v, sr, gla_norm_g[i], bsz, seq)
            b = _pool(up, pool_w[i], pool_b[i], pool_scale[i], bsz, seq)
            w_out = w_out_odd[i]
        x1, h, comb_t = _out_router(x2d, a, b, w_out.astype(BF16), g1,
                                    norm_ffn[l].reshape(1, d), sc2, sh2, rw_hi, rw_lo, rb, seq)
        x2d = _moe(x1, h, comb_t.T, moe_w_gate[l].astype(BF16), moe_w_up[l].astype(BF16),
                   moe_w_down[l].astype(BF16), g2, fn, seq, final=(l == depth - 1))
    return x2d.reshape(bsz, seq, d)
```

```python
import functools

import jax
import jax.numpy as jnp
from jax import lax
from jax.experimental import pallas as pl
from jax.experimental.pallas import tpu as pltpu

F32 = jnp.float32
BF16 = jnp.bfloat16

D_MODEL = 1024
D_CONV = 512
CONV_WIDTH = 31
CONV_GROUP_SIZE = 64
SB_HEADS = 8
SB_HEAD_DIM = 64
GLA_HEADS = 4
GLA_DK = 64
GLA_DV = 128
GLA_RANK = 16
GLA_TAU = 16.0
GLA_CHUNK = 64
D_POOL = 512
POOL_WINDOWS = (2, 4, 8, 16)
N_EXPERTS = 16
N_EXPERT_GROUPS = 4
EXPERTS_PER_GROUP = 4
D_EXPERT = 512
RMS_EPS = 1e-6
LN_EPS = 1e-5

LANES = 128
VMEM_LIMIT = 56 * 1024 * 1024

TM = 512
TS_CONV = 256
CONV_HALO = 32
CONV_ROWS = 64
TS_GLA = 1024
TS_POOL = 512
POOL_HALO = 16
TQ = 256
TM_MOE = 256
SB_DEAD = -120.0


def _cparams(*sem):
    return pltpu.CompilerParams(dimension_semantics=sem, vmem_limit_bytes=VMEM_LIMIT)


def _sigmoid(x):
    return 1.0 / (1.0 + jnp.exp(-x))


def _dot(a, b):
    return jnp.dot(a, b, preferred_element_type=F32)


def _dot_nt(a, b, **kw):
    return lax.dot_general(a, b, (((1,), (1,)), ((), ())), preferred_element_type=F32, **kw)


def _dot_tn(a, b):
    return lax.dot_general(a, b, (((0,), (0,)), ((), ())), preferred_element_type=F32)


def _split_dot(x, m):
    hi = x.astype(BF16)
    lo = (x - hi.astype(F32)).astype(BF16)
    return _dot(hi, m) + _dot(lo, m)


def _split_dot_left(m, x):
    hi = x.astype(BF16)
    lo = (x - hi.astype(F32)).astype(BF16)
    return _dot(m, hi) + _dot(m, lo)


def _modulated_rmsnorm(x, g, sc, sh):
    ms = jnp.mean(x * x, axis=-1, keepdims=True)
    return (x * lax.rsqrt(ms + RMS_EPS) * g) * (1.0 + sc) + sh


def _ada_kernel(c_ref, w_ref, b_ref, o_ref):
    c = c_ref[...]
    cond = c * _sigmoid(c)
    o_ref[0] = jnp.dot(cond, w_ref[0], preferred_element_type=F32,
                       precision=lax.Precision.HIGHEST) + b_ref[0]


def _ada(c, ada_w, ada_b):
    depth, d, n = ada_w.shape
    bsz = c.shape[0]
    tn = 1536
    return pl.pallas_call(
        _ada_kernel,
        grid=(depth, n // tn),
        in_specs=[
            pl.BlockSpec((bsz, d), lambda l, j: (0, 0)),
            pl.BlockSpec((1, d, tn), lambda l, j: (l, 0, j)),
            pl.BlockSpec((1, 1, tn), lambda l, j: (l, 0, j)),
        ],
        out_specs=pl.BlockSpec((1, bsz, tn), lambda l, j: (l, 0, j)),
        out_shape=jax.ShapeDtypeStruct((depth, bsz, n), F32),
        compiler_params=_cparams("arbitrary", "arbitrary"),
        name="ada_mod",
    )(c, ada_w, ada_b.reshape(depth, 1, n))


def _even_in_kernel(x_ref, g_ref, sc_ref, sh_ref, w_ref, y_ref, q_ref, k_ref, v_ref):
    h = _modulated_rmsnorm(x_ref[...], g_ref[...], sc_ref[0], sh_ref[0]).astype(BF16)
    val = _dot(h, w_ref[:, 0:512])
    gate = _dot(h, w_ref[:, 512:1024])
    y_ref[...] = val * _sigmoid(gate)
    q_ref[...] = (_dot(h, w_ref[:, 1024:1536]) * (SB_HEAD_DIM ** -0.5)).astype(BF16)
    k_ref[...] = _dot(h, w_ref[:, 1536:2048]).astype(BF16)
    v_ref[...] = _dot(h, w_ref[:, 2048:2560]).astype(BF16)


def _even_in(x2d, g, sc, sh, w_in, seq):
    t, d = x2d.shape
    n = w_in.shape[1]
    per_b = seq // TM
    vec = pl.BlockSpec((1, 1, d), lambda i: (i // per_b, 0, 0))
    out = lambda dt: jax.ShapeDtypeStruct((t, 512), dt)
    return pl.pallas_call(
        _even_in_kernel,
        grid=(t // TM,),
        in_specs=[
            pl.BlockSpec((TM, d), lambda i: (i, 0)),
            pl.BlockSpec((1, d), lambda i: (0, 0)),
            vec, vec,
            pl.BlockSpec((d, n), lambda i: (0, 0)),
        ],
        out_specs=[pl.BlockSpec((TM, 512), lambda i: (i, 0))] * 4,
        out_shape=[out(F32), out(BF16), out(BF16), out(BF16)],
        compiler_params=_cparams("arbitrary"),
        name="even_in",
    )(x2d, g, sc, sh, w_in)


def _conv_kernel(y_ref, halo_ref, w_ref, b_ref, ng_ref, nb_ref, avg_ref, o_ref, buf, cv):
    i = pl.program_id(1)
    buf[0:CONV_HALO, :] = jnp.where(i > 0, halo_ref[...], 0.0)
    buf[CONV_HALO:, :] = y_ref[...]
    first = CONV_HALO - (CONV_WIDTH - 1)
    for rc in range(TS_CONV // CONV_ROWS):
        r0 = rc * CONV_ROWS
        for lc in range(D_CONV // LANES):
            ls = slice(lc * LANES, (lc + 1) * LANES)
            acc = jnp.zeros((CONV_ROWS, LANES), F32)
            for j in range(CONV_WIDTH):
                s = r0 + first + j
                acc = acc + w_ref[j:j + 1, ls] * buf[s:s + CONV_ROWS, ls]
            cv[r0:r0 + CONV_ROWS, ls] = acc + b_ref[:, ls]
    y = cv[...]
    avg = avg_ref[...]
    d = y - _split_dot(y, avg)
    var = _split_dot(d * d, avg)
    yn = d * lax.rsqrt(var + LN_EPS) * ng_ref[...] + nb_ref[...]
    o_ref[...] = (yn * _sigmoid(yn)).astype(BF16)


def _conv(y2d, conv_w, conv_b, norm_g, norm_b, bsz, seq):
    t = y2d.shape[0]
    per_b = seq // TS_CONV
    ratio = TS_CONV // CONV_HALO
    gid = jnp.arange(D_CONV) // CONV_GROUP_SIZE
    avg = jnp.where(gid[:, None] == gid[None, :], 1.0 / CONV_GROUP_SIZE, 0.0).astype(BF16)
    w_pad = jnp.pad(conv_w, ((0, 32 - CONV_WIDTH), (0, 0)))
    row = lambda a: a.reshape(1, D_CONV)
    full = lambda shape: pl.BlockSpec(shape, lambda b, i: (0, 0))
    return pl.pallas_call(
        _conv_kernel,
        grid=(bsz, per_b),
        in_specs=[
            pl.BlockSpec((TS_CONV, D_CONV), lambda b, i: (b * per_b + i, 0)),
            pl.BlockSpec((CONV_HALO, D_CONV),
                         lambda b, i: (jnp.maximum((b * per_b + i) * ratio - 1, 0), 0)),
            full((32, D_CONV)), full((1, D_CONV)), full((1, D_CONV)), full((1, D_CONV)),
            full((D_CONV, D_CONV)),
        ],
        out_specs=pl.BlockSpec((TS_CONV, D_CONV), lambda b, i: (b * per_b + i, 0)),
        out_shape=jax.ShapeDtypeStruct((t, D_CONV), BF16),
        scratch_shapes=[pltpu.VMEM((TS_CONV + CONV_HALO, D_CONV), F32),
                        pltpu.VMEM((TS_CONV, D_CONV), F32)],
        compiler_params=_cparams("arbitrary", "arbitrary"),
        name="conv_module",
    )(y2d, y2d, w_pad, row(conv_b), row(norm_g), row(norm_b), avg)


def _sb_kernel(q_ref, k_ref, v_ref, u_ref, o_ref, acc_ref, cr_ref):
    qi = pl.program_id(2)
    q = q_ref[...]
    lane = lax.broadcasted_iota(jnp.int32, (TQ, LANES), 1)
    row = lax.broadcasted_iota(jnp.int32, (TQ, TQ), 0)
    col = lax.broadcasted_iota(jnp.int32, (TQ, TQ), 1)
    qm = [jnp.where(lane < SB_HEAD_DIM, q, jnp.zeros_like(q)),
          jnp.where(lane >= SB_HEAD_DIM, q, jnp.zeros_like(q))]
    acc_ref[...] = jnp.zeros_like(acc_ref)
    cr_ref[...] = jnp.zeros_like(cr_ref)

    def key_blocks(first_kb, count):
        for hd in range(2):
            cr = cr_ref[hd]
            acc = acc_ref[hd]
            for step in range(count):
                kb = first_kb - step
                ks = pl.multiple_of(kb * TQ, TQ)
                z = _dot_nt(qm[hd], k_ref[pl.ds(ks, TQ), :])
                l1p = jnp.log(1.0 + jnp.exp(-jnp.abs(z)))
                strict = (col + (kb - qi) * TQ) < row
                log_keep = jnp.where(strict, -jnp.maximum(z, 0.0) - l1p, 0.0)
                later = _split_dot(log_keep, u_ref[...]) + cr
                a = jnp.where(strict, jnp.exp(jnp.minimum(z, 0.0) - l1p + later), 0.0)
                acc = acc + _dot(a.astype(BF16), v_ref[pl.ds(ks, TQ), :])
                cr = cr + jnp.sum(log_keep, axis=-1, keepdims=True)
            cr_ref[hd] = cr
            acc_ref[hd] = acc

    def alive():
        return (jnp.max(cr_ref[...]) > SB_DEAD).astype(jnp.int32)

    @pl.when(qi == 0)
    def _():
        key_blocks(qi, 1)

    @pl.when(qi > 0)
    def _():
        key_blocks(qi, 2)

    def cond(carry):
        kb, live = carry
        return jnp.logical_and(kb >= 0, live > 0)

    def body(carry):
        kb, _ = carry
        key_blocks(kb, 1)
        return kb - 1, alive()

    lax.while_loop(cond, body, (qi - 2, alive()))
    o_ref[...] = jnp.where(lane < SB_HEAD_DIM, acc_ref[0], acc_ref[1]).astype(BF16)


def _sb_attention(q, k, v, bsz, seq):
    t = q.shape[0]
    nq = seq // TQ
    idx = jnp.arange(TQ)
    upper = (idx[:, None] > idx[None, :]).astype(BF16)
    kv = pl.BlockSpec((seq, LANES), lambda b, p, i: (b, p))
    return pl.pallas_call(
        _sb_kernel,
        grid=(bsz, SB_HEADS // 2, nq),
        in_specs=[
            pl.BlockSpec((TQ, LANES), lambda b, p, i: (b * nq + i, p)),
            kv, kv,
            pl.BlockSpec((TQ, TQ), lambda b, p, i: (0, 0)),
        ],
        out_specs=pl.BlockSpec((TQ, LANES), lambda b, p, i: (b * nq + i, p)),
        out_shape=jax.ShapeDtypeStruct((t, SB_HEADS * SB_HEAD_DIM), BF16),
        scratch_shapes=[pltpu.VMEM((2, TQ, LANES), F32), pltpu.VMEM((2, TQ, 1), F32)],
        compiler_params=_cparams("arbitrary", "arbitrary", "arbitrary"),
        name="stick_breaking",
    )(q, k, v, upper)


def _first_of(vals, target):
    idx = jnp.full(target.shape, len(vals) - 1, jnp.int32)
    for j in range(len(vals) - 2, -1, -1):
        idx = jnp.where(vals[j] == target, j, idx)
    return idx


def _top2(vals):
    m1 = functools.reduce(jnp.maximum, vals)
    i1 = _first_of(vals, m1)
    rest = [jnp.where(i1 == j, -jnp.inf, v) for j, v in enumerate(vals)]
    m2 = functools.reduce(jnp.maximum, rest)
    i2 = _first_of(rest, m2)
    return m1, m2, i1, i2


def _pick(idx, vals):
    out = vals[-1]
    for j in range(len(vals) - 2, -1, -1):
        out = jnp.where(idx == j, vals[j], out)
    return out


def _out_router_kernel(expert_rows, x_ref, a_ref, b_ref, w_ref, g1_ref, ng_ref, sc_ref, sh_ref,
                       rwh_ref, rwl_ref, rb_ref, tri_ref,
                       x1_ref, hp_ref, pos_ref, wt_ref, cnt_ref, member):
    half = a_ref.shape[1]
    mix = _dot(a_ref[...], w_ref[0:half, :]) + _dot(b_ref[...], w_ref[half:, :])
    x1 = x_ref[...] + g1_ref[0] * mix
    x1_ref[...] = x1
    h = _modulated_rmsnorm(x1, ng_ref[...], sc_ref[0], sh_ref[0])
    hb = h.astype(BF16)
    hp_ref[...] = h

    hl = (h - hb.astype(F32)).astype(BF16)
    rwh = rwh_ref[...]
    logits = (_dot(hb, rwh) + _dot(hl, rwh) + _dot(hb, rwl_ref[...])).T[0:N_EXPERTS, :]
    ex = jnp.exp(logits - jnp.max(logits, axis=0, keepdims=True))
    scores = ex / jnp.sum(ex, axis=0, keepdims=True)
    sel = scores + rb_ref[...]
    srow = [scores[e:e + 1, :] for e in range(N_EXPERTS)]
    lrow = [sel[e:e + 1, :] for e in range(N_EXPERTS)]
    tops = [_top2(lrow[g * EXPERTS_PER_GROUP:(g + 1) * EXPERTS_PER_GROUP])
            for g in range(N_EXPERT_GROUPS)]
    gscore = [tp[0] + tp[1] for tp in tops]
    gidx = _first_of(gscore, functools.reduce(jnp.maximum, gscore))
    e1 = gidx * EXPERTS_PER_GROUP + _pick(gidx, [tp[2] for tp in tops])
    e2 = gidx * EXPERTS_PER_GROUP + _pick(gidx, [tp[3] for tp in tops])
    w1 = _pick(e1, srow)
    w2 = _pick(e2, srow)
    tot = w1 + w2
    wt_ref[0:1, :] = w1 / tot
    wt_ref[1:2, :] = w2 / tot

    @pl.when(pl.program_id(0) == 0)
    def _():
        cnt_ref[...] = jnp.zeros_like(cnt_ref)

    for e in range(N_EXPERTS):
        member[e:e + 1, :] = jnp.where(jnp.logical_or(e1 == e, e2 == e), 1.0, 0.0)
    m = member[...]
    incl = _dot(m.astype(BF16), tri_ref[...])
    excl = incl - m + cnt_ref[:, 0:1]
    cnt_ref[...] = cnt_ref[...] + incl[:, TM - 1:TM]
    erow = [excl[e:e + 1, :] for e in range(N_EXPERTS)]
    pos_ref[0:1, :] = e1 * expert_rows + _pick(e1, erow).astype(jnp.int32)
    pos_ref[1:2, :] = e2 * expert_rows + _pick(e2, erow).astype(jnp.int32)


def _out_router(x2d, a, b, w_out, g1, ng, sc, sh, rw_hi, rw_lo, rb, seq):
    t, d = x2d.shape
    per_b = seq // TM
    vec = pl.BlockSpec((1, 1, d), lambda i: (i // per_b, 0, 0))
    full = lambda shape: pl.BlockSpec(shape, lambda i: (0, 0))
    half = a.shape[1]
    idx = jnp.arange(TM)
    tri = (idx[:, None] <= idx[None, :]).astype(BF16)
    return pl.pallas_call(
        functools.partial(_out_router_kernel, t),
        grid=(t // TM,),
        in_specs=[
            pl.BlockSpec((TM, d), lambda i: (i, 0)),
            pl.BlockSpec((TM, half), lambda i: (i, 0)),
            pl.BlockSpec((TM, half), lambda i: (i, 0)),
            full((2 * half, d)),
            vec, full((1, d)), vec, vec,
            full((d, LANES)), full((d, LANES)), full((N_EXPERTS, 1)), full((TM, TM)),
        ],
        out_specs=[
            pl.BlockSpec((TM, d), lambda i: (i, 0)),
            pl.BlockSpec((TM, d), lambda i: (i, 0)),
            pl.BlockSpec((2, TM), lambda i: (0, i)),
            pl.BlockSpec((2, TM), lambda i: (0, i)),
            pl.BlockSpec((N_EXPERTS, LANES), lambda i: (0, 0)),
        ],
        out_shape=[
            jax.ShapeDtypeStruct((t, d), F32),
            jax.ShapeDtypeStruct((t, d), F32),
            jax.ShapeDtypeStruct((2, t), jnp.int32),
            jax.ShapeDtypeStruct((2, t), F32),
            jax.ShapeDtypeStruct((N_EXPERTS, LANES), F32),
        ],
        scratch_shapes=[pltpu.VMEM((N_EXPERTS, TM), F32)],
        compiler_params=_cparams("arbitrary"),
        name="out_router",
    )(x2d, a, b, w_out, g1, ng, sc, sh, rw_hi, rw_lo, rb, tri)


def _row_copy(src, src_row, dst, dst_row, sem):
    return pltpu.make_async_copy(src.at[pl.ds(src_row, 1), :], dst.at[pl.ds(dst_row, 1), :], sem)


def _dispatch_kernel(pos_ref, hp_ref, xs_ref, sem):
    n_tok = pos_ref.shape[0] // 2
    base = pl.program_id(0) * TM

    def issue(r, carry):
        for pick in range(2):
            _row_copy(hp_ref, r, xs_ref, pos_ref[pick * n_tok + base + r], sem).start()
        return carry

    lax.fori_loop(0, TM, issue, 0, unroll=8)
    for _ in range(2):
        pltpu.make_async_copy(hp_ref, xs_ref.at[pl.ds(0, TM), :], sem).wait()


def _dispatch(pos_flat, hp, rows):
    t, width = hp.shape
    return pl.pallas_call(
        _dispatch_kernel,
        grid_spec=pltpu.PrefetchScalarGridSpec(
            num_scalar_prefetch=1,
            grid=(t // TM,),
            in_specs=[pl.BlockSpec((TM, width), lambda i, pos: (i, 0))],
            out_specs=pl.BlockSpec(memory_space=pl.ANY),
            scratch_shapes=[pltpu.SemaphoreType.DMA],
        ),
        out_shape=jax.ShapeDtypeStruct((rows, width), F32),
        compiler_params=_cparams("arbitrary"),
        name="moe_dispatch",
    )(pos_flat, hp)


def _expert_kernel(te_ref, tb_ref, tv_ref, xs_ref, wg_ref, wu_ref, wd_ref, ys_ref,
                   wg_bf, wu_bf, wd_bf):
    i = pl.program_id(0)
    expert = te_ref[i]
    valid = tv_ref[i]

    @pl.when(jnp.logical_or(i == 0, expert != te_ref[jnp.maximum(i - 1, 0)]))
    def _():
        wg_bf[...] = wg_ref[0].astype(BF16)
        wu_bf[...] = wu_ref[0].astype(BF16)
        wd_bf[...] = wd_ref[0].astype(BF16)

    @pl.when(valid > 0)
    def _():
        row = lax.broadcasted_iota(jnp.int32, xs_ref.shape, 0)
        h = jnp.where(row < valid, xs_ref[...], 0.0).astype(BF16)
        gate = _dot(h, wg_bf[...])
        up = _dot(h, wu_bf[...])
        act = (gate * _sigmoid(gate) * up).astype(BF16)
        ys_ref[...] = _dot(act, wd_bf[...])

    @pl.when(valid == 0)
    def _():
        ys_ref[...] = jnp.zeros_like(ys_ref)


def _experts(tile_expert, tile_block, tile_valid, xs, layer, wg, wu, wd):
    rows, width = xs.shape
    d = width
    n_tiles = tile_expert.shape[0]
    wspec = lambda shape: pl.BlockSpec((None,) + shape, lambda i, te, tb, tv: (layer, te[i], 0, 0))
    return pl.pallas_call(
        _expert_kernel,
        grid_spec=pltpu.PrefetchScalarGridSpec(
            num_scalar_prefetch=3,
            grid=(n_tiles,),
            in_specs=[
                pl.BlockSpec((TM_MOE, width), lambda i, te, tb, tv: (tb[i], 0)),
                wspec((1, d, D_EXPERT)), wspec((1, d, D_EXPERT)), wspec((1, D_EXPERT, d)),
            ],
            out_specs=pl.BlockSpec((TM_MOE, width), lambda i, te, tb, tv: (tb[i], 0)),
            scratch_shapes=[pltpu.VMEM((d, D_EXPERT), BF16), pltpu.VMEM((d, D_EXPERT), BF16),
                            pltpu.VMEM((D_EXPERT, d), BF16)],
        ),
        out_shape=jax.ShapeDtypeStruct((rows, width), F32),
        compiler_params=_cparams("arbitrary"),
        name="moe_experts",
    )(tile_expert, tile_block, tile_valid, xs, wg, wu, wd)


def _combine_kernel(final, pos_ref, x1_ref, wt_ref, g2_ref, fn_ref, ys_ref, o_ref, buf, sems):
    n_tok = pos_ref.shape[0] // 2
    i = pl.program_id(0)
    n = pl.num_programs(0)

    def gather(tile, slot):
        base = tile * TM

        def issue(r, carry):
            for pick in range(2):
                _row_copy(ys_ref, pos_ref[pick * n_tok + base + r], buf.at[slot, pick], r,
                          sems.at[slot]).start()
            return carry

        lax.fori_loop(0, TM, issue, 0, unroll=8)

    @pl.when(i == 0)
    def _():
        gather(0, 0)

    @pl.when(i + 1 < n)
    def _():
        gather(i + 1, (i + 1) % 2)

    slot = i % 2
    for pick in range(2):
        pltpu.make_async_copy(ys_ref.at[pl.ds(0, TM), :], buf.at[slot, pick], sems.at[slot]).wait()
    y = wt_ref[:, 0:1] * buf[slot, 0] + wt_ref[:, 1:2] * buf[slot, 1]
    x2 = x1_ref[...] + g2_ref[0] * y
    if final:
        ms = jnp.mean(x2 * x2, axis=-1, keepdims=True)
        x2 = x2 * lax.rsqrt(ms + RMS_EPS) * fn_ref[...]
    o_ref[...] = x2


def _combine(pos_flat, x1, wt, g2, fn, ys, seq, final):
    t, d = x1.shape
    per_b = seq // TM
    return pl.pallas_call(
        functools.partial(_combine_kernel, final),
        grid_spec=pltpu.PrefetchScalarGridSpec(
            num_scalar_prefetch=1,
            grid=(t // TM,),
            in_specs=[
                pl.BlockSpec((TM, d), lambda i, pos: (i, 0)),
                pl.BlockSpec((TM, 2), lambda i, pos: (i, 0)),
                pl.BlockSpec((1, 1, d), lambda i, pos: (i // per_b, 0, 0)),
                pl.BlockSpec((1, d), lambda i, pos: (0, 0)),
                pl.BlockSpec(memory_space=pl.ANY),
            ],
            out_specs=pl.BlockSpec((TM, d), lambda i, pos: (i, 0)),
            scratch_shapes=[pltpu.VMEM((2, 2, TM, d), F32),
                            pltpu.SemaphoreType.DMA((2,))],
        ),
        out_shape=jax.ShapeDtypeStruct((t, d), F32),
        compiler_params=_cparams("arbitrary"),
        name="moe_combine_final" if final else "moe_combine",
    )(pos_flat, x1, wt, g2, fn, ys)


def _tile_schedule(counts, n_tok):
    blocks_per_expert = n_tok // TM_MOE
    n_tiles = 2 * n_tok // TM_MOE + N_EXPERTS
    tiles = (counts + TM_MOE - 1) // TM_MOE
    ends = jnp.cumsum(tiles)
    starts = ends - tiles
    j = jnp.arange(n_tiles, dtype=jnp.int32)
    used = j < ends[-1]
    e = jnp.minimum(jnp.sum((j[:, None] >= ends[None, :]).astype(jnp.int32), axis=1), N_EXPERTS - 1)
    local = j - starts[e]
    last_e = jnp.max(jnp.where(tiles > 0, jnp.arange(N_EXPERTS, dtype=jnp.int32), 0))
    tile_expert = jnp.where(used, e, last_e)
    tile_block = jnp.where(used, e * blocks_per_expert + local, N_EXPERTS * blocks_per_expert)
    tile_valid = jnp.where(used, jnp.clip(counts[e] - local * TM_MOE, 0, TM_MOE), 0)
    return tile_expert.astype(jnp.int32), tile_block.astype(jnp.int32), tile_valid.astype(jnp.int32)


def _moe(x1, hp, pos, wt, counts, layer, wg, wu, wd, g2, fn, seq, final):
    t = x1.shape[0]
    rows = N_EXPERTS * t + TM_MOE
    pos_flat = pos.reshape(-1)
    te, tb, tv = _tile_schedule(counts[:, 0].astype(jnp.int32), t)
    xs = _dispatch(pos_flat, hp, rows)
    ys = _experts(te, tb, tv, xs, layer, wg, wu, wd)
    return _combine(pos_flat, x1, wt.T, g2, fn, ys, seq, final)


def _log_sigmoid(x):
    return jnp.minimum(x, 0.0) - jnp.log1p(jnp.exp(-jnp.abs(x)))


def _odd_in_kernel(x_ref, g_ref, sc_ref, sh_ref, w_ref, gw_ref, gb_ref,
                   q_ref, k_ref, v_ref, sr_ref, la_ref, up_ref):
    h = _modulated_rmsnorm(x_ref[...], g_ref[...], sc_ref[0], sh_ref[0]).astype(BF16)
    q_ref[...] = _dot(h, w_ref[:, 0:512]) * (GLA_DK ** -0.5)
    k_ref[...] = _dot(h, w_ref[:, 512:1024])
    v_ref[...] = _dot(h, w_ref[:, 1024:1536]).astype(BF16)
    r = _dot(h, w_ref[:, 1536:2048])
    sr_ref[...] = r * _sigmoid(r)
    up_ref[...] = _dot(h, w_ref[:, 2048:2560])
    a_low = _dot(h, w_ref[:, 2560:2688]).astype(BF16)
    la_ref[...] = _log_sigmoid(_dot(a_low, gw_ref[...]) + gb_ref[...]) * (1.0 / GLA_TAU)


def _odd_in(x2d, g, sc, sh, w_cat, gate_w, gate_b, seq):
    t, d = x2d.shape
    n = w_cat.shape[1]
    per_b = seq // TM
    vec = pl.BlockSpec((1, 1, d), lambda i: (i // per_b, 0, 0))
    full = lambda shape: pl.BlockSpec(shape, lambda i: (0, 0))
    out = lambda dt: jax.ShapeDtypeStruct((t, 512), dt)
    return pl.pallas_call(
        _odd_in_kernel,
        grid=(t // TM,),
        in_specs=[
            pl.BlockSpec((TM, d), lambda i: (i, 0)),
            full((1, d)), vec, vec, full((d, n)),
            full((LANES, 512)), full((1, 512)),
        ],
        out_specs=[pl.BlockSpec((TM, 512), lambda i: (i, 0))] * 6,
        out_shape=[out(F32), out(F32), out(BF16), out(F32), out(F32), out(F32)],
        compiler_params=_cparams("arbitrary"),
        name="odd_in",
    )(x2d, g, sc, sh, w_cat, gate_w, gate_b)


def _gla_kernel(q_ref, k_ref, la_ref, v_ref, sr_ref, g_ref, tri_ref, o_ref, st_ref):
    tri = tri_ref[...]
    row = lax.broadcasted_iota(jnp.int32, (GLA_CHUNK, GLA_CHUNK), 0)
    col = lax.broadcasted_iota(jnp.int32, (GLA_CHUNK, GLA_CHUNK), 1)
    causal = col <= row

    @pl.when(pl.program_id(1) == 0)
    def _():
        st_ref[...] = jnp.zeros_like(st_ref)

    def chunk(n, carry):
        rows = pl.ds(pl.multiple_of(n * GLA_CHUNK, GLA_CHUNK), GLA_CHUNK)
        for hd in range(GLA_HEADS):
            ls = slice(hd * LANES, (hd + 1) * LANES)
            b = _split_dot_left(tri, la_ref[rows, ls])
            b_last = b[GLA_CHUNK - 1:GLA_CHUNK, :]
            k = k_ref[rows, ls]
            v = v_ref[rows, ls]
            state_t = st_ref[hd]
            q_in = (q_ref[rows, ls] * jnp.exp(b)).astype(BF16)
            k_in = (k * jnp.exp(-b)).astype(BF16)
            k_dec = (k * jnp.exp(b_last - b)).astype(BF16)
            scores = jnp.where(causal, _dot_nt(q_in, k_in), 0.0).astype(BF16)
            o = _dot(scores, v) + _dot_nt(q_in, state_t.astype(BF16))
            o = o * lax.rsqrt(jnp.mean(o * o, axis=-1, keepdims=True) + RMS_EPS)
            o_ref[rows, ls] = (o * g_ref[:, ls] * sr_ref[rows, ls]).astype(BF16)
            st_ref[hd] = state_t * jnp.exp(b_last) + _dot_tn(v, k_dec)
        return carry

    lax.fori_loop(0, TS_GLA // GLA_CHUNK, chunk, 0)


def _gla(q, k, la, v, sr, norm_g, bsz, seq):
    t, width = q.shape
    per_b = seq // TS_GLA
    idx = jnp.arange(GLA_CHUNK)
    tri = (idx[:, None] >= idx[None, :]).astype(BF16)
    blk = pl.BlockSpec((TS_GLA, width), lambda b, i: (b * per_b + i, 0))
    return pl.pallas_call(
        _gla_kernel,
        grid=(bsz, per_b),
        in_specs=[blk, blk, blk, blk, blk,
                  pl.BlockSpec((1, width), lambda b, i: (0, 0)),
                  pl.BlockSpec((GLA_CHUNK, GLA_CHUNK), lambda b, i: (0, 0))],
        out_specs=blk,
        out_shape=jax.ShapeDtypeStruct((t, width), BF16),
        scratch_shapes=[pltpu.VMEM((GLA_HEADS, GLA_DV, LANES), F32)],
        compiler_params=_cparams("arbitrary", "arbitrary"),
        name="gla",
    )(q, k, la, v, sr, norm_g.reshape(1, -1), tri)


def _pool_kernel(u_ref, halo_ref, w_ref, b_ref, s_ref, o_ref, buf):
    i = pl.program_id(1)
    buf[0:POOL_HALO, :] = jnp.where(i > 0, halo_ref[...], 0.0)
    buf[POOL_HALO:, :] = u_ref[...]
    pos = (i * TS_POOL + 1 + lax.broadcasted_iota(jnp.int32, (TS_POOL, 1), 0)).astype(F32)
    for gi, win in enumerate(POOL_WINDOWS):
        ls = slice(gi * LANES, (gi + 1) * LANES)
        tok = buf[POOL_HALO:, ls]
        tot = tok
        for back in range(1, win):
            tot = tot + buf[POOL_HALO - back:POOL_HALO - back + TS_POOL, ls]
        p = tot / jnp.minimum(pos, float(win)) - tok
        y = _dot(p.astype(BF16), w_ref[gi]) + b_ref[:, ls]
        o_ref[:, ls] = (y * s_ref[:, ls]).astype(BF16)


def _pool(u2d, pool_w, pool_b, pool_scale, bsz, seq):
    t = u2d.shape[0]
    per_b = seq // TS_POOL
    ratio = TS_POOL // POOL_HALO
    full = lambda shape: pl.BlockSpec(shape, lambda b, i: (0,) * len(shape))
    return pl.pallas_call(
        _pool_kernel,
        grid=(bsz, per_b),
        in_specs=[
            pl.BlockSpec((TS_POOL, D_POOL), lambda b, i: (b * per_b + i, 0)),
            pl.BlockSpec((POOL_HALO, D_POOL),
                         lambda b, i: (jnp.maximum((b * per_b + i) * ratio - 1, 0), 0)),
            full((len(POOL_WINDOWS), LANES, LANES)), full((1, D_POOL)), full((1, D_POOL)),
        ],
        out_specs=pl.BlockSpec((TS_POOL, D_POOL), lambda b, i: (b * per_b + i, 0)),
        out_shape=jax.ShapeDtypeStruct((t, D_POOL), BF16),
        scratch_shapes=[pltpu.VMEM((TS_POOL + POOL_HALO, D_POOL), F32)],
        compiler_params=_cparams("arbitrary", "arbitrary"),
        name="pool_mixer",
    )(u2d, u2d, pool_w.astype(BF16), pool_b.reshape(1, D_POOL), pool_scale.reshape(1, D_POOL))


def _pad_heads(w, heads, dim):
    lead = w.shape[:-1]
    w = w.reshape(lead + (heads, dim))
    w = jnp.pad(w, [(0, 0)] * len(lead) + [(0, 0), (0, LANES - dim)])
    return w.reshape(lead + (heads * LANES,))


def _odd_weights(w_in, gate_w, gate_b):
    dkt = GLA_HEADS * GLA_DK
    dvt = GLA_HEADS * GLA_DV
    o = [0, dkt, 2 * dkt, 2 * dkt + dvt, 2 * dkt + 2 * dvt, 2 * dkt + 2 * dvt + GLA_RANK]
    wq, wk, wv, wr, wa, wp = [w_in[:, o[j]:(o + [w_in.shape[1]])[j + 1]] for j in range(6)]
    w_cat = jnp.concatenate([
        _pad_heads(wq, GLA_HEADS, GLA_DK), _pad_heads(wk, GLA_HEADS, GLA_DK), wv, wr, wp,
        jnp.pad(wa, ((0, 0), (0, LANES - GLA_RANK)))], axis=1).astype(BF16)
    gw = jnp.pad(_pad_heads(gate_w, GLA_HEADS, GLA_DK), ((0, LANES - GLA_RANK), (0, 0))).astype(BF16)
    gb = _pad_heads(gate_b, GLA_HEADS, GLA_DK).reshape(1, -1)
    return w_cat, gw, gb


def kernel(x, c, ada_w, ada_b, norm_mix, norm_ffn, w_in_even, w_out_even, conv_w, conv_b, conv_norm_g, conv_norm_b, w_in_odd, w_out_odd, gla_gate_w, gla_gate_b, gla_norm_g, pool_w, pool_b, pool_scale, router_w, router_bias, moe_w_gate, moe_w_up, moe_w_down, final_norm):
    bsz, seq, d = x.shape
    t = bsz * seq
    depth = ada_w.shape[0]
    mod = _ada(c, ada_w, ada_b)
    rw_pad = jnp.pad(router_w, ((0, 0), (0, LANES - N_EXPERTS)))
    rw_hi = rw_pad.astype(BF16)
    rw_lo = (rw_pad - rw_hi.astype(F32)).astype(BF16)
    rb =router_bias.reshape(N_EXPERTS, 1)
    fn = final_norm.reshape(1, d)
    x2d = x.reshape(t, d)
    for l in range(depth):
        sh1, sc1, g1, sh2, sc2, g2 = [mod[l, :, j * d:(j + 1) * d].reshape(bsz, 1, d)
                                      for j in range(6)]
        i = l // 2
        nm = norm_mix[l].reshape(1, d)
        if l % 2 == 0:
            y, q, k, v = _even_in(x2d, nm, sc1, sh1, w_in_even[i].astype(BF16), seq)
            a = _conv(y, conv_w[i], conv_b[i], conv_norm_g[i], conv_norm_b[i], bsz, seq)
            b = _sb_attention(q, k, v, bsz, seq)
            w_out = w_out_even[i]
        else:
            w_cat, gw, gb = _odd_weights(w_in_odd[i], gla_gate_w[i], gla_gate_b[i])
            q, k, v, sr, la, up = _odd_in(x2d, nm, sc1, sh1, w_cat, gw, gb, seq)
            a = _gla(q, k, la, v, sr, gla_norm_g[i], bsz, seq)
            b = _pool(up, pool_w[i], pool_b[i], pool_scale[i], bsz, seq)
            w_out = w_out_odd[i]
        x1, hp, pos, wt, counts = _out_router(x2d, a, b, w_out.astype(BF16), g1,
                                              norm_ffn[l].reshape(1, d), sc2, sh2,
                                              rw_hi, rw_lo, rb, seq)
        x2d = _moe(x1, hp, pos, wt, counts, l, moe_w_gate, moe_w_up, moe_w_down, g2, fn, seq,
                   final=(l == depth - 1))
    return x2d.reshape(bsz, seq, d)
```

```python
import functools

import jax
import jax.numpy as jnp
from jax import lax
from jax.experimental import pallas as pl
from jax.experimental.pallas import tpu as pltpu

F32 = jnp.float32
BF16 = jnp.bfloat16

D_MODEL = 1024
D_CONV = 512
CONV_WIDTH = 31
CONV_GROUP_SIZE = 64
SB_HEADS = 8
SB_HEAD_DIM = 64
GLA_HEADS = 4
GLA_DK = 64
GLA_DV = 128
GLA_RANK = 16
GLA_TAU = 16.0
GLA_CHUNK = 64
D_POOL = 512
POOL_WINDOWS = (2, 4, 8, 16)
N_EXPERTS = 16
N_EXPERT_GROUPS = 4
EXPERTS_PER_GROUP = 4
D_EXPERT = 512
RMS_EPS = 1e-6
LN_EPS = 1e-5

LANES = 128
SUBLANES = 8
VMEM_LIMIT = 56 * 1024 * 1024

TM = 512
TS_CONV = 256
CONV_HALO = 32
CONV_ROWS = 64
TS_GLA = 1024
GLA_UNROLL = 4
TS_POOL = 512
POOL_HALO = 16
TQ = 256
TM_MOE = 512
SB_DEAD = -120.0


def _cparams(*sem):
    return pltpu.CompilerParams(dimension_semantics=sem, vmem_limit_bytes=VMEM_LIMIT)


def _sigmoid(x):
    return 1.0 / (1.0 + jnp.exp(-x))


def _dot(a, b):
    return jnp.dot(a, b, preferred_element_type=F32)


def _dot_nt(a, b, **kw):
    return lax.dot_general(a, b, (((1,), (1,)), ((), ())), preferred_element_type=F32, **kw)


def _dot_tn(a, b):
    return lax.dot_general(a, b, (((0,), (0,)), ((), ())), preferred_element_type=F32)


def _split_dot(x, m):
    hi = x.astype(BF16)
    lo = (x - hi.astype(F32)).astype(BF16)
    return _dot(hi, m) + _dot(lo, m)


def _split_dot_left(m, x):
    hi = x.astype(BF16)
    lo = (x - hi.astype(F32)).astype(BF16)
    return _dot(m, hi) + _dot(m, lo)


def _modulated_rmsnorm(x, g, sc, sh):
    ms = jnp.mean(x * x, axis=-1, keepdims=True)
    return (x * lax.rsqrt(ms + RMS_EPS) * g) * (1.0 + sc) + sh


def _ada_kernel(c_ref, w_ref, b_ref, o_ref):
    c = c_ref[...]
    cond = c * _sigmoid(c)
    o_ref[0] = jnp.dot(cond, w_ref[0], preferred_element_type=F32,
                       precision=lax.Precision.HIGHEST) + b_ref[0]


def _ada(c, ada_w, ada_b):
    depth, d, n = ada_w.shape
    bsz = c.shape[0]
    tn = 1536
    return pl.pallas_call(
        _ada_kernel,
        grid=(depth, n // tn),
        in_specs=[
            pl.BlockSpec((bsz, d), lambda l, j: (0, 0)),
            pl.BlockSpec((1, d, tn), lambda l, j: (l, 0, j)),
            pl.BlockSpec((1, 1, tn), lambda l, j: (l, 0, j)),
        ],
        out_specs=pl.BlockSpec((1, bsz, tn), lambda l, j: (l, 0, j)),
        out_shape=jax.ShapeDtypeStruct((depth, bsz, n), F32),
        compiler_params=_cparams("arbitrary", "arbitrary"),
        name="ada_mod",
    )(c, ada_w, ada_b.reshape(depth, 1, n))


def _even_in_kernel(x_ref, g_ref, sc_ref, sh_ref, w_ref, y_ref, q_ref, k_ref, v_ref):
    h = _modulated_rmsnorm(x_ref[...], g_ref[...], sc_ref[0], sh_ref[0]).astype(BF16)
    val = _dot(h, w_ref[:, 0:512])
    gate = _dot(h, w_ref[:, 512:1024])
    y_ref[...] = val * _sigmoid(gate)
    q_ref[...] = (_dot(h, w_ref[:, 1024:1536]) * (SB_HEAD_DIM ** -0.5)).astype(BF16)
    k_ref[...] = _dot(h, w_ref[:, 1536:2048]).astype(BF16)
    v_ref[...] = _dot(h, w_ref[:, 2048:2560]).astype(BF16)


def _even_in(x2d, g, sc, sh, w_in, seq):
    t, d = x2d.shape
    n = w_in.shape[1]
    per_b = seq // TM
    vec = pl.BlockSpec((1, 1, d), lambda i: (i // per_b, 0, 0))
    out = lambda dt: jax.ShapeDtypeStruct((t, 512), dt)
    return pl.pallas_call(
        _even_in_kernel,
        grid=(t // TM,),
        in_specs=[
            pl.BlockSpec((TM, d), lambda i: (i, 0)),
            pl.BlockSpec((1, d), lambda i: (0, 0)),
            vec, vec,
            pl.BlockSpec((d, n), lambda i: (0, 0)),
        ],
        out_specs=[pl.BlockSpec((TM, 512), lambda i: (i, 0))] * 4,
        out_shape=[out(F32), out(BF16), out(BF16), out(BF16)],
        compiler_params=_cparams("arbitrary"),
        name="even_in",
    )(x2d, g, sc, sh, w_in)


def _conv_kernel(y_ref, halo_ref, w_ref, b_ref, ng_ref, nb_ref, avg_ref, o_ref, buf, sh, cv):
    i = pl.program_id(1)
    buf[0:CONV_HALO, :] = jnp.where(i > 0, halo_ref[...], 0.0)
    buf[CONV_HALO:, :] = y_ref[...]
    sh_rows = sh.shape[1]
    for r in range(1, SUBLANES):
        sh[r - 1] = buf[r:r + sh_rows, :]
    first = CONV_HALO - (CONV_WIDTH - 1)
    for rc in range(TS_CONV // CONV_ROWS):
        r0 = rc * CONV_ROWS
        for lc in range(D_CONV // LANES):
            ls = slice(lc * LANES, (lc + 1) * LANES)
            acc = jnp.zeros((CONV_ROWS, LANES), F32)
            for j in range(CONV_WIDTH):
                whole, r = divmod(first + j, SUBLANES)
                s = r0 + whole * SUBLANES
                src = buf[s:s + CONV_ROWS, ls] if r == 0 else sh[r - 1, s:s + CONV_ROWS, ls]
                acc = acc + w_ref[j:j + 1, ls] * src
            cv[r0:r0 + CONV_ROWS, ls] = acc + b_ref[:, ls]
    y = cv[...]
    avg = avg_ref[...]
    d = y - _split_dot(y, avg)
    var = _split_dot(d * d, avg)
    yn = d * lax.rsqrt(var + LN_EPS) * ng_ref[...] + nb_ref[...]
    o_ref[...] = (yn * _sigmoid(yn)).astype(BF16)


def _conv(y2d, conv_w, conv_b, norm_g, norm_b, bsz, seq):
    t = y2d.shape[0]
    per_b = seq // TS_CONV
    ratio = TS_CONV // CONV_HALO
    gid = jnp.arange(D_CONV) // CONV_GROUP_SIZE
    avg = jnp.where(gid[:, None] == gid[None, :], 1.0 / CONV_GROUP_SIZE, 0.0).astype(BF16)
    w_pad = jnp.pad(conv_w, ((0, 32 - CONV_WIDTH), (0, 0)))
    row = lambda a: a.reshape(1, D_CONV)
    full = lambda shape: pl.BlockSpec(shape, lambda b, i: (0, 0))
    return pl.pallas_call(
        _conv_kernel,
        grid=(bsz, per_b),
        in_specs=[
            pl.BlockSpec((TS_CONV, D_CONV), lambda b, i: (b * per_b + i, 0)),
            pl.BlockSpec((CONV_HALO, D_CONV),
                         lambda b, i: (jnp.maximum((b * per_b + i) * ratio - 1, 0), 0)),
            full((32, D_CONV)), full((1, D_CONV)), full((1, D_CONV)), full((1, D_CONV)),
            full((D_CONV, D_CONV)),
        ],
        out_specs=pl.BlockSpec((TS_CONV, D_CONV), lambda b, i: (b * per_b + i, 0)),
        out_shape=jax.ShapeDtypeStruct((t, D_CONV), BF16),
        scratch_shapes=[pltpu.VMEM((TS_CONV + CONV_HALO, D_CONV), F32),
                        pltpu.VMEM((SUBLANES - 1, TS_CONV + CONV_HALO - SUBLANES, D_CONV), F32),
                        pltpu.VMEM((TS_CONV, D_CONV), F32)],
        compiler_params=_cparams("arbitrary", "arbitrary"),
        name="conv_module",
    )(y2d, y2d, w_pad, row(conv_b), row(norm_g), row(norm_b), avg)


def _sb_kernel(q_ref, k_ref, v_ref, u_ref, o_ref, acc_ref, cr_ref):
    qi = pl.program_id(2)
    q = q_ref[...]
    lane = lax.broadcasted_iota(jnp.int32, (TQ, LANES), 1)
    row = lax.broadcasted_iota(jnp.int32, (TQ, TQ), 0)
    col = lax.broadcasted_iota(jnp.int32, (TQ, TQ), 1)
    qm = [jnp.where(lane < SB_HEAD_DIM, q, jnp.zeros_like(q)),
          jnp.where(lane >= SB_HEAD_DIM, q, jnp.zeros_like(q))]
    acc_ref[...] = jnp.zeros_like(acc_ref)
    cr_ref[...] = jnp.zeros_like(cr_ref)

    def key_blocks(first_kb, count):
        for hd in range(2):
            cr = cr_ref[hd]
            acc = acc_ref[hd]
            for step in range(count):
                kb = first_kb - step
                ks = pl.multiple_of(kb * TQ, TQ)
                z = _dot_nt(qm[hd], k_ref[pl.ds(ks, TQ), :])
                l1p = jnp.log(1.0 + jnp.exp(-jnp.abs(z)))
                strict = (col + (kb - qi) * TQ) < row
                log_keep = jnp.where(strict, -jnp.maximum(z, 0.0) - l1p, 0.0)
                later = _split_dot(log_keep, u_ref[...]) + cr
                a = jnp.where(strict, jnp.exp(jnp.minimum(z, 0.0) - l1p + later), 0.0)
                acc = acc + _dot(a.astype(BF16), v_ref[pl.ds(ks, TQ), :])
                cr = cr + jnp.sum(log_keep, axis=-1, keepdims=True)
            cr_ref[hd] = cr
            acc_ref[hd] = acc

    def alive():
        return (jnp.max(cr_ref[...]) > SB_DEAD).astype(jnp.int32)

    @pl.when(qi == 0)
    def _():
        key_blocks(qi, 1)

    @pl.when(qi > 0)
    def _():
        key_blocks(qi, 2)

    def cond(carry):
        kb, live = carry
        return jnp.logical_and(kb >= 0, live > 0)

    def body(carry):
        kb, _ = carry
        key_blocks(kb, 1)
        return kb - 1, alive()

    lax.while_loop(cond, body, (qi - 2, alive()))
    o_ref[...] = jnp.where(lane < SB_HEAD_DIM, acc_ref[0], acc_ref[1]).astype(BF16)


def _sb_attention(q, k, v, bsz, seq):
    t = q.shape[0]
    nq = seq // TQ
    idx = jnp.arange(TQ)
    upper = (idx[:, None] > idx[None, :]).astype(BF16)
    kv = pl.BlockSpec((seq, LANES), lambda b, p, i: (b, p))
    return pl.pallas_call(
        _sb_kernel,
        grid=(bsz, SB_HEADS // 2, nq),
        in_specs=[
            pl.BlockSpec((TQ, LANES), lambda b, p, i: (b * nq + i, p)),
            kv, kv,
            pl.BlockSpec((TQ, TQ), lambda b, p, i: (0, 0)),
        ],
        out_specs=pl.BlockSpec((TQ, LANES), lambda b, p, i: (b * nq + i, p)),
        out_shape=jax.ShapeDtypeStruct((t, SB_HEADS * SB_HEAD_DIM), BF16),
        scratch_shapes=[pltpu.VMEM((2, TQ, LANES), F32), pltpu.VMEM((2, TQ, 1), F32)],
        compiler_params=_cparams("arbitrary", "arbitrary", "arbitrary"),
        name="stick_breaking",
    )(q, k, v, upper)


def _first_of(vals, target):
    idx = jnp.full(target.shape, len(vals) - 1, jnp.int32)
    for j in range(len(vals) - 2, -1, -1):
        idx = jnp.where(vals[j] == target, j, idx)
    return idx


def _top2(vals):
    m1 = functools.reduce(jnp.maximum, vals)
    i1 = _first_of(vals, m1)
    rest = [jnp.where(i1 == j, -jnp.inf, v) for j, v in enumerate(vals)]
    m2 = functools.reduce(jnp.maximum, rest)
    i2 = _first_of(rest, m2)
    return m1, m2, i1, i2


def _pick(idx, vals):
    out = vals[-1]
    for j in range(len(vals) - 2, -1, -1):
        out = jnp.where(idx == j, vals[j], out)
    return out


def _out_router_kernel(expert_rows, x_ref, a_ref, b_ref, w_ref, g1_ref, ng_ref, sc_ref, sh_ref,
                       rw_ref, rb_ref, tri_ref,
                       x1_ref, hp_ref, pos_ref, wt_ref, cnt_ref, member):
    half = a_ref.shape[1]
    mix = _dot(a_ref[...], w_ref[0:half, :]) + _dot(b_ref[...], w_ref[half:, :])
    x1 = x_ref[...] + g1_ref[0] * mix
    x1_ref[...] = x1
    h = _modulated_rmsnorm(x1, ng_ref[...], sc_ref[0], sh_ref[0])
    hb = h.astype(BF16)
    hp_ref[...] = h

    hl = (h - hb.astype(F32)).astype(BF16)
    both = _dot_nt(rw_ref[...], hb)
    logits = (both[0:N_EXPERTS, :] + both[N_EXPERTS:, :]
              + _dot_nt(rw_ref[0:N_EXPERTS, :], hl))
    ex = jnp.exp(logits - jnp.max(logits, axis=0, keepdims=True))
    scores = ex / jnp.sum(ex, axis=0, keepdims=True)
    sel = scores + rb_ref[...]
    srow = [scores[e:e + 1, :] for e in range(N_EXPERTS)]
    lrow = [sel[e:e + 1, :] for e in range(N_EXPERTS)]
    tops = [_top2(lrow[g * EXPERTS_PER_GROUP:(g + 1) * EXPERTS_PER_GROUP])
            for g in range(N_EXPERT_GROUPS)]
    gscore = [tp[0] + tp[1] for tp in tops]
    gidx = _first_of(gscore, functools.reduce(jnp.maximum, gscore))
    e1 = gidx * EXPERTS_PER_GROUP + _pick(gidx, [tp[2] for tp in tops])
    e2 = gidx * EXPERTS_PER_GROUP + _pick(gidx, [tp[3] for tp in tops])
    w1 = _pick(e1, srow)
    w2 = _pick(e2, srow)
    tot = w1 + w2
    wt_ref[0:1, :] = w1 / tot
    wt_ref[1:2, :] = w2 / tot

    @pl.when(pl.program_id(0) == 0)
    def _():
        cnt_ref[...] = jnp.zeros_like(cnt_ref)

    for e in range(N_EXPERTS):
        member[e:e + 1, :] = jnp.where(jnp.logical_or(e1 == e, e2 == e), 1.0, 0.0)
    m = member[...]
    incl = _dot(m.astype(BF16), tri_ref[...])
    excl = incl - m + cnt_ref[:, 0:1]
    cnt_ref[...] = cnt_ref[...] + incl[:, TM - 1:TM]
    erow = [excl[e:e + 1, :] for e in range(N_EXPERTS)]
    pos_ref[0:1, :] = e1 * expert_rows + _pick(e1, erow).astype(jnp.int32)
    pos_ref[1:2, :] = e2 * expert_rows + _pick(e2, erow).astype(jnp.int32)


def _out_router(x2d, a, b, w_out, g1, ng, sc, sh, rw_stack, rb, seq):
    t, d = x2d.shape
    per_b = seq // TM
    vec = pl.BlockSpec((1, 1, d), lambda i: (i // per_b, 0, 0))
    full = lambda shape: pl.BlockSpec(shape, lambda i: (0, 0))
    half = a.shape[1]
    idx = jnp.arange(TM)
    tri = (idx[:, None] <= idx[None, :]).astype(BF16)
    return pl.pallas_call(
        functools.partial(_out_router_kernel, t),
        grid=(t // TM,),
        in_specs=[
            pl.BlockSpec((TM, d), lambda i: (i, 0)),
            pl.BlockSpec((TM, half), lambda i: (i, 0)),
            pl.BlockSpec((TM, half), lambda i: (i, 0)),
            full((2 * half, d)),
            vec, full((1, d)), vec, vec,
            full((2 * N_EXPERTS, d)), full((N_EXPERTS, 1)), full((TM, TM)),
        ],
        out_specs=[
            pl.BlockSpec((TM, d), lambda i: (i, 0)),
            pl.BlockSpec((TM, d), lambda i: (i, 0)),
            pl.BlockSpec((2, TM), lambda i: (0, i)),
            pl.BlockSpec((2, TM), lambda i: (0, i)),
            pl.BlockSpec((N_EXPERTS, LANES), lambda i: (0, 0)),
        ],
        out_shape=[
            jax.ShapeDtypeStruct((t, d), F32),
            jax.ShapeDtypeStruct((t, d), F32),
            jax.ShapeDtypeStruct((2, t), jnp.int32),
            jax.ShapeDtypeStruct((2, t), F32),
            jax.ShapeDtypeStruct((N_EXPERTS, LANES), F32),
        ],
        scratch_shapes=[pltpu.VMEM((N_EXPERTS, TM), F32)],
        compiler_params=_cparams("arbitrary"),
        name="out_router",
    )(x2d, a, b, w_out, g1, ng, sc, sh, rw_stack, rb, tri)


def _row_copy(src, src_row, dst, dst_row, sem):
    return pltpu.make_async_copy(src.at[pl.ds(src_row, 1), :], dst.at[pl.ds(dst_row, 1), :], sem)


def _dispatch_kernel(pos_ref, hp_ref, xs_ref, sem):
    n_tok = pos_ref.shape[0] // 2
    base = pl.program_id(0) * TM

    def issue(r, carry):
        for pick in range(2):
            _row_copy(hp_ref, r, xs_ref, pos_ref[pick * n_tok + base + r], sem).start(priority=pick)
        return carry

    lax.fori_loop(0, TM, issue, 0, unroll=8)
    for _ in range(2):
        pltpu.make_async_copy(hp_ref, xs_ref.at[pl.ds(0, TM), :], sem).wait()


def _dispatch(pos_flat, hp, rows):
    t, width = hp.shape
    return pl.pallas_call(
        _dispatch_kernel,
        grid_spec=pltpu.PrefetchScalarGridSpec(
            num_scalar_prefetch=1,
            grid=(t // TM,),
            in_specs=[pl.BlockSpec((TM, width), lambda i, pos: (i, 0))],
            out_specs=pl.BlockSpec(memory_space=pl.ANY),
            scratch_shapes=[pltpu.SemaphoreType.DMA],
        ),
        out_shape=jax.ShapeDtypeStruct((rows, width), F32),
        compiler_params=_cparams("arbitrary"),
        name="moe_dispatch",
    )(pos_flat, hp)


def _expert_kernel(te_ref, tb_ref, tv_ref, xs_ref, wg_ref, wu_ref, wd_ref, ys_ref,
                   wg_bf, wu_bf, wd_bf):
    i = pl.program_id(0)
    expert = te_ref[i]
    valid = tv_ref[i]

    @pl.when(jnp.logical_or(i == 0, expert != te_ref[jnp.maximum(i - 1, 0)]))
    def _():
        wg_bf[...] = wg_ref[0].astype(BF16)
        wu_bf[...] = wu_ref[0].astype(BF16)
        wd_bf[...] = wd_ref[0].astype(BF16)

    @pl.when(valid > 0)
    def _():
        row = lax.broadcasted_iota(jnp.int32, xs_ref.shape, 0)
        h = jnp.where(row < valid, xs_ref[...], 0.0).astype(BF16)
        gate = _dot(h, wg_bf[...])
        up = _dot(h, wu_bf[...])
        act = (gate * _sigmoid(gate) * up).astype(BF16)
        ys_ref[...] = _dot(act, wd_bf[...])

    @pl.when(valid == 0)
    def _():
        ys_ref[...] = jnp.zeros_like(ys_ref)


def _experts(tile_expert, tile_block, tile_valid, xs, layer, wg, wu, wd):
    rows, width = xs.shape
    d = width
    n_tiles = tile_expert.shape[0]
    wspec = lambda shape: pl.BlockSpec((None,) + shape, lambda i, te, tb, tv: (layer, te[i], 0, 0))
    return pl.pallas_call(
        _expert_kernel,
        grid_spec=pltpu.PrefetchScalarGridSpec(
            num_scalar_prefetch=3,
            grid=(n_tiles,),
            in_specs=[
                pl.BlockSpec((TM_MOE, width), lambda i, te, tb, tv: (tb[i], 0)),
                wspec((1, d, D_EXPERT)), wspec((1, d, D_EXPERT)), wspec((1, D_EXPERT, d)),
            ],
            out_specs=pl.BlockSpec((TM_MOE, width), lambda i, te, tb, tv: (tb[i], 0)),
            scratch_shapes=[pltpu.VMEM((d, D_EXPERT), BF16), pltpu.VMEM((d, D_EXPERT), BF16),
                            pltpu.VMEM((D_EXPERT, d), BF16)],
        ),
        out_shape=jax.ShapeDtypeStruct((rows, width), F32),
        compiler_params=_cparams("arbitrary"),
        name="moe_experts",
    )(tile_expert, tile_block, tile_valid, xs, wg, wu, wd)


def _combine_kernel(final, pos_ref, x1_ref, wt_ref, g2_ref, fn_ref, ys_ref, o_ref, buf, sems):
    n_tok = pos_ref.shape[0] // 2
    i = pl.program_id(0)
    n = pl.num_programs(0)

    def gather(tile, slot):
        base = tile * TM

        def issue(r, carry):
            for pick in range(2):
                _row_copy(ys_ref, pos_ref[pick * n_tok + base + r], buf.at[slot, pick], r,
                          sems.at[slot]).start(priority=pick)
            return carry

        lax.fori_loop(0, TM, issue, 0, unroll=8)

    @pl.when(i == 0)
    def _():
        gather(0, 0)

    @pl.when(i + 1 < n)
    def _():
        gather(i + 1, (i + 1) % 2)

    slot = i % 2
    for pick in range(2):
        pltpu.make_async_copy(ys_ref.at[pl.ds(0, TM), :], buf.at[slot, pick], sems.at[slot]).wait()
    y = wt_ref[:, 0:1] * buf[slot, 0] + wt_ref[:, 1:2] * buf[slot, 1]
    x2 = x1_ref[...] + g2_ref[0] * y
    if final:
        ms = jnp.mean(x2 * x2, axis=-1, keepdims=True)
        x2 = x2 * lax.rsqrt(ms + RMS_EPS) * fn_ref[...]
    o_ref[...] = x2


def _combine(pos_flat, x1, wt, g2, fn, ys, seq, final):
    t, d = x1.shape
    per_b = seq // TM
    return pl.pallas_call(
        functools.partial(_combine_kernel, final),
        grid_spec=pltpu.PrefetchScalarGridSpec(
            num_scalar_prefetch=1,
            grid=(t // TM,),
            in_specs=[
                pl.BlockSpec((TM, d), lambda i, pos: (i, 0)),
                pl.BlockSpec((TM, 2), lambda i, pos: (i, 0)),
                pl.BlockSpec((1, 1, d), lambda i, pos: (i // per_b, 0, 0)),
                pl.BlockSpec((1, d), lambda i, pos: (0, 0)),
                pl.BlockSpec(memory_space=pl.ANY),
            ],
            out_specs=pl.BlockSpec((TM, d), lambda i, pos: (i, 0)),
            scratch_shapes=[pltpu.VMEM((2, 2, TM, d), F32),
                            pltpu.SemaphoreType.DMA((2,))],
        ),
        out_shape=jax.ShapeDtypeStruct((t, d), F32),
        compiler_params=_cparams("arbitrary"),
        name="moe_combine_final" if final else "moe_combine",
    )(pos_flat, x1, wt, g2, fn, ys)


def _tile_schedule(counts, n_tok):
    blocks_per_expert = n_tok // TM_MOE
    n_tiles = 2 * n_tok // TM_MOE + N_EXPERTS
    tiles = (counts + TM_MOE - 1) // TM_MOE
    ends = jnp.cumsum(tiles)
    starts = ends - tiles
    j = jnp.arange(n_tiles, dtype=jnp.int32)
    used = j < ends[-1]
    e = jnp.minimum(jnp.sum((j[:, None] >= ends[None, :]).astype(jnp.int32), axis=1), N_EXPERTS - 1)
    local = j - starts[e]
    last_e = jnp.max(jnp.where(tiles > 0, jnp.arange(N_EXPERTS, dtype=jnp.int32), 0))
    tile_expert = jnp.where(used, e, last_e)
    tile_block = jnp.where(used, e * blocks_per_expert + local, N_EXPERTS * blocks_per_expert)
    tile_valid = jnp.where(used, jnp.clip(counts[e] - local * TM_MOE, 0, TM_MOE), 0)
    return tile_expert.astype(jnp.int32), tile_block.astype(jnp.int32), tile_valid.astype(jnp.int32)


def _moe(x1, hp, pos, wt, counts, layer, wg, wu, wd, g2, fn, seq, final):
    t = x1.shape[0]
    rows = N_EXPERTS * t + TM_MOE
    pos_flat = pos.reshape(-1)
    te, tb, tv = _tile_schedule(counts[:, 0].astype(jnp.int32), t)
    xs = _dispatch(pos_flat, hp, rows)
    ys = _experts(te, tb, tv, xs, layer, wg, wu, wd)
    return _combine(pos_flat, x1, wt.T, g2, fn, ys, seq, final)


def _log_sigmoid(x):
    return jnp.minimum(x, 0.0) - jnp.log1p(jnp.exp(-jnp.abs(x)))


def _odd_in_kernel(x_ref, g_ref, sc_ref, sh_ref, w_ref, gw_ref, gb_ref,
                   q_ref, k_ref, v_ref, sr_ref, la_ref, up_ref):
    h = _modulated_rmsnorm(x_ref[...], g_ref[...], sc_ref[0], sh_ref[0]).astype(BF16)
    q_ref[...] = _dot(h, w_ref[:, 0:512]) * (GLA_DK ** -0.5)
    k_ref[...] = _dot(h, w_ref[:, 512:1024])
    v_ref[...] = _dot(h, w_ref[:, 1024:1536]).astype(BF16)
    r = _dot(h, w_ref[:, 1536:2048])
    sr_ref[...] = r * _sigmoid(r)
    up_ref[...] = _dot(h, w_ref[:, 2048:2560])
    a_low = _dot(h, w_ref[:, 2560:2688]).astype(BF16)
    la_ref[...] = _log_sigmoid(_dot(a_low, gw_ref[...]) + gb_ref[...]) * (1.0 / GLA_TAU)


def _odd_in(x2d, g, sc, sh, w_cat, gate_w, gate_b, seq):
    t, d = x2d.shape
    n = w_cat.shape[1]
    per_b = seq // TM
    vec = pl.BlockSpec((1, 1, d), lambda i: (i // per_b, 0, 0))
    full = lambda shape: pl.BlockSpec(shape, lambda i: (0, 0))
    out = lambda dt: jax.ShapeDtypeStruct((t, 512), dt)
    return pl.pallas_call(
        _odd_in_kernel,
        grid=(t // TM,),
        in_specs=[
            pl.BlockSpec((TM, d), lambda i: (i, 0)),
            full((1, d)), vec, vec, full((d, n)),
            full((LANES, 512)), full((1, 512)),
        ],
        out_specs=[pl.BlockSpec((TM, 512), lambda i: (i, 0))] * 6,
        out_shape=[out(F32), out(F32), out(BF16), out(F32), out(F32), out(F32)],
        compiler_params=_cparams("arbitrary"),
        name="odd_in",
    )(x2d, g, sc, sh, w_cat, gate_w, gate_b)


def _gla_kernel(q_ref, k_ref, la_ref, v_ref, sr_ref, g_ref, fwd_ref, rev_ref, o_ref, st_ref):
    group = GLA_CHUNK * GLA_UNROLL
    row = lax.broadcasted_iota(jnp.int32, (group, group), 0)
    col = lax.broadcasted_iota(jnp.int32, (group, group), 1)
    shift = GLA_CHUNK.bit_length() - 1
    keep = jnp.logical_and(jnp.right_shift(row, shift) == jnp.right_shift(col, shift), col <= row)

    @pl.when(pl.program_id(1) == 0)
    def _():
        st_ref[...] = jnp.zeros_like(st_ref)

    def chunk_group(n, carry):
        rows = pl.ds(pl.multiple_of(n * group, group), group)
        la = la_ref[rows, :]
        b_all = _split_dot_left(fwd_ref[...], la)
        r_all = _split_dot_left(rev_ref[...], la)
        for hd in range(GLA_HEADS):
            ls = slice(hd * LANES, (hd + 1) * LANES)
            b = b_all[:, ls]
            k = k_ref[rows, ls]
            v = v_ref[rows, ls]
            q_in = (q_ref[rows, ls] * jnp.exp(b)).astype(BF16)
            k_in = (k * jnp.exp(-b)).astype(BF16)
            k_dec = (k * jnp.exp(r_all[:, ls])).astype(BF16)
            scores = jnp.where(keep, _dot_nt(q_in, k_in), 0.0).astype(BF16)
            o_intra = _dot(scores, v)
            state_t = st_ref[hd]
            outs = []
            for c in range(GLA_UNROLL):
                cs = slice(c * GLA_CHUNK, (c + 1) * GLA_CHUNK)
                outs.append(o_intra[cs, :] + _dot_nt(q_in[cs, :], state_t.astype(BF16)))
                b_last = b[(c + 1) * GLA_CHUNK - 1:(c + 1) * GLA_CHUNK, :]
                state_t = state_t * jnp.exp(b_last) + _dot_tn(v[cs, :], k_dec[cs, :])
            st_ref[hd] = state_t
            o = jnp.concatenate(outs, axis=0)
            o = o * lax.rsqrt(jnp.mean(o * o, axis=-1, keepdims=True) + RMS_EPS)
            o_ref[rows, ls] = (o * g_ref[:, ls] * sr_ref[rows, ls]).astype(BF16)
        return carry

    lax.fori_loop(0, TS_GLA // group, chunk_group, 0)


def _gla(q, k, la, v, sr, norm_g, bsz, seq):
    t, width = q.shape
    per_b = seq // TS_GLA
    group = GLA_CHUNK * GLA_UNROLL
    idx = jnp.arange(group)
    same = (idx[:, None] // GLA_CHUNK) == (idx[None, :] // GLA_CHUNK)
    fwd = jnp.logical_and(same, idx[:, None] >= idx[None, :]).astype(BF16)
    rev = jnp.logical_and(same, idx[:, None] < idx[None, :]).astype(BF16)
    blk = pl.BlockSpec((TS_GLA, width), lambda b, i: (b * per_b + i, 0))
    tri_spec = pl.BlockSpec((group, group), lambda b, i: (0, 0))
    return pl.pallas_call(
        _gla_kernel,
        grid=(bsz, per_b),
        in_specs=[blk, blk, blk, blk, blk,
                  pl.BlockSpec((1, width), lambda b, i: (0, 0)),
                  tri_spec, tri_spec],
        out_specs=blk,
        out_shape=jax.ShapeDtypeStruct((t, width), BF16),
        scratch_shapes=[pltpu.VMEM((GLA_HEADS, GLA_DV, LANES), F32)],
        compiler_params=_cparams("arbitrary", "arbitrary"),
        name="gla",
    )(q, k, la, v, sr, norm_g.reshape(1, -1), fwd, rev)


def _pool_kernel(u_ref, halo_ref, w_ref, b_ref, s_ref, o_ref, buf):
    i = pl.program_id(1)
    buf[0:POOL_HALO, :] = jnp.where(i > 0, halo_ref[...], 0.0)
    buf[POOL_HALO:, :] = u_ref[...]
    pos = (i * TS_POOL + 1 + lax.broadcasted_iota(jnp.int32, (TS_POOL, 1), 0)).astype(F32)
    for gi, win in enumerate(POOL_WINDOWS):
        ls = slice(gi * LANES, (gi + 1) * LANES)
        tok = buf[POOL_HALO:, ls]
        tot = tok
        for back in range(1, win):
            tot = tot + buf[POOL_HALO - back:POOL_HALO - back + TS_POOL, ls]
        p = tot / jnp.minimum(pos, float(win)) - tok
        y = _dot(p.astype(BF16), w_ref[gi]) + b_ref[:, ls]
        o_ref[:, ls] = (y * s_ref[:, ls]).astype(BF16)


def _pool(u2d, pool_w, pool_b, pool_scale, bsz, seq):
    t = u2d.shape[0]
    per_b = seq // TS_POOL
    ratio = TS_POOL // POOL_HALO
    full = lambda shape: pl.BlockSpec(shape, lambda b, i: (0,) * len(shape))
    return pl.pallas_call(
        _pool_kernel,
        grid=(bsz, per_b),
        in_specs=[
            pl.BlockSpec((TS_POOL, D_POOL), lambda b, i: (b * per_b + i, 0)),
            pl.BlockSpec((POOL_HALO, D_POOL),
                         lambda b, i: (jnp.maximum((b * per_b + i) * ratio - 1, 0), 0)),
            full((len(POOL_WINDOWS), LANES, LANES)), full((1, D_POOL)), full((1, D_POOL)),
        ],
        out_specs=pl.BlockSpec((TS_POOL, D_POOL), lambda b, i: (b * per_b + i, 0)),
        out_shape=jax.ShapeDtypeStruct((t, D_POOL), BF16),
        scratch_shapes=[pltpu.VMEM((TS_POOL + POOL_HALO, D_POOL), F32)],
        compiler_params=_cparams("arbitrary", "arbitrary"),
        name="pool_mixer",
    )(u2d, u2d, pool_w.astype(BF16), pool_b.reshape(1, D_POOL), pool_scale.reshape(1, D_POOL))


def _pad_heads(w, heads, dim):
    lead = w.shape[:-1]
    w = w.reshape(lead + (heads, dim))
    w = jnp.pad(w, [(0, 0)] * len(lead) + [(0, 0), (0, LANES - dim)])
    return w.reshape(lead + (heads * LANES,))


def _odd_weights(w_in, gate_w, gate_b):
    dkt = GLA_HEADS * GLA_DK
    dvt = GLA_HEADS * GLA_DV
    o = [0, dkt, 2 * dkt, 2 * dkt + dvt, 2 * dkt + 2 * dvt, 2 * dkt + 2 * dvt + GLA_RANK]
    wq, wk, wv, wr, wa, wp = [w_in[:, o[j]:(o + [w_in.shape[1]])[j + 1]] for j in range(6)]
    w_cat = jnp.concatenate([
        _pad_heads(wq, GLA_HEADS, GLA_DK), _pad_heads(wk, GLA_HEADS, GLA_DK), wv, wr, wp,
        jnp.pad(wa, ((0, 0), (0, LANES - GLA_RANK)))], axis=1).astype(BF16)
    gw = jnp.pad(_pad_heads(gate_w, GLA_HEADS, GLA_DK), ((0, LANES - GLA_RANK), (0, 0))).astype(BF16)
    gb = _pad_heads(gate_b, GLA_HEADS, GLA_DK).reshape(1, -1)
    return w_cat, gw, gb


def kernel(x, c, ada_w, ada_b, norm_mix, norm_ffn, w_in_even, w_out_even, conv_w, conv_b, conv_norm_g, conv_norm_b, w_in_odd, w_out_odd, gla_gate_w, gla_gate_b, gla_norm_g, pool_w, pool_b, pool_scale, router_w, router_bias, moe_w_gate, moe_w_up, moe_w_down, final_norm):
    bsz, seq, d = x.shape
    t = bsz * seq
    depth = ada_w.shape[0]
    mod = _ada(c, ada_w, ada_b)
    rw_hi = router_w.T.astype(BF16)
    rw_lo = (router_w.T - rw_hi.astype(F32)).astype(BF16)
    rw_stack = jnp.concatenate([rw_hi, rw_lo], axis=0)
    rb =router_bias.reshape(N_EXPERTS, 1)
    fn = final_norm.reshape(1, d)
    x2d = x.reshape(t, d)
    for l in range(depth):
        sh1, sc1, g1, sh2, sc2, g2 = [mod[l, :, j * d:(j + 1) * d].reshape(bsz, 1, d)
                                      for j in range(6)]
        i = l // 2
        nm = norm_mix[l].reshape(1, d)
        if l % 2 == 0:
            y, q, k, v = _even_in(x2d, nm, sc1, sh1, w_in_even[i].astype(BF16), seq)
            a = _conv(y, conv_w[i], conv_b[i], conv_norm_g[i], conv_norm_b[i], bsz, seq)
            b = _sb_attention(q, k, v, bsz, seq)
            w_out = w_out_even[i]
        else:
            w_cat, gw, gb = _odd_weights(w_in_odd[i], gla_gate_w[i], gla_gate_b[i])
            q, k, v, sr, la, up = _odd_in(x2d, nm, sc1, sh1, w_cat, gw, gb, seq)
            a = _gla(q, k, la, v, sr, gla_norm_g[i], bsz, seq)
            b = _pool(up, pool_w[i], pool_b[i], pool_scale[i], bsz, seq)
            w_out = w_out_odd[i]
        x1, hp, pos, wt, counts = _out_router(x2d, a, b, w_out.astype(BF16), g1,
                                              norm_ffn[l].reshape(1, d), sc2, sh2,
                                              rw_stack, rb, seq)
        x2d = _moe(x1, hp, pos, wt, counts, l, moe_w_gate, moe_w_up, moe_w_down, g2, fn, seq,
                   final=(l == depth - 1))
    return x2d.reshape(bsz, seq, d)
```

```python
import functools

import jax
import jax.numpy as jnp
from jax import lax
from jax.experimental import pallas as pl
from jax.experimental.pallas import tpu as pltpu

F32 = jnp.float32
BF16 = jnp.bfloat16

D_MODEL = 1024
D_CONV = 512
CONV_WIDTH = 31
CONV_GROUP_SIZE = 64
SB_HEADS = 8
SB_HEAD_DIM = 64
GLA_HEADS = 4
GLA_DK = 64
GLA_DV = 128
GLA_RANK = 16
GLA_TAU = 16.0
GLA_CHUNK = 64
D_POOL = 512
POOL_WINDOWS = (2, 4, 8, 16)
N_EXPERTS = 16
N_EXPERT_GROUPS = 4
EXPERTS_PER_GROUP = 4
D_EXPERT = 512
RMS_EPS = 1e-6
LN_EPS = 1e-5

LANES = 128
SUBLANES = 8
VMEM_LIMIT = 56 * 1024 * 1024

TM = 512
TS_CONV = 256
CONV_HALO = 32
CONV_ROWS = 64
TS_GLA = 1024
GLA_UNROLL = 4
TS_POOL = 512
POOL_HALO = 16
TQ = 256
TK = 128
TM_MOE = 512
SB_DEAD = -120.0


def _cparams(*sem):
    return pltpu.CompilerParams(dimension_semantics=sem, vmem_limit_bytes=VMEM_LIMIT)


def _sigmoid(x):
    return 1.0 / (1.0 + jnp.exp(-x))


def _dot(a, b):
    return jnp.dot(a, b, preferred_element_type=F32)


def _dot_nt(a, b, **kw):
    return lax.dot_general(a, b, (((1,), (1,)), ((), ())), preferred_element_type=F32, **kw)


def _dot_tn(a, b):
    return lax.dot_general(a, b, (((0,), (0,)), ((), ())), preferred_element_type=F32)


def _split_dot(x, m):
    hi = x.astype(BF16)
    lo = (x - hi.astype(F32)).astype(BF16)
    return _dot(hi, m) + _dot(lo, m)


def _split_dot_left(m, x):
    hi = x.astype(BF16)
    lo = (x - hi.astype(F32)).astype(BF16)
    return _dot(m, hi) + _dot(m, lo)


def _modulated_rmsnorm(x, g, sc, sh):
    ms = jnp.mean(x * x, axis=-1, keepdims=True)
    return (x * lax.rsqrt(ms + RMS_EPS) * g) * (1.0 + sc) + sh


def _ada_kernel(c_ref, w_ref, b_ref, o_ref):
    c = c_ref[...]
    cond = c * _sigmoid(c)
    o_ref[0] = jnp.dot(cond, w_ref[0], preferred_element_type=F32,
                       precision=lax.Precision.HIGHEST) + b_ref[0]


def _ada(c, ada_w, ada_b):
    depth, d, n = ada_w.shape
    bsz = c.shape[0]
    tn = 1536
    return pl.pallas_call(
        _ada_kernel,
        grid=(depth, n // tn),
        in_specs=[
            pl.BlockSpec((bsz, d), lambda l, j: (0, 0)),
            pl.BlockSpec((1, d, tn), lambda l, j: (l, 0, j)),
            pl.BlockSpec((1, 1, tn), lambda l, j: (l, 0, j)),
        ],
        out_specs=pl.BlockSpec((1, bsz, tn), lambda l, j: (l, 0, j)),
        out_shape=jax.ShapeDtypeStruct((depth, bsz, n), F32),
        compiler_params=_cparams("arbitrary", "arbitrary"),
        name="ada_mod",
    )(c, ada_w, ada_b.reshape(depth, 1, n))


def _even_in_kernel(x_ref, g_ref, sc_ref, sh_ref, w_ref, y_ref, q_ref, k_ref, v_ref):
    h = _modulated_rmsnorm(x_ref[...], g_ref[...], sc_ref[0], sh_ref[0]).astype(BF16)
    val = _dot(h, w_ref[:, 0:512])
    gate = _dot(h, w_ref[:, 512:1024])
    y_ref[...] = val * _sigmoid(gate)
    q_ref[...] = (_dot(h, w_ref[:, 1024:1536]) * (SB_HEAD_DIM ** -0.5)).astype(BF16)
    k_ref[...] = _dot(h, w_ref[:, 1536:2048]).astype(BF16)
    v_ref[...] = _dot(h, w_ref[:, 2048:2560]).astype(BF16)


def _even_in(x2d, g, sc, sh, w_in, seq):
    t, d = x2d.shape
    n = w_in.shape[1]
    per_b = seq // TM
    vec = pl.BlockSpec((1, 1, d), lambda i: (i // per_b, 0, 0))
    out = lambda dt: jax.ShapeDtypeStruct((t, 512), dt)
    return pl.pallas_call(
        _even_in_kernel,
        grid=(t // TM,),
        in_specs=[
            pl.BlockSpec((TM, d), lambda i: (i, 0)),
            pl.BlockSpec((1, d), lambda i: (0, 0)),
            vec, vec,
            pl.BlockSpec((d, n), lambda i: (0, 0)),
        ],
        out_specs=[pl.BlockSpec((TM, 512), lambda i: (i, 0))] * 4,
        out_shape=[out(F32), out(BF16), out(BF16), out(BF16)],
        compiler_params=_cparams("arbitrary"),
        name="even_in",
    )(x2d, g, sc, sh, w_in)


def _conv_kernel(y_ref, halo_ref, w_ref, b_ref, ng_ref, nb_ref, avg_ref, o_ref, buf, sh, cv):
    i = pl.program_id(1)
    buf[0:CONV_HALO, :] = jnp.where(i > 0, halo_ref[...], 0.0)
    buf[CONV_HALO:, :] = y_ref[...]
    sh_rows = sh.shape[1]
    for r in range(1, SUBLANES):
        sh[r - 1] = buf[r:r + sh_rows, :]
    first = CONV_HALO - (CONV_WIDTH - 1)
    for rc in range(TS_CONV // CONV_ROWS):
        r0 = rc * CONV_ROWS
        for lc in range(D_CONV // LANES):
            ls = slice(lc * LANES, (lc + 1) * LANES)
            acc = jnp.zeros((CONV_ROWS, LANES), F32)
            for j in range(CONV_WIDTH):
                whole, r = divmod(first + j, SUBLANES)
                s = r0 + whole * SUBLANES
                src = buf[s:s + CONV_ROWS, ls] if r == 0 else sh[r - 1, s:s + CONV_ROWS, ls]
                acc = acc + w_ref[j:j + 1, ls] * src
            cv[r0:r0 + CONV_ROWS, ls] = acc + b_ref[:, ls]
    y = cv[...]
    avg = avg_ref[...]
    d = y - _split_dot(y, avg)
    var = _split_dot(d * d, avg)
    yn = d * lax.rsqrt(var + LN_EPS) * ng_ref[...] + nb_ref[...]
    o_ref[...] = (yn * _sigmoid(yn)).astype(BF16)


def _conv(y2d, conv_w, conv_b, norm_g, norm_b, bsz, seq):
    t = y2d.shape[0]
    per_b = seq // TS_CONV
    ratio = TS_CONV // CONV_HALO
    gid = jnp.arange(D_CONV) // CONV_GROUP_SIZE
    avg = jnp.where(gid[:, None] == gid[None, :], 1.0 / CONV_GROUP_SIZE, 0.0).astype(BF16)
    w_pad = jnp.pad(conv_w, ((0, 32 - CONV_WIDTH), (0, 0)))
    row = lambda a: a.reshape(1, D_CONV)
    full = lambda shape: pl.BlockSpec(shape, lambda b, i: (0, 0))
    return pl.pallas_call(
        _conv_kernel,
        grid=(bsz, per_b),
        in_specs=[
            pl.BlockSpec((TS_CONV, D_CONV), lambda b, i: (b * per_b + i, 0)),
            pl.BlockSpec((CONV_HALO, D_CONV),
                         lambda b, i: (jnp.maximum((b * per_b + i) * ratio - 1, 0), 0)),
            full((32, D_CONV)), full((1, D_CONV)), full((1, D_CONV)), full((1, D_CONV)),
            full((D_CONV, D_CONV)),
        ],
        out_specs=pl.BlockSpec((TS_CONV, D_CONV), lambda b, i: (b * per_b + i, 0)),
        out_shape=jax.ShapeDtypeStruct((t, D_CONV), BF16),
        scratch_shapes=[pltpu.VMEM((TS_CONV + CONV_HALO, D_CONV), F32),
                        pltpu.VMEM((SUBLANES - 1, TS_CONV + CONV_HALO - SUBLANES, D_CONV), F32),
                        pltpu.VMEM((TS_CONV, D_CONV), F32)],
        compiler_params=_cparams("arbitrary", "arbitrary"),
        name="conv_module",
    )(y2d, y2d, w_pad, row(conv_b), row(norm_g), row(norm_b), avg)


def _sb_kernel(q_ref, k_ref, v_ref, u_ref, o_ref, acc_ref, cr_ref):
    qi = pl.program_id(2)
    q = q_ref[...]
    lane = lax.broadcasted_iota(jnp.int32, (TQ, LANES), 1)
    row = lax.broadcasted_iota(jnp.int32, (TQ, TK), 0)
    col = lax.broadcasted_iota(jnp.int32, (TQ, TK), 1)
    qm = [jnp.where(lane < SB_HEAD_DIM, q, jnp.zeros_like(q)),
          jnp.where(lane >= SB_HEAD_DIM, q, jnp.zeros_like(q))]
    acc_ref[...] = jnp.zeros_like(acc_ref)
    cr_ref[...] = jnp.zeros_like(cr_ref)
    per_q = TQ // TK

    def key_blocks(first_kb, count, n_diagonal):
        for hd in range(2):
            cr = cr_ref[hd]
            acc = acc_ref[hd]
            for step in range(count):
                kb = first_kb - step
                ks = pl.multiple_of(kb * TK, TK)
                z = _dot_nt(qm[hd], k_ref[pl.ds(ks, TK), :])
                l1p = jnp.log(1.0 + jnp.exp(-jnp.abs(z)))
                log_keep = -jnp.maximum(z, 0.0) - l1p
                if step < n_diagonal:
                    strict = (col + (kb * TK - qi * TQ)) < row
                    log_keep = jnp.where(strict, log_keep, 0.0)
                later = _split_dot(log_keep, u_ref[...]) + cr
                a = jnp.exp(jnp.minimum(z, 0.0) - l1p + later)
                if step < n_diagonal:
                    a = jnp.where(strict, a, 0.0)
                acc = acc + _dot(a.astype(BF16), v_ref[pl.ds(ks, TK), :])
                cr = cr + jnp.sum(log_keep, axis=-1, keepdims=True)
            cr_ref[hd] = cr
            acc_ref[hd] = acc

    def alive():
        return (jnp.max(cr_ref[...]) > SB_DEAD).astype(jnp.int32)

    last_diag = qi * per_q + per_q - 1

    @pl.when(qi == 0)
    def _():
        key_blocks(last_diag, per_q, per_q)

    @pl.when(qi > 0)
    def _():
        key_blocks(last_diag, 2 * per_q, per_q)

    def cond(carry):
        kb, live = carry
        return jnp.logical_and(kb >= 0, live > 0)

    def body(carry):
        kb, _ = carry
        key_blocks(kb, 1, 0)
        return kb - 1, alive()

    lax.while_loop(cond, body, (last_diag - 2 * per_q, alive()))
    o_ref[...] = jnp.where(lane < SB_HEAD_DIM, acc_ref[0], acc_ref[1]).astype(BF16)


def _sb_attention(q, k, v, bsz, seq):
    t = q.shape[0]
    nq = seq // TQ
    idx = jnp.arange(TK)
    upper = (idx[:, None] > idx[None, :]).astype(BF16)
    kv = pl.BlockSpec((seq, LANES), lambda b, p, i: (b, p))
    return pl.pallas_call(
        _sb_kernel,
        grid=(bsz, SB_HEADS // 2, nq),
        in_specs=[
            pl.BlockSpec((TQ, LANES), lambda b, p, i: (b * nq + i, p)),
            kv, kv,
            pl.BlockSpec((TK, TK), lambda b, p, i: (0, 0)),
        ],
        out_specs=pl.BlockSpec((TQ, LANES), lambda b, p, i: (b * nq + i, p)),
        out_shape=jax.ShapeDtypeStruct((t, SB_HEADS * SB_HEAD_DIM), BF16),
        scratch_shapes=[pltpu.VMEM((2, TQ, LANES), F32), pltpu.VMEM((2, TQ, 1), F32)],
        compiler_params=_cparams("arbitrary", "arbitrary", "arbitrary"),
        name="stick_breaking",
    )(q, k, v, upper)


def _first_of(vals, target):
    idx = jnp.full(target.shape, len(vals) - 1, jnp.int32)
    for j in range(len(vals) - 2, -1, -1):
        idx = jnp.where(vals[j] == target, j, idx)
    return idx


def _top2(vals):
    m1 = functools.reduce(jnp.maximum, vals)
    i1 = _first_of(vals, m1)
    rest = [jnp.where(i1 == j, -jnp.inf, v) for j, v in enumerate(vals)]
    m2 = functools.reduce(jnp.maximum, rest)
    i2 = _first_of(rest, m2)
    return m1, m2, i1, i2


def _pick(idx, vals):
    out = vals[-1]
    for j in range(len(vals) - 2, -1, -1):
        out = jnp.where(idx == j, vals[j], out)
    return out


def _out_router_kernel(expert_rows, x_ref, a_ref, b_ref, w_ref, g1_ref, ng_ref, sc_ref, sh_ref,
                       rw_ref, rb_ref, tri_ref,
                       x1_ref, hp_ref, pos_ref, wt_ref, cnt_ref, member):
    half = a_ref.shape[1]
    mix = _dot(a_ref[...], w_ref[0:half, :]) + _dot(b_ref[...], w_ref[half:, :])
    x1 = x_ref[...] + g1_ref[0] * mix
    x1_ref[...] = x1
    h = _modulated_rmsnorm(x1, ng_ref[...], sc_ref[0], sh_ref[0])
    hb = h.astype(BF16)
    _to_slabs(hp_ref, h)

    hl = (h - hb.astype(F32)).astype(BF16)
    both = _dot_nt(rw_ref[...], hb)
    logits = (both[0:N_EXPERTS, :] + both[N_EXPERTS:, :]
              + _dot_nt(rw_ref[0:N_EXPERTS, :], hl))
    ex = jnp.exp(logits - jnp.max(logits, axis=0, keepdims=True))
    scores = ex / jnp.sum(ex, axis=0, keepdims=True)
    sel = scores + rb_ref[...]
    srow = [scores[e:e + 1, :] for e in range(N_EXPERTS)]
    lrow = [sel[e:e + 1, :] for e in range(N_EXPERTS)]
    tops = [_top2(lrow[g * EXPERTS_PER_GROUP:(g + 1) * EXPERTS_PER_GROUP])
            for g in range(N_EXPERT_GROUPS)]
    gscore = [tp[0] + tp[1] for tp in tops]
    gidx = _first_of(gscore, functools.reduce(jnp.maximum, gscore))
    e1 = gidx * EXPERTS_PER_GROUP + _pick(gidx, [tp[2] for tp in tops])
    e2 = gidx * EXPERTS_PER_GROUP + _pick(gidx, [tp[3] for tp in tops])
    w1 = _pick(e1, srow)
    w2 = _pick(e2, srow)
    tot = w1 + w2
    wt_ref[0:1, :] = w1 / tot
    wt_ref[1:2, :] = w2 / tot

    @pl.when(pl.program_id(0) == 0)
    def _():
        cnt_ref[...] = jnp.zeros_like(cnt_ref)

    for e in range(N_EXPERTS):
        member[e:e + 1, :] = jnp.where(jnp.logical_or(e1 == e, e2 == e), 1.0, 0.0)
    m = member[...]
    incl = _dot(m.astype(BF16), tri_ref[...])
    excl = incl - m + cnt_ref[:, 0:1]
    cnt_ref[...] = cnt_ref[...] + incl[:, TM - 1:TM]
    erow = [excl[e:e + 1, :] for e in range(N_EXPERTS)]
    pos_ref[0:1, :] = (e1 * expert_rows + _pick(e1, erow).astype(jnp.int32)) * SUBLANES
    pos_ref[1:2, :] = (e2 * expert_rows + _pick(e2, erow).astype(jnp.int32)) * SUBLANES


def _out_router(x2d, a, b, w_out, g1, ng, sc, sh, rw_stack, rb, seq):
    t, d = x2d.shape
    per_b = seq // TM
    vec = pl.BlockSpec((1, 1, d), lambda i: (i // per_b, 0, 0))
    full = lambda shape: pl.BlockSpec(shape, lambda i: (0, 0))
    half = a.shape[1]
    idx = jnp.arange(TM)
    tri = (idx[:, None] <= idx[None, :]).astype(BF16)
    return pl.pallas_call(
        functools.partial(_out_router_kernel, t),
        grid=(t // TM,),
        in_specs=[
            pl.BlockSpec((TM, d), lambda i: (i, 0)),
            pl.BlockSpec((TM, half), lambda i: (i, 0)),
            pl.BlockSpec((TM, half), lambda i: (i, 0)),
            full((2 * half, d)),
            vec, full((1, d)), vec, vec,
            full((2 * N_EXPERTS, d)), full((N_EXPERTS, 1)), full((TM, TM)),
        ],
        out_specs=[
            pl.BlockSpec((TM, d), lambda i: (i, 0)),
            pl.BlockSpec((TM * SUBLANES, d // SUBLANES), lambda i: (i, 0)),
            pl.BlockSpec((2, TM), lambda i: (0, i)),
            pl.BlockSpec((2, TM), lambda i: (0, i)),
            pl.BlockSpec((N_EXPERTS, LANES), lambda i: (0, 0)),
        ],
        out_shape=[
            jax.ShapeDtypeStruct((t, d), F32),
            jax.ShapeDtypeStruct((t * SUBLANES, d // SUBLANES), F32),
            jax.ShapeDtypeStruct((2, t), jnp.int32),
            jax.ShapeDtypeStruct((2, t), F32),
            jax.ShapeDtypeStruct((N_EXPERTS, LANES), F32),
        ],
        scratch_shapes=[pltpu.VMEM((N_EXPERTS, TM), F32)],
        compiler_params=_cparams("arbitrary"),
        name="out_router",
    )(x2d, a, b, w_out, g1, ng, sc, sh, rw_stack, rb, tri)


def _to_slabs(ref, x):
    m = x.shape[0]
    for j in range(SUBLANES):
        ref[pl.ds(j, m, stride=SUBLANES), :] = x[:, j * LANES:(j + 1) * LANES]


def _from_slabs(ref):
    m = ref.shape[0] // SUBLANES
    return jnp.concatenate([ref[pl.ds(j, m, stride=SUBLANES), :] for j in range(SUBLANES)],
                           axis=-1)


def _slab(ref, offset):
    return ref.at[pl.ds(pl.multiple_of(offset, SUBLANES), SUBLANES), :]


def _dispatch_kernel(pos_ref, hp_ref, xs_ref, sem):
    n_tok = pos_ref.shape[0] // 2
    base = pl.program_id(0) * TM

    def issue(r, carry):
        for pick in range(2):
            pltpu.make_async_copy(_slab(hp_ref, r * SUBLANES),
                                  _slab(xs_ref, pos_ref[pick * n_tok + base + r]),
                                  sem).start(priority=pick)
        return carry

    lax.fori_loop(0, TM, issue, 0, unroll=8)
    for _ in range(2):
        pltpu.make_async_copy(hp_ref, xs_ref.at[pl.ds(0, TM * SUBLANES), :], sem).wait()


def _dispatch(pos_flat, hp, rows):
    t = hp.shape[0] // SUBLANES
    return pl.pallas_call(
        _dispatch_kernel,
        grid_spec=pltpu.PrefetchScalarGridSpec(
            num_scalar_prefetch=1,
            grid=(t // TM,),
            in_specs=[pl.BlockSpec((TM * SUBLANES, LANES), lambda i, pos: (i, 0))],
            out_specs=pl.BlockSpec(memory_space=pl.ANY),
            scratch_shapes=[pltpu.SemaphoreType.DMA],
        ),
        out_shape=jax.ShapeDtypeStruct((rows * SUBLANES, LANES), F32),
        compiler_params=_cparams("arbitrary"),
        name="moe_dispatch",
    )(pos_flat, hp)


def _expert_kernel(te_ref, tb_ref, tv_ref, xs_ref, wg_ref, wu_ref, wd_ref, ys_ref,
                   wg_bf, wu_bf, wd_bf):
    i = pl.program_id(0)
    expert = te_ref[i]
    valid = tv_ref[i]

    @pl.when(jnp.logical_or(i == 0, expert != te_ref[jnp.maximum(i - 1, 0)]))
    def _():
        wg_bf[...] = wg_ref[0].astype(BF16)
        wu_bf[...] = wu_ref[0].astype(BF16)
        wd_bf[...] = wd_ref[0].astype(BF16)

    @pl.when(valid > 0)
    def _():
        x = _from_slabs(xs_ref)
        row = lax.broadcasted_iota(jnp.int32, x.shape, 0)
        h = jnp.where(row < valid, x, 0.0).astype(BF16)
        gate = _dot(h, wg_bf[...])
        up = _dot(h, wu_bf[...])
        act = (gate * _sigmoid(gate) * up).astype(BF16)
        _to_slabs(ys_ref, _dot(act, wd_bf[...]))

    @pl.when(valid == 0)
    def _():
        ys_ref[...] = jnp.zeros_like(ys_ref)


def _experts(tile_expert, tile_block, tile_valid, xs, layer, wg, wu, wd):
    d = SUBLANES * LANES
    n_tiles = tile_expert.shape[0]
    wspec = lambda shape: pl.BlockSpec((None,) + shape, lambda i, te, tb, tv: (layer, te[i], 0, 0))
    rowspec = pl.BlockSpec((TM_MOE * SUBLANES, LANES), lambda i, te, tb, tv: (tb[i], 0))
    return pl.pallas_call(
        _expert_kernel,
        grid_spec=pltpu.PrefetchScalarGridSpec(
            num_scalar_prefetch=3,
            grid=(n_tiles,),
            in_specs=[
                rowspec,
                wspec((1, d, D_EXPERT)), wspec((1, d, D_EXPERT)), wspec((1, D_EXPERT, d)),
            ],
            out_specs=rowspec,
            scratch_shapes=[pltpu.VMEM((d, D_EXPERT), BF16), pltpu.VMEM((d, D_EXPERT), BF16),
                            pltpu.VMEM((D_EXPERT, d), BF16)],
        ),
        out_shape=jax.ShapeDtypeStruct(xs.shape, F32),
        compiler_params=_cparams("arbitrary"),
        name="moe_experts",
    )(tile_expert, tile_block, tile_valid, xs, wg, wu, wd)


def _combine_kernel(final, pos_ref, x1_ref, wt_ref, g2_ref, fn_ref, ys_ref, o_ref, buf, sems):
    n_tok = pos_ref.shape[0] // 2
    i = pl.program_id(0)
    n = pl.num_programs(0)

    def gather(tile, slot):
        base = tile * TM

        def issue(r, carry):
            for pick in range(2):
                pltpu.make_async_copy(_slab(ys_ref, pos_ref[pick * n_tok + base + r]),
                                      _slab(buf.at[slot, pick], r * SUBLANES),
                                      sems.at[slot]).start(priority=pick)
            return carry

        lax.fori_loop(0, TM, issue, 0, unroll=8)

    @pl.when(i == 0)
    def _():
        gather(0, 0)

    @pl.when(i + 1 < n)
    def _():
        gather(i + 1, (i + 1) % 2)

    slot = i % 2
    for pick in range(2):
        pltpu.make_async_copy(ys_ref.at[pl.ds(0, TM * SUBLANES), :], buf.at[slot, pick],
                              sems.at[slot]).wait()
    y = (wt_ref[:, 0:1] * _from_slabs(buf.at[slot, 0])
         + wt_ref[:, 1:2] * _from_slabs(buf.at[slot, 1]))
    x2 = x1_ref[...] + g2_ref[0] * y
    if final:
        ms = jnp.mean(x2 * x2, axis=-1, keepdims=True)
        x2 = x2 * lax.rsqrt(ms + RMS_EPS) * fn_ref[...]
    o_ref[...] = x2


def _combine(pos_flat, x1, wt, g2, fn, ys, seq, final):
    t, d = x1.shape
    per_b = seq // TM
    return pl.pallas_call(
        functools.partial(_combine_kernel, final),
        grid_spec=pltpu.PrefetchScalarGridSpec(
            num_scalar_prefetch=1,
            grid=(t // TM,),
            in_specs=[
                pl.BlockSpec((TM, d), lambda i, pos: (i, 0)),
                pl.BlockSpec((TM, 2), lambda i, pos: (i, 0)),
                pl.BlockSpec((1, 1, d), lambda i, pos: (i // per_b, 0, 0)),
                pl.BlockSpec((1, d), lambda i, pos: (0, 0)),
                pl.BlockSpec(memory_space=pl.ANY),
            ],
            out_specs=pl.BlockSpec((TM, d), lambda i, pos: (i, 0)),
            scratch_shapes=[pltpu.VMEM((2, 2, TM * SUBLANES, LANES), F32),
                            pltpu.SemaphoreType.DMA((2,))],
        ),
        out_shape=jax.ShapeDtypeStruct((t, d), F32),
        compiler_params=_cparams("arbitrary"),
        name="moe_combine_final" if final else "moe_combine",
    )(pos_flat, x1, wt, g2, fn, ys)


def _tile_schedule(counts, n_tok):
    blocks_per_expert = n_tok // TM_MOE
    n_tiles = 2 * n_tok // TM_MOE + N_EXPERTS
    tiles = (counts + TM_MOE - 1) // TM_MOE
    ends = jnp.cumsum(tiles)
    starts = ends - tiles
    j = jnp.arange(n_tiles, dtype=jnp.int32)
    used = j < ends[-1]
    e = jnp.minimum(jnp.sum((j[:, None] >= ends[None, :]).astype(jnp.int32), axis=1), N_EXPERTS - 1)
    local = j - starts[e]
    last_e = jnp.max(jnp.where(tiles > 0, jnp.arange(N_EXPERTS, dtype=jnp.int32), 0))
    tile_expert = jnp.where(used, e, last_e)
    tile_block = jnp.where(used, e * blocks_per_expert + local, N_EXPERTS * blocks_per_expert)
    tile_valid = jnp.where(used, jnp.clip(counts[e] - local * TM_MOE, 0, TM_MOE), 0)
    return tile_expert.astype(jnp.int32), tile_block.astype(jnp.int32), tile_valid.astype(jnp.int32)


def _moe(x1, hp, pos, wt, counts, layer, wg, wu, wd, g2, fn, seq, final):
    t = x1.shape[0]
    rows = N_EXPERTS * t + TM_MOE
    pos_flat = pos.reshape(-1)
    te, tb, tv = _tile_schedule(counts[:, 0].astype(jnp.int32), t)
    xs = _dispatch(pos_flat, hp, rows)
    ys = _experts(te, tb, tv, xs, layer, wg, wu, wd)
    return _combine(pos_flat, x1, wt.T, g2, fn, ys, seq, final)


def _log_sigmoid(x):
    return jnp.minimum(x, 0.0) - jnp.log1p(jnp.exp(-jnp.abs(x)))


def _odd_in_kernel(x_ref, g_ref, sc_ref, sh_ref, w_ref, gw_ref, gb_ref,
                   q_ref, k_ref, v_ref, sr_ref, la_ref, up_ref):
    h = _modulated_rmsnorm(x_ref[...], g_ref[...], sc_ref[0], sh_ref[0]).astype(BF16)
    q_ref[...] = _dot(h, w_ref[:, 0:512]) * (GLA_DK ** -0.5)
    k_ref[...] = _dot(h, w_ref[:, 512:1024])
    v_ref[...] = _dot(h, w_ref[:, 1024:1536]).astype(BF16)
    r = _dot(h, w_ref[:, 1536:2048])
    sr_ref[...] = r * _sigmoid(r)
    up_ref[...] = _dot(h, w_ref[:, 2048:2560])
    a_low = _dot(h, w_ref[:, 2560:2688]).astype(BF16)
    la_ref[...] = _log_sigmoid(_dot(a_low, gw_ref[...]) + gb_ref[...]) * (1.0 / GLA_TAU)


def _odd_in(x2d, g, sc, sh, w_cat, gate_w, gate_b, seq):
    t, d = x2d.shape
    n = w_cat.shape[1]
    per_b = seq // TM
    vec = pl.BlockSpec((1, 1, d), lambda i: (i // per_b, 0, 0))
    full = lambda shape: pl.BlockSpec(shape, lambda i: (0, 0))
    out = lambda dt: jax.ShapeDtypeStruct((t, 512), dt)
    return pl.pallas_call(
        _odd_in_kernel,
        grid=(t // TM,),
        in_specs=[
            pl.BlockSpec((TM, d), lambda i: (i, 0)),
            full((1, d)), vec, vec, full((d, n)),
            full((LANES, 512)), full((1, 512)),
        ],
        out_specs=[pl.BlockSpec((TM, 512), lambda i: (i, 0))] * 6,
        out_shape=[out(F32), out(F32), out(BF16), out(F32), out(F32), out(F32)],
        compiler_params=_cparams("arbitrary"),
        name="odd_in",
    )(x2d, g, sc, sh, w_cat, gate_w, gate_b)


def _gla_kernel(q_ref, k_ref, la_ref, v_ref, sr_ref, g_ref, fwd_ref, rev_ref, o_ref, st_ref):
    group = GLA_CHUNK * GLA_UNROLL
    row = lax.broadcasted_iota(jnp.int32, (group, group), 0)
    col = lax.broadcasted_iota(jnp.int32, (group, group), 1)
    shift = GLA_CHUNK.bit_length() - 1
    keep = jnp.logical_and(jnp.right_shift(row, shift) == jnp.right_shift(col, shift), col <= row)

    @pl.when(pl.program_id(1) == 0)
    def _():
        st_ref[...] = jnp.zeros_like(st_ref)

    def chunk_group(n, carry):
        rows = pl.ds(pl.multiple_of(n * group, group), group)
        la = la_ref[rows, :]
        b_all = _split_dot_left(fwd_ref[...], la)
        r_all = _split_dot_left(rev_ref[...], la)
        for hd in range(GLA_HEADS):
            ls = slice(hd * LANES, (hd + 1) * LANES)
            b = b_all[:, ls]
            k = k_ref[rows, ls]
            v = v_ref[rows, ls]
            q_in = (q_ref[rows, ls] * jnp.exp(b)).astype(BF16)
            k_in = (k * jnp.exp(-b)).astype(BF16)
            k_dec = (k * jnp.exp(r_all[:, ls])).astype(BF16)
            scores = jnp.where(keep, _dot_nt(q_in, k_in), 0.0).astype(BF16)
            o_intra = _dot(scores, v)
            state_t = st_ref[hd]
            outs = []
            for c in range(GLA_UNROLL):
                cs = slice(c * GLA_CHUNK, (c + 1) * GLA_CHUNK)
                outs.append(o_intra[cs, :] + _dot_nt(q_in[cs, :], state_t.astype(BF16)))
                b_last = b[(c + 1) * GLA_CHUNK - 1:(c + 1) * GLA_CHUNK, :]
                state_t = state_t * jnp.exp(b_last) + _dot_tn(v[cs, :], k_dec[cs, :])
            st_ref[hd] = state_t
            o = jnp.concatenate(outs, axis=0)
            o = o * lax.rsqrt(jnp.mean(o * o, axis=-1, keepdims=True) + RMS_EPS)
            o_ref[rows, ls] = (o * g_ref[:, ls] * sr_ref[rows, ls]).astype(BF16)
        return carry

    lax.fori_loop(0, TS_GLA // group, chunk_group, 0)


def _gla(q, k, la, v, sr, norm_g, bsz, seq):
    t, width = q.shape
    per_b = seq // TS_GLA
    group = GLA_CHUNK * GLA_UNROLL
    idx = jnp.arange(group)
    same = (idx[:, None] // GLA_CHUNK) == (idx[None, :] // GLA_CHUNK)
    fwd = jnp.logical_and(same, idx[:, None] >= idx[None, :]).astype(BF16)
    rev = jnp.logical_and(same, idx[:, None] < idx[None, :]).astype(BF16)
    blk = pl.BlockSpec((TS_GLA, width), lambda b, i: (b * per_b + i, 0))
    tri_spec = pl.BlockSpec((group, group), lambda b, i: (0, 0))
    return pl.pallas_call(
        _gla_kernel,
        grid=(bsz, per_b),
        in_specs=[blk, blk, blk, blk, blk,
                  pl.BlockSpec((1, width), lambda b, i: (0, 0)),
                  tri_spec, tri_spec],
        out_specs=blk,
        out_shape=jax.ShapeDtypeStruct((t, width), BF16),
        scratch_shapes=[pltpu.VMEM((GLA_HEADS, GLA_DV, LANES), F32)],
        compiler_params=_cparams("arbitrary", "arbitrary"),
        name="gla",
    )(q, k, la, v, sr, norm_g.reshape(1, -1), fwd, rev)


def _pool_kernel(u_ref, halo_ref, w_ref, b_ref, s_ref, o_ref, buf):
    i = pl.program_id(1)
    buf[0:POOL_HALO, :] = jnp.where(i > 0, halo_ref[...], 0.0)
    buf[POOL_HALO:, :] = u_ref[...]
    pos = (i * TS_POOL + 1 + lax.broadcasted_iota(jnp.int32, (TS_POOL, 1), 0)).astype(F32)
    for gi, win in enumerate(POOL_WINDOWS):
        ls = slice(gi * LANES, (gi + 1) * LANES)
        tok = buf[POOL_HALO:, ls]
        tot = tok
        for back in range(1, win):
            tot = tot + buf[POOL_HALO - back:POOL_HALO - back + TS_POOL, ls]
        p = tot / jnp.minimum(pos, float(win)) - tok
        y = _dot(p.astype(BF16), w_ref[gi]) + b_ref[:, ls]
        o_ref[:, ls] = (y * s_ref[:, ls]).astype(BF16)


def _pool(u2d, pool_w, pool_b, pool_scale, bsz, seq):
    t = u2d.shape[0]
    per_b = seq // TS_POOL
    ratio = TS_POOL // POOL_HALO
    full = lambda shape: pl.BlockSpec(shape, lambda b, i: (0,) * len(shape))
    return pl.pallas_call(
        _pool_kernel,
        grid=(bsz, per_b),
        in_specs=[
            pl.BlockSpec((TS_POOL, D_POOL), lambda b, i: (b * per_b + i, 0)),
            pl.BlockSpec((POOL_HALO, D_POOL),
                         lambda b, i: (jnp.maximum((b * per_b + i) * ratio - 1, 0), 0)),
            full((len(POOL_WINDOWS), LANES, LANES)), full((1, D_POOL)), full((1, D_POOL)),
        ],
        out_specs=pl.BlockSpec((TS_POOL, D_POOL), lambda b, i: (b * per_b + i, 0)),
        out_shape=jax.ShapeDtypeStruct((t, D_POOL), BF16),
        scratch_shapes=[pltpu.VMEM((TS_POOL + POOL_HALO, D_POOL), F32)],
        compiler_params=_cparams("arbitrary", "arbitrary"),
        name="pool_mixer",
    )(u2d, u2d, pool_w.astype(BF16), pool_b.reshape(1, D_POOL), pool_scale.reshape(1, D_POOL))


def _pad_heads(w, heads, dim):
    lead = w.shape[:-1]
    w = w.reshape(lead + (heads, dim))
    w = jnp.pad(w, [(0, 0)] * len(lead) + [(0, 0), (0, LANES - dim)])
    return w.reshape(lead + (heads * LANES,))


def _odd_weights(w_in, gate_w, gate_b):
    dkt = GLA_HEADS * GLA_DK
    dvt = GLA_HEADS * GLA_DV
    o = [0, dkt, 2 * dkt, 2 * dkt + dvt, 2 * dkt + 2 * dvt, 2 * dkt + 2 * dvt + GLA_RANK]
    wq, wk, wv, wr, wa, wp = [w_in[:, o[j]:(o + [w_in.shape[1]])[j + 1]] for j in range(6)]
    w_cat = jnp.concatenate([
        _pad_heads(wq, GLA_HEADS, GLA_DK), _pad_heads(wk, GLA_HEADS, GLA_DK), wv, wr, wp,
        jnp.pad(wa, ((0, 0), (0, LANES - GLA_RANK)))], axis=1).astype(BF16)
    gw = jnp.pad(_pad_heads(gate_w, GLA_HEADS, GLA_DK), ((0, LANES - GLA_RANK), (0, 0))).astype(BF16)
    gb = _pad_heads(gate_b, GLA_HEADS, GLA_DK).reshape(1, -1)
    return w_cat, gw, gb


def kernel(x, c, ada_w, ada_b, norm_mix, norm_ffn, w_in_even, w_out_even, conv_w, conv_b, conv_norm_g, conv_norm_b, w_in_odd, w_out_odd, gla_gate_w, gla_gate_b, gla_norm_g, pool_w, pool_b, pool_scale, router_w, router_bias, moe_w_gate, moe_w_up, moe_w_down, final_norm):
    bsz, seq, d = x.shape
    t = bsz * seq
    depth = ada_w.shape[0]
    mod = _ada(c, ada_w, ada_b)
    rw_hi = router_w.T.astype(BF16)
    rw_lo = (router_w.T - rw_hi.astype(F32)).astype(BF16)
    rw_stack = jnp.concatenate([rw_hi, rw_lo], axis=0)
    rb =router_bias.reshape(N_EXPERTS, 1)
    fn = final_norm.reshape(1, d)
    x2d = x.reshape(t, d)
    for l in range(depth):
        sh1, sc1, g1, sh2, sc2, g2 = [mod[l, :, j * d:(j + 1) * d].reshape(bsz, 1, d)
                                      for j in range(6)]
        i = l // 2
        nm = norm_mix[l].reshape(1, d)
        if l % 2 == 0:
            y, q, k, v = _even_in(x2d, nm, sc1, sh1, w_in_even[i].astype(BF16), seq)
            a = _conv(y, conv_w[i], conv_b[i], conv_norm_g[i], conv_norm_b[i], bsz, seq)
            b = _sb_attention(q, k, v, bsz, seq)
            w_out = w_out_even[i]
        else:
            w_cat, gw, gb = _odd_weights(w_in_odd[i], gla_gate_w[i], gla_gate_b[i])
            q, k, v, sr, la, up = _odd_in(x2d, nm, sc1, sh1, w_cat, gw, gb, seq)
            a = _gla(q, k, la, v, sr, gla_norm_g[i], bsz, seq)
            b = _pool(up, pool_w[i], pool_b[i], pool_scale[i], bsz, seq)
            w_out = w_out_odd[i]
        x1, hp, pos, wt, counts = _out_router(x2d, a, b, w_out.astype(BF16), g1,
                                              norm_ffn[l].reshape(1, d), sc2, sh2,
                                              rw_stack, rb, seq)
        x2d = _moe(x1, hp, pos, wt, counts, l, moe_w_gate, moe_w_up, moe_w_down, g2, fn, seq,
                   final=(l == depth - 1))
    return x2d.reshape(bsz, seq, d)
```

```python
import functools

import jax
import jax.numpy as jnp
from jax import lax
from jax.experimental import pallas as pl
from jax.experimental.pallas import tpu as pltpu

F32 = jnp.float32
BF16 = jnp.bfloat16

D_MODEL = 1024
D_CONV = 512
CONV_WIDTH = 31
CONV_GROUP_SIZE = 64
SB_HEADS = 8
SB_HEAD_DIM = 64
GLA_HEADS = 4
GLA_DK = 64
GLA_DV = 128
GLA_RANK = 16
GLA_TAU = 16.0
GLA_CHUNK = 64
D_POOL = 512
POOL_WINDOWS = (2, 4, 8, 16)
N_EXPERTS = 16
N_EXPERT_GROUPS = 4
EXPERTS_PER_GROUP = 4
D_EXPERT = 512
RMS_EPS = 1e-6
LN_EPS = 1e-5

LANES = 128
SUBLANES = 8
VMEM_LIMIT = 56 * 1024 * 1024

TM = 512
TS_CONV = 256
CONV_HALO = 32
CONV_ROWS = 64
TS_GLA = 1024
GLA_UNROLL = 4
TS_POOL = 512
POOL_HALO = 16
TQ = 256
TK = 128
TM_MOE = 512
SB_DEAD = -120.0


def _cparams(*sem):
    return pltpu.CompilerParams(dimension_semantics=sem, vmem_limit_bytes=VMEM_LIMIT)


def _sigmoid(x):
    return 1.0 / (1.0 + jnp.exp(-x))


def _dot(a, b):
    return jnp.dot(a, b, preferred_element_type=F32)


def _dot_nt(a, b, **kw):
    return lax.dot_general(a, b, (((1,), (1,)), ((), ())), preferred_element_type=F32, **kw)


def _dot_tn(a, b):
    return lax.dot_general(a, b, (((0,), (0,)), ((), ())), preferred_element_type=F32)


def _split_dot(x, m):
    hi = x.astype(BF16)
    lo = (x - hi.astype(F32)).astype(BF16)
    return _dot(hi, m) + _dot(lo, m)


def _split_dot_left(m, x):
    hi = x.astype(BF16)
    lo = (x - hi.astype(F32)).astype(BF16)
    return _dot(m, hi) + _dot(m, lo)


def _modulated_rmsnorm(x, g, sc, sh):
    ms = jnp.mean(x * x, axis=-1, keepdims=True)
    return (x * lax.rsqrt(ms + RMS_EPS) * g) * (1.0 + sc) + sh


def _ada_kernel(c_ref, w_ref, b_ref, o_ref):
    c = c_ref[...]
    cond = c * _sigmoid(c)
    o_ref[0] = jnp.dot(cond, w_ref[0], preferred_element_type=F32,
                       precision=lax.Precision.HIGHEST) + b_ref[0]


def _ada(c, ada_w, ada_b):
    depth, d, n = ada_w.shape
    bsz = c.shape[0]
    tn = 1536
    return pl.pallas_call(
        _ada_kernel,
        grid=(depth, n // tn),
        in_specs=[
            pl.BlockSpec((bsz, d), lambda l, j: (0, 0)),
            pl.BlockSpec((1, d, tn), lambda l, j: (l, 0, j)),
            pl.BlockSpec((1, 1, tn), lambda l, j: (l, 0, j)),
        ],
        out_specs=pl.BlockSpec((1, bsz, tn), lambda l, j: (l, 0, j)),
        out_shape=jax.ShapeDtypeStruct((depth, bsz, n), F32),
        compiler_params=_cparams("arbitrary", "arbitrary"),
        name="ada_mod",
    )(c, ada_w, ada_b.reshape(depth, 1, n))


def _even_in_kernel(x_ref, g_ref, sc_ref, sh_ref, w_ref, y_ref, q_ref, k_ref, v_ref):
    h = _modulated_rmsnorm(x_ref[...], g_ref[...], sc_ref[0], sh_ref[0]).astype(BF16)
    val = _dot(h, w_ref[:, 0:512])
    gate = _dot(h, w_ref[:, 512:1024])
    y_ref[...] = val * _sigmoid(gate)
    q_ref[...] = (_dot(h, w_ref[:, 1024:1536]) * (SB_HEAD_DIM ** -0.5)).astype(BF16)
    k_ref[...] = _dot(h, w_ref[:, 1536:2048]).astype(BF16)
    v_ref[...] = _dot(h, w_ref[:, 2048:2560]).astype(BF16)


def _even_in(x2d, g, sc, sh, w_in, seq):
    t, d = x2d.shape
    n = w_in.shape[1]
    per_b = seq // TM
    vec = pl.BlockSpec((1, 1, d), lambda i: (i // per_b, 0, 0))
    out = lambda dt: jax.ShapeDtypeStruct((t, 512), dt)
    return pl.pallas_call(
        _even_in_kernel,
        grid=(t // TM,),
        in_specs=[
            pl.BlockSpec((TM, d), lambda i: (i, 0)),
            pl.BlockSpec((1, d), lambda i: (0, 0)),
            vec, vec,
            pl.BlockSpec((d, n), lambda i: (0, 0)),
        ],
        out_specs=[pl.BlockSpec((TM, 512), lambda i: (i, 0))] * 4,
        out_shape=[out(F32), out(BF16), out(BF16), out(BF16)],
        compiler_params=_cparams("arbitrary"),
        name="even_in",
    )(x2d, g, sc, sh, w_in)


def _conv_kernel(y_ref, halo_ref, w_ref, b_ref, ng_ref, nb_ref, avg_ref, o_ref, buf, sh, cv):
    i = pl.program_id(1)
    buf[0:CONV_HALO, :] = jnp.where(i > 0, halo_ref[...], 0.0)
    buf[CONV_HALO:, :] = y_ref[...]
    sh_rows = sh.shape[1]
    for r in range(1, SUBLANES):
        sh[r - 1] = buf[r:r + sh_rows, :]
    first = CONV_HALO - (CONV_WIDTH - 1)
    for rc in range(TS_CONV // CONV_ROWS):
        r0 = rc * CONV_ROWS
        for lc in range(D_CONV // LANES):
            ls = slice(lc * LANES, (lc + 1) * LANES)
            acc = jnp.zeros((CONV_ROWS, LANES), F32)
            for j in range(CONV_WIDTH):
                whole, r = divmod(first + j, SUBLANES)
                s = r0 + whole * SUBLANES
                src = buf[s:s + CONV_ROWS, ls] if r == 0 else sh[r - 1, s:s + CONV_ROWS, ls]
                acc = acc + w_ref[j:j + 1, ls] * src
            cv[r0:r0 + CONV_ROWS, ls] = acc + b_ref[:, ls]
    y = cv[...]
    avg = avg_ref[...]
    d = y - _split_dot(y, avg)
    var = _split_dot(d * d, avg)
    yn = d * lax.rsqrt(var + LN_EPS) * ng_ref[...] + nb_ref[...]
    o_ref[...] = (yn * _sigmoid(yn)).astype(BF16)


def _conv(y2d, conv_w, conv_b, norm_g, norm_b, bsz, seq):
    t = y2d.shape[0]
    per_b = seq // TS_CONV
    ratio = TS_CONV // CONV_HALO
    gid = jnp.arange(D_CONV) // CONV_GROUP_SIZE
    avg = jnp.where(gid[:, None] == gid[None, :], 1.0 / CONV_GROUP_SIZE, 0.0).astype(BF16)
    w_pad = jnp.pad(conv_w, ((0, 32 - CONV_WIDTH), (0, 0)))
    row = lambda a: a.reshape(1, D_CONV)
    full = lambda shape: pl.BlockSpec(shape, lambda b, i: (0, 0))
    return pl.pallas_call(
        _conv_kernel,
        grid=(bsz, per_b),
        in_specs=[
            pl.BlockSpec((TS_CONV, D_CONV), lambda b, i: (b * per_b + i, 0)),
            pl.BlockSpec((CONV_HALO, D_CONV),
                         lambda b, i: (jnp.maximum((b * per_b + i) * ratio - 1, 0), 0)),
            full((32, D_CONV)), full((1, D_CONV)), full((1, D_CONV)), full((1, D_CONV)),
            full((D_CONV, D_CONV)),
        ],
        out_specs=pl.BlockSpec((TS_CONV, D_CONV), lambda b, i: (b * per_b + i, 0)),
        out_shape=jax.ShapeDtypeStruct((t, D_CONV), BF16),
        scratch_shapes=[pltpu.VMEM((TS_CONV + CONV_HALO, D_CONV), F32),
                        pltpu.VMEM((SUBLANES - 1, TS_CONV + CONV_HALO - SUBLANES, D_CONV), F32),
                        pltpu.VMEM((TS_CONV, D_CONV), F32)],
        compiler_params=_cparams("arbitrary", "arbitrary"),
        name="conv_module",
    )(y2d, y2d, w_pad, row(conv_b), row(norm_g), row(norm_b), avg)


def _sb_kernel(q_ref, k_ref, v_ref, u_ref, o_ref, acc_ref, cr_ref):
    qi = pl.program_id(2)
    q = q_ref[...]
    lane = lax.broadcasted_iota(jnp.int32, (TQ, LANES), 1)
    row = lax.broadcasted_iota(jnp.int32, (TQ, TK), 0)
    col = lax.broadcasted_iota(jnp.int32, (TQ, TK), 1)
    qm = [jnp.where(lane < SB_HEAD_DIM, q, jnp.zeros_like(q)),
          jnp.where(lane >= SB_HEAD_DIM, q, jnp.zeros_like(q))]
    acc_ref[...] = jnp.zeros_like(acc_ref)
    cr_ref[...] = jnp.zeros_like(cr_ref)
    per_q = TQ // TK

    def key_blocks(first_kb, count, n_diagonal):
        for hd in range(2):
            cr = cr_ref[hd]
            acc = acc_ref[hd]
            for step in range(count):
                kb = first_kb - step
                ks = pl.multiple_of(kb * TK, TK)
                z = _dot_nt(qm[hd], k_ref[pl.ds(ks, TK), :])
                l1p = jnp.log(1.0 + jnp.exp(-jnp.abs(z)))
                log_keep = -jnp.maximum(z, 0.0) - l1p
                if step < n_diagonal:
                    strict = (col + (kb * TK - qi * TQ)) < row
                    log_keep = jnp.where(strict, log_keep, 0.0)
                later = _split_dot(log_keep, u_ref[...]) + cr
                a = jnp.exp(jnp.minimum(z, 0.0) - l1p + later)
                if step < n_diagonal:
                    a = jnp.where(strict, a, 0.0)
                acc = acc + _dot(a.astype(BF16), v_ref[pl.ds(ks, TK), :])
                cr = cr + jnp.sum(log_keep, axis=-1, keepdims=True)
            cr_ref[hd] = cr
            acc_ref[hd] = acc

    def alive():
        return (jnp.max(cr_ref[...]) > SB_DEAD).astype(jnp.int32)

    last_diag = qi * per_q + per_q - 1

    @pl.when(qi == 0)
    def _():
        key_blocks(last_diag, per_q, per_q)

    @pl.when(qi > 0)
    def _():
        key_blocks(last_diag, 2 * per_q, per_q)

    def cond(carry):
        kb, live = carry
        return jnp.logical_and(kb >= 0, live > 0)

    def body(carry):
        kb, _ = carry
        key_blocks(kb, 1, 0)
        return kb - 1, alive()

    lax.while_loop(cond, body, (last_diag - 2 * per_q, alive()))
    o_ref[...] = jnp.where(lane < SB_HEAD_DIM, acc_ref[0], acc_ref[1]).astype(BF16)


def _sb_attention(q, k, v, bsz, seq):
    t = q.shape[0]
    nq = seq // TQ
    idx = jnp.arange(TK)
    upper = (idx[:, None] > idx[None, :]).astype(BF16)
    kv = pl.BlockSpec((seq, LANES), lambda b, p, i: (b, p))
    return pl.pallas_call(
        _sb_kernel,
        grid=(bsz, SB_HEADS // 2, nq),
        in_specs=[
            pl.BlockSpec((TQ, LANES), lambda b, p, i: (b * nq + i, p)),
            kv, kv,
            pl.BlockSpec((TK, TK), lambda b, p, i: (0, 0)),
        ],
        out_specs=pl.BlockSpec((TQ, LANES), lambda b, p, i: (b * nq + i, p)),
        out_shape=jax.ShapeDtypeStruct((t, SB_HEADS * SB_HEAD_DIM), BF16),
        scratch_shapes=[pltpu.VMEM((2, TQ, LANES), F32), pltpu.VMEM((2, TQ, 1), F32)],
        compiler_params=_cparams("arbitrary", "arbitrary", "arbitrary"),
        name="stick_breaking",
    )(q, k, v, upper)


def _first_of(vals, target):
    idx = jnp.full(target.shape, len(vals) - 1, jnp.int32)
    for j in range(len(vals) - 2, -1, -1):
        idx = jnp.where(vals[j] == target, j, idx)
    return idx


def _top2(vals):
    m1 = functools.reduce(jnp.maximum, vals)
    i1 = _first_of(vals, m1)
    rest = [jnp.where(i1 == j, -jnp.inf, v) for j, v in enumerate(vals)]
    m2 = functools.reduce(jnp.maximum, rest)
    i2 = _first_of(rest, m2)
    return m1, m2, i1, i2


def _pick(idx, vals):
    out = vals[-1]
    for j in range(len(vals) - 2, -1, -1):
        out = jnp.where(idx == j, vals[j], out)
    return out


def _out_router_kernel(expert_rows, x_ref, a_ref, b_ref, w_ref, g1_ref, ng_ref, sc_ref, sh_ref,
                       rw_ref, rb_ref, tri_ref,
                       x1_ref, pos_ref, wt_ref, cnt_ref, sched_ref, xs_ref,
                       member, stage, pos_s, pos_sem, row_sems):
    step = pl.program_id(0)
    n_steps = pl.num_programs(0)
    slot = step % 2

    def wait_rows(s):
        for _ in range(2):
            pltpu.make_async_copy(stage.at[s], xs_ref.at[pl.ds(0, TM * SUBLANES), :],
                                  row_sems.at[s]).wait()

    half = a_ref.shape[1]
    mix = _dot(a_ref[...], w_ref[0:half, :]) + _dot(b_ref[...], w_ref[half:, :])
    x1 = x_ref[...] + g1_ref[0] * mix
    x1_ref[...] = x1
    h = _modulated_rmsnorm(x1, ng_ref[...], sc_ref[0], sh_ref[0])
    hb = h.astype(BF16)

    @pl.when(step >= 2)
    def _():
        wait_rows(slot)

    _to_slabs(stage.at[slot], h)

    hl = (h - hb.astype(F32)).astype(BF16)
    both = _dot_nt(rw_ref[...], hb)
    logits = (both[0:N_EXPERTS, :] + both[N_EXPERTS:, :]
              + _dot_nt(rw_ref[0:N_EXPERTS, :], hl))
    ex = jnp.exp(logits - jnp.max(logits, axis=0, keepdims=True))
    scores = ex / jnp.sum(ex, axis=0, keepdims=True)
    sel = scores + rb_ref[...]
    srow = [scores[e:e + 1, :] for e in range(N_EXPERTS)]
    lrow = [sel[e:e + 1, :] for e in range(N_EXPERTS)]
    tops = [_top2(lrow[g * EXPERTS_PER_GROUP:(g + 1) * EXPERTS_PER_GROUP])
            for g in range(N_EXPERT_GROUPS)]
    gscore = [tp[0] + tp[1] for tp in tops]
    gidx = _first_of(gscore, functools.reduce(jnp.maximum, gscore))
    e1 = gidx * EXPERTS_PER_GROUP + _pick(gidx, [tp[2] for tp in tops])
    e2 = gidx * EXPERTS_PER_GROUP + _pick(gidx, [tp[3] for tp in tops])
    w1 = _pick(e1, srow)
    w2 = _pick(e2, srow)
    tot = w1 + w2
    wt_ref[0:1, :] = w1 / tot
    wt_ref[1:2, :] = w2 / tot

    @pl.when(pl.program_id(0) == 0)
    def _():
        cnt_ref[...] = jnp.zeros_like(cnt_ref)

    for e in range(N_EXPERTS):
        member[e:e + 1, :] = jnp.where(jnp.logical_or(e1 == e, e2 == e), 1.0, 0.0)
    m = member[...]
    incl = _dot(m.astype(BF16), tri_ref[...])
    excl = incl - m + cnt_ref[:, 0:1]
    cnt_ref[...] = cnt_ref[...] + incl[:, TM - 1:TM]
    erow = [excl[e:e + 1, :] for e in range(N_EXPERTS)]
    pos_ref[0:1, :] = (e1 * expert_rows + _pick(e1, erow).astype(jnp.int32)) * SUBLANES
    pos_ref[1:2, :] = (e2 * expert_rows + _pick(e2, erow).astype(jnp.int32)) * SUBLANES

    to_smem = pltpu.make_async_copy(pos_ref, pos_s, pos_sem)
    to_smem.start()
    to_smem.wait()

    def issue(r, carry):
        for pick in range(2):
            pltpu.make_async_copy(_slab(stage.at[slot], r * SUBLANES),
                                  _slab(xs_ref, pos_s[pick, r]),
                                  row_sems.at[slot]).start(priority=pick)
        return carry

    lax.fori_loop(0, TM, issue, 0, unroll=8)

    @pl.when(step == n_steps - 1)
    def _():
        wait_rows(slot)

        @pl.when(n_steps > 1)
        def _():
            wait_rows(1 - slot)

        _tile_schedule(cnt_ref[...], expert_rows, sched_ref)


def _tile_schedule(cnt, n_tok, sched_ref):
    blocks_per_expert = n_tok // TM_MOE
    tiles = jnp.floor((cnt + (TM_MOE - 1)) * (1.0 / TM_MOE))
    e_row = lax.broadcasted_iota(jnp.int32, (N_EXPERTS, N_EXPERTS), 0)
    e_col = lax.broadcasted_iota(jnp.int32, (N_EXPERTS, N_EXPERTS), 1)
    lower = jnp.where(e_col <= e_row, 1.0, 0.0).astype(BF16)
    ends = _dot(lower, tiles.astype(BF16))
    starts = ends - tiles
    e_id = lax.broadcasted_iota(jnp.int32, cnt.shape, 0).astype(F32)
    j = lax.broadcasted_iota(jnp.int32, cnt.shape, 1).astype(F32)
    total = ends[N_EXPERTS - 1:N_EXPERTS, :]
    e_j = jnp.minimum(jnp.sum(jnp.where(j >= ends, 1.0, 0.0), axis=0, keepdims=True),
                      float(N_EXPERTS - 1))
    mine = e_id == e_j
    start_j = jnp.sum(jnp.where(mine, starts, 0.0), axis=0, keepdims=True)
    cnt_j = jnp.sum(jnp.where(mine, cnt, 0.0), axis=0, keepdims=True)
    last_e = jnp.max(jnp.where(tiles > 0.0, e_id, 0.0), axis=0, keepdims=True)
    j_row = j[0:1, :]
    local = j_row - start_j
    used = j_row < total
    sched_ref[...] = jnp.zeros_like(sched_ref)
    sched_ref[0:1, :] = jnp.where(used, e_j, last_e).astype(jnp.int32)
    sched_ref[1:2, :] = jnp.where(used, e_j * blocks_per_expert + local,
                                  float(N_EXPERTS * blocks_per_expert)).astype(jnp.int32)
    sched_ref[2:3, :] = jnp.where(used, jnp.clip(cnt_j - local * TM_MOE, 0.0, float(TM_MOE)),
                                  0.0).astype(jnp.int32)


def _out_router(x2d, a, b, w_out, g1, ng, sc, sh, rw_stack, rb, seq):
    t, d = x2d.shape
    per_b = seq // TM
    vec = pl.BlockSpec((1, 1, d), lambda i: (i // per_b, 0, 0))
    full = lambda shape: pl.BlockSpec(shape, lambda i: (0, 0))
    half = a.shape[1]
    idx = jnp.arange(TM)
    tri = (idx[:, None] <= idx[None, :]).astype(BF16)
    assert d == SUBLANES * LANES and 2 * t // TM_MOE + N_EXPERTS <= LANES
    rows = N_EXPERTS * t + TM_MOE
    return pl.pallas_call(
        functools.partial(_out_router_kernel, t),
        grid=(t // TM,),
        in_specs=[
            pl.BlockSpec((TM, d), lambda i: (i, 0)),
            pl.BlockSpec((TM, half), lambda i: (i, 0)),
            pl.BlockSpec((TM, half), lambda i: (i, 0)),
            full((2 * half, d)),
            vec, full((1, d)), vec, vec,
            full((2 * N_EXPERTS, d)), full((N_EXPERTS, 1)), full((TM, TM)),
        ],
        out_specs=[
            pl.BlockSpec((TM, d), lambda i: (i, 0)),
            pl.BlockSpec((2, TM), lambda i: (0, i)),
            pl.BlockSpec((2, TM), lambda i: (0, i)),
            pl.BlockSpec((N_EXPERTS, LANES), lambda i: (0, 0)),
            pl.BlockSpec((SUBLANES, LANES), lambda i: (0, 0)),
            pl.BlockSpec(memory_space=pl.ANY),
        ],
        out_shape=[
            jax.ShapeDtypeStruct((t, d), F32),
            jax.ShapeDtypeStruct((2, t), jnp.int32),
            jax.ShapeDtypeStruct((2, t), F32),
            jax.ShapeDtypeStruct((N_EXPERTS, LANES), F32),
            jax.ShapeDtypeStruct((SUBLANES, LANES), jnp.int32),
            jax.ShapeDtypeStruct((rows * SUBLANES, LANES), F32),
        ],
        scratch_shapes=[pltpu.VMEM((N_EXPERTS, TM), F32),
                        pltpu.VMEM((2, TM * SUBLANES, LANES), F32),
                        pltpu.SMEM((2, TM), jnp.int32),
                        pltpu.SemaphoreType.DMA,
                        pltpu.SemaphoreType.DMA((2,))],
        compiler_params=_cparams("arbitrary"),
        name="out_router",
    )(x2d, a, b, w_out, g1, ng, sc, sh, rw_stack, rb, tri)


def _to_slabs(ref, x):
    m = x.shape[0]
    for j in range(SUBLANES):
        ref[pl.ds(j, m, stride=SUBLANES), :] = x[:, j * LANES:(j + 1) * LANES]


def _from_slabs(ref):
    m = ref.shape[0] // SUBLANES
    return jnp.concatenate([ref[pl.ds(j, m, stride=SUBLANES), :] for j in range(SUBLANES)],
                           axis=-1)


def _slab(ref, offset):
    return ref.at[pl.ds(pl.multiple_of(offset, SUBLANES), SUBLANES), :]


def _expert_kernel(te_ref, tb_ref, tv_ref, xs_ref, wg_ref, wu_ref, wd_ref, ys_ref,
                   wg_bf, wu_bf, wd_bf):
    i = pl.program_id(0)
    expert = te_ref[i]
    valid = tv_ref[i]

    @pl.when(jnp.logical_or(i == 0, expert != te_ref[jnp.maximum(i - 1, 0)]))
    def _():
        wg_bf[...] = wg_ref[0].astype(BF16)
        wu_bf[...] = wu_ref[0].astype(BF16)
        wd_bf[...] = wd_ref[0].astype(BF16)

    @pl.when(valid > 0)
    def _():
        x = _from_slabs(xs_ref)
        row = lax.broadcasted_iota(jnp.int32, x.shape, 0)
        h = jnp.where(row < valid, x, 0.0).astype(BF16)
        gate = _dot(h, wg_bf[...])
        up = _dot(h, wu_bf[...])
        act = (gate * _sigmoid(gate) * up).astype(BF16)
        _to_slabs(ys_ref, _dot(act, wd_bf[...]))

    @pl.when(valid == 0)
    def _():
        ys_ref[...] = jnp.zeros_like(ys_ref)


def _experts(tile_expert, tile_block, tile_valid, xs, layer, wg, wu, wd):
    d = SUBLANES * LANES
    n_tiles = tile_expert.shape[0]
    wspec = lambda shape: pl.BlockSpec((None,) + shape, lambda i, te, tb, tv: (layer, te[i], 0, 0))
    rowspec = pl.BlockSpec((TM_MOE * SUBLANES, LANES), lambda i, te, tb, tv: (tb[i], 0))
    return pl.pallas_call(
        _expert_kernel,
        grid_spec=pltpu.PrefetchScalarGridSpec(
            num_scalar_prefetch=3,
            grid=(n_tiles,),
            in_specs=[
                rowspec,
                wspec((1, d, D_EXPERT)), wspec((1, d, D_EXPERT)), wspec((1, D_EXPERT, d)),
            ],
            out_specs=rowspec,
            scratch_shapes=[pltpu.VMEM((d, D_EXPERT), BF16), pltpu.VMEM((d, D_EXPERT), BF16),
                            pltpu.VMEM((D_EXPERT, d), BF16)],
        ),
        out_shape=jax.ShapeDtypeStruct(xs.shape, F32),
        compiler_params=_cparams("arbitrary"),
        name="moe_experts",
    )(tile_expert, tile_block, tile_valid, xs, wg, wu, wd)


def _combine_kernel(final, pos_ref, x1_ref, wt_ref, g2_ref, fn_ref, ys_ref, o_ref, buf, sems):
    n_tok = pos_ref.shape[0] // 2
    i = pl.program_id(0)
    n = pl.num_programs(0)

    def gather(tile, slot):
        base = tile * TM

        def issue(r, carry):
            for pick in range(2):
                pltpu.make_async_copy(_slab(ys_ref, pos_ref[pick * n_tok + base + r]),
                                      _slab(buf.at[slot, pick], r * SUBLANES),
                                      sems.at[slot]).start(priority=pick)
            return carry

        lax.fori_loop(0, TM, issue, 0, unroll=8)

    @pl.when(i == 0)
    def _():
        gather(0, 0)

    @pl.when(i + 1 < n)
    def _():
        gather(i + 1, (i + 1) % 2)

    slot = i % 2
    for pick in range(2):
        pltpu.make_async_copy(ys_ref.at[pl.ds(0, TM * SUBLANES), :], buf.at[slot, pick],
                              sems.at[slot]).wait()
    y = (wt_ref[:, 0:1] * _from_slabs(buf.at[slot, 0])
         + wt_ref[:, 1:2] * _from_slabs(buf.at[slot, 1]))
    x2 = x1_ref[...] + g2_ref[0] * y
    if final:
        ms = jnp.mean(x2 * x2, axis=-1, keepdims=True)
        x2 = x2 * lax.rsqrt(ms + RMS_EPS) * fn_ref[...]
    o_ref[...] = x2


def _combine(pos_flat, x1, wt, g2, fn, ys, seq, final):
    t, d = x1.shape
    per_b = seq // TM
    return pl.pallas_call(
        functools.partial(_combine_kernel, final),
        grid_spec=pltpu.PrefetchScalarGridSpec(
            num_scalar_prefetch=1,
            grid=(t // TM,),
            in_specs=[
                pl.BlockSpec((TM, d), lambda i, pos: (i, 0)),
                pl.BlockSpec((TM, 2), lambda i, pos: (i, 0)),
                pl.BlockSpec((1, 1, d), lambda i, pos: (i // per_b, 0, 0)),
                pl.BlockSpec((1, d), lambda i, pos: (0, 0)),
                pl.BlockSpec(memory_space=pl.ANY),
            ],
            out_specs=pl.BlockSpec((TM, d), lambda i, pos: (i, 0)),
            scratch_shapes=[pltpu.VMEM((2, 2, TM * SUBLANES, LANES), F32),
                            pltpu.SemaphoreType.DMA((2,))],
        ),
        out_shape=jax.ShapeDtypeStruct((t, d), F32),
        compiler_params=_cparams("arbitrary"),
        name="moe_combine_final" if final else "moe_combine",
    )(pos_flat, x1, wt, g2, fn, ys)


def _moe(x1, xs, pos, wt, sched, layer, wg, wu, wd, g2, fn, seq, final):
    n_tiles = 2 * x1.shape[0] // TM_MOE + N_EXPERTS
    ys = _experts(sched[0, :n_tiles], sched[1, :n_tiles], sched[2, :n_tiles], xs, layer,
                  wg, wu, wd)
    return _combine(pos.reshape(-1), x1, wt.T, g2, fn, ys, seq, final)


def _log_sigmoid(x):
    return jnp.minimum(x, 0.0) - jnp.log1p(jnp.exp(-jnp.abs(x)))


def _odd_in_kernel(x_ref, g_ref, sc_ref, sh_ref, w_ref, gw_ref, gb_ref,
                   q_ref, k_ref, v_ref, sr_ref, la_ref, up_ref):
    h = _modulated_rmsnorm(x_ref[...], g_ref[...], sc_ref[0], sh_ref[0]).astype(BF16)
    q_ref[...] = _dot(h, w_ref[:, 0:512]) * (GLA_DK ** -0.5)
    k_ref[...] = _dot(h, w_ref[:, 512:1024])
    v_ref[...] = _dot(h, w_ref[:, 1024:1536]).astype(BF16)
    r = _dot(h, w_ref[:, 1536:2048])
    sr_ref[...] = r * _sigmoid(r)
    up_ref[...] = _dot(h, w_ref[:, 2048:2560])
    a_low = _dot(h, w_ref[:, 2560:2688]).astype(BF16)
    la_ref[...] = _log_sigmoid(_dot(a_low, gw_ref[...]) + gb_ref[...]) * (1.0 / GLA_TAU)


def _odd_in(x2d, g, sc, sh, w_cat, gate_w, gate_b, seq):
    t, d = x2d.shape
    n = w_cat.shape[1]
    per_b = seq // TM
    vec = pl.BlockSpec((1, 1, d), lambda i: (i // per_b, 0, 0))
    full = lambda shape: pl.BlockSpec(shape, lambda i: (0, 0))
    out = lambda dt: jax.ShapeDtypeStruct((t, 512), dt)
    return pl.pallas_call(
        _odd_in_kernel,
        grid=(t // TM,),
        in_specs=[
            pl.BlockSpec((TM, d), lambda i: (i, 0)),
            full((1, d)), vec, vec, full((d, n)),
            full((LANES, 512)), full((1, 512)),
        ],
        out_specs=[pl.BlockSpec((TM, 512), lambda i: (i, 0))] * 6,
        out_shape=[out(F32), out(F32), out(BF16), out(F32), out(F32), out(F32)],
        compiler_params=_cparams("arbitrary"),
        name="odd_in",
    )(x2d, g, sc, sh, w_cat, gate_w, gate_b)


def _gla_kernel(q_ref, k_ref, la_ref, v_ref, sr_ref, g_ref, fwd_ref, rev_ref, o_ref, st_ref):
    group = GLA_CHUNK * GLA_UNROLL
    row = lax.broadcasted_iota(jnp.int32, (group, group), 0)
    col = lax.broadcasted_iota(jnp.int32, (group, group), 1)
    shift = GLA_CHUNK.bit_length() - 1
    keep = jnp.logical_and(jnp.right_shift(row, shift) == jnp.right_shift(col, shift), col <= row)

    @pl.when(pl.program_id(1) == 0)
    def _():
        st_ref[...] = jnp.zeros_like(st_ref)

    def chunk_group(n, carry):
        rows = pl.ds(pl.multiple_of(n * group, group), group)
        la = la_ref[rows, :]
        b_all = _split_dot_left(fwd_ref[...], la)
        r_all = _split_dot_left(rev_ref[...], la)
        for hd in range(GLA_HEADS):
            ls = slice(hd * LANES, (hd + 1) * LANES)
            b = b_all[:, ls]
            k = k_ref[rows, ls]
            v = v_ref[rows, ls]
            q_in = (q_ref[rows, ls] * jnp.exp(b)).astype(BF16)
            k_in = (k * jnp.exp(-b)).astype(BF16)
            k_dec = (k * jnp.exp(r_all[:, ls])).astype(BF16)
            scores = jnp.where(keep, _dot_nt(q_in, k_in), 0.0).astype(BF16)
            o_intra = _dot(scores, v)
            state_t = st_ref[hd]
            outs = []
            for c in range(GLA_UNROLL):
                cs = slice(c * GLA_CHUNK, (c + 1) * GLA_CHUNK)
                outs.append(o_intra[cs, :] + _dot_nt(q_in[cs, :], state_t.astype(BF16)))
                b_last = b[(c + 1) * GLA_CHUNK - 1:(c + 1) * GLA_CHUNK, :]
                state_t = state_t * jnp.exp(b_last) + _dot_tn(v[cs, :], k_dec[cs, :])
            st_ref[hd] = state_t
            o = jnp.concatenate(outs, axis=0)
            o = o * lax.rsqrt(jnp.mean(o * o, axis=-1, keepdims=True) + RMS_EPS)
            o_ref[rows, ls] = (o * g_ref[:, ls] * sr_ref[rows, ls]).astype(BF16)
        return carry

    lax.fori_loop(0, TS_GLA // group, chunk_group, 0)


def _gla(q, k, la, v, sr, norm_g, bsz, seq):
    t, width = q.shape
    per_b = seq // TS_GLA
    group = GLA_CHUNK * GLA_UNROLL
    idx = jnp.arange(group)
    same = (idx[:, None] // GLA_CHUNK) == (idx[None, :] // GLA_CHUNK)
    fwd = jnp.logical_and(same, idx[:, None] >= idx[None, :]).astype(BF16)
    rev = jnp.logical_and(same, idx[:, None] < idx[None, :]).astype(BF16)
    blk = pl.BlockSpec((TS_GLA, width), lambda b, i: (b * per_b + i, 0))
    tri_spec = pl.BlockSpec((group, group), lambda b, i: (0, 0))
    return pl.pallas_call(
        _gla_kernel,
        grid=(bsz, per_b),
        in_specs=[blk, blk, blk, blk, blk,
                  pl.BlockSpec((1, width), lambda b, i: (0, 0)),
                  tri_spec, tri_spec],
        out_specs=blk,
        out_shape=jax.ShapeDtypeStruct((t, width), BF16),
        scratch_shapes=[pltpu.VMEM((GLA_HEADS, GLA_DV, LANES), F32)],
        compiler_params=_cparams("arbitrary", "arbitrary"),
        name="gla",
    )(q, k, la, v, sr, norm_g.reshape(1, -1), fwd, rev)


def _pool_kernel(u_ref, halo_ref, w_ref, b_ref, s_ref, o_ref, buf):
    i = pl.program_id(1)
    buf[0:POOL_HALO, :] = jnp.where(i > 0, halo_ref[...], 0.0)
    buf[POOL_HALO:, :] = u_ref[...]
    pos = (i * TS_POOL + 1 + lax.broadcasted_iota(jnp.int32, (TS_POOL, 1), 0)).astype(F32)
    for gi, win in enumerate(POOL_WINDOWS):
        ls = slice(gi * LANES, (gi + 1) * LANES)
        tok = buf[POOL_HALO:, ls]
        tot = tok
        for back in range(1, win):
            tot = tot + buf[POOL_HALO - back:POOL_HALO - back + TS_POOL, ls]
        p = tot / jnp.minimum(pos, float(win)) - tok
        y = _dot(p.astype(BF16), w_ref[gi]) + b_ref[:, ls]
        o_ref[:, ls] = (y * s_ref[:, ls]).astype(BF16)


def _pool(u2d, pool_w, pool_b, pool_scale, bsz, seq):
    t = u2d.shape[0]
    per_b = seq // TS_POOL
    ratio = TS_POOL // POOL_HALO
    full = lambda shape: pl.BlockSpec(shape, lambda b, i: (0,) * len(shape))
    return pl.pallas_call(
        _pool_kernel,
        grid=(bsz, per_b),
        in_specs=[
            pl.BlockSpec((TS_POOL, D_POOL), lambda b, i: (b * per_b + i, 0)),
            pl.BlockSpec((POOL_HALO, D_POOL),
                         lambda b, i: (jnp.maximum((b * per_b + i) * ratio - 1, 0), 0)),
            full((len(POOL_WINDOWS), LANES, LANES)), full((1, D_POOL)), full((1, D_POOL)),
        ],
        out_specs=pl.BlockSpec((TS_POOL, D_POOL), lambda b, i: (b * per_b + i, 0)),
        out_shape=jax.ShapeDtypeStruct((t, D_POOL), BF16),
        scratch_shapes=[pltpu.VMEM((TS_POOL + POOL_HALO, D_POOL), F32)],
        compiler_params=_cparams("arbitrary", "arbitrary"),
        name="pool_mixer",
    )(u2d, u2d, pool_w.astype(BF16), pool_b.reshape(1, D_POOL), pool_scale.reshape(1, D_POOL))


def _pad_heads(w, heads, dim):
    lead = w.shape[:-1]
    w = w.reshape(lead + (heads, dim))
    w = jnp.pad(w, [(0, 0)] * len(lead) + [(0, 0), (0, LANES - dim)])
    return w.reshape(lead + (heads * LANES,))


def _odd_weights(w_in, gate_w, gate_b):
    dkt = GLA_HEADS * GLA_DK
    dvt = GLA_HEADS * GLA_DV
    o = [0, dkt, 2 * dkt, 2 * dkt + dvt, 2 * dkt + 2 * dvt, 2 * dkt + 2 * dvt + GLA_RANK]
    wq, wk, wv, wr, wa, wp = [w_in[:, o[j]:(o + [w_in.shape[1]])[j + 1]] for j in range(6)]
    w_cat = jnp.concatenate([
        _pad_heads(wq, GLA_HEADS, GLA_DK), _pad_heads(wk, GLA_HEADS, GLA_DK), wv, wr, wp,
        jnp.pad(wa, ((0, 0), (0, LANES - GLA_RANK)))], axis=1).astype(BF16)
    gw = jnp.pad(_pad_heads(gate_w, GLA_HEADS, GLA_DK), ((0, LANES - GLA_RANK), (0, 0))).astype(BF16)
    gb = _pad_heads(gate_b, GLA_HEADS, GLA_DK).reshape(1, -1)
    return w_cat, gw, gb


def kernel(x, c, ada_w, ada_b, norm_mix, norm_ffn, w_in_even, w_out_even, conv_w, conv_b, conv_norm_g, conv_norm_b, w_in_odd, w_out_odd, gla_gate_w, gla_gate_b, gla_norm_g, pool_w, pool_b, pool_scale, router_w, router_bias, moe_w_gate, moe_w_up, moe_w_down, final_norm):
    bsz, seq, d = x.shape
    t = bsz * seq
    depth = ada_w.shape[0]
    mod = _ada(c, ada_w, ada_b)
    rw_hi = router_w.T.astype(BF16)
    rw_lo = (router_w.T - rw_hi.astype(F32)).astype(BF16)
    rw_stack = jnp.concatenate([rw_hi, rw_lo], axis=0)
    rb =router_bias.reshape(N_EXPERTS, 1)
    fn = final_norm.reshape(1, d)
    x2d = x.reshape(t, d)
    for l in range(depth):
        sh1, sc1, g1, sh2, sc2, g2 = [mod[l, :, j * d:(j + 1) * d].reshape(bsz, 1, d)
                                      for j in range(6)]
        i = l // 2
        nm = norm_mix[l].reshape(1, d)
        if l % 2 == 0:
            y, q, k, v = _even_in(x2d, nm, sc1, sh1, w_in_even[i].astype(BF16), seq)
            a = _conv(y, conv_w[i], conv_b[i], conv_norm_g[i], conv_norm_b[i], bsz, seq)
            b = _sb_attention(q, k, v, bsz, seq)
            w_out = w_out_even[i]
        else:
            w_cat, gw, gb = _odd_weights(w_in_odd[i], gla_gate_w[i], gla_gate_b[i])
            q, k, v, sr, la, up = _odd_in(x2d, nm, sc1, sh1, w_cat, gw, gb, seq)
            a = _gla(q, k, la, v, sr, gla_norm_g[i], bsz, seq)
            b = _pool(up, pool_w[i], pool_b[i], pool_scale[i], bsz, seq)
            w_out = w_out_odd[i]
        x1, pos, wt, _, sched, xs = _out_router(x2d, a, b, w_out.astype(BF16), g1,
                                                norm_ffn[l].reshape(1, d), sc2, sh2,
                                                rw_stack, rb, seq)
        x2d = _moe(x1, xs, pos, wt, sched, l, moe_w_gate, moe_w_up, moe_w_down, g2, fn, seq,
                   final=(l == depth - 1))
    return x2d.reshape(bsz, seq, d)
```

```python
import functools

import jax
import jax.numpy as jnp
from jax import lax
from jax.experimental import pallas as pl
from jax.experimental.pallas import tpu as pltpu

F32 = jnp.float32
BF16 = jnp.bfloat16

D_MODEL = 1024
D_CONV = 512
CONV_WIDTH = 31
CONV_GROUP_SIZE = 64
SB_HEADS = 8
SB_HEAD_DIM = 64
GLA_HEADS = 4
GLA_DK = 64
GLA_DV = 128
GLA_RANK = 16
GLA_TAU = 16.0
GLA_CHUNK = 64
D_POOL = 512
POOL_WINDOWS = (2, 4, 8, 16)
N_EXPERTS = 16
N_EXPERT_GROUPS = 4
EXPERTS_PER_GROUP = 4
D_EXPERT = 512
RMS_EPS = 1e-6
LN_EPS = 1e-5

LANES = 128
SUBLANES = 8
VMEM_LIMIT = 56 * 1024 * 1024

TM = 512
TS_CONV = 256
CONV_HALO = 32
CONV_ROWS = 64
TS_GLA = 1024
GLA_UNROLL = 4
TS_POOL = 512
POOL_HALO = 16
TQ = 256
TK = 128
SB_PAIRS_PER_STEP = 2
TM_MOE = 512
SB_DEAD = -120.0


def _cparams(*sem):
    return pltpu.CompilerParams(dimension_semantics=sem, vmem_limit_bytes=VMEM_LIMIT)


def _sigmoid(x):
    return 1.0 / (1.0 + jnp.exp(-x))


def _dot(a, b):
    return jnp.dot(a, b, preferred_element_type=F32)


def _dot_nt(a, b, **kw):
    return lax.dot_general(a, b, (((1,), (1,)), ((), ())), preferred_element_type=F32, **kw)


def _dot_tn(a, b):
    return lax.dot_general(a, b, (((0,), (0,)), ((), ())), preferred_element_type=F32)


def _split_dot(x, m):
    hi = x.astype(BF16)
    lo = (x - hi.astype(F32)).astype(BF16)
    return _dot(hi, m) + _dot(lo, m)


def _split_dot_left(m, x):
    hi = x.astype(BF16)
    lo = (x - hi.astype(F32)).astype(BF16)
    return _dot(m, hi) + _dot(m, lo)


def _modulated_rmsnorm(x, g, sc, sh):
    ms = jnp.mean(x * x, axis=-1, keepdims=True)
    return (x * lax.rsqrt(ms + RMS_EPS) * g) * (1.0 + sc) + sh


def _ada_kernel(c_ref, w_ref, b_ref, o_ref):
    c = c_ref[...]
    cond = c * _sigmoid(c)
    o_ref[0] = jnp.dot(cond, w_ref[0], preferred_element_type=F32,
                       precision=lax.Precision.HIGHEST) + b_ref[0]


def _ada(c, ada_w, ada_b):
    depth, d, n = ada_w.shape
    bsz = c.shape[0]
    tn = 1536
    return pl.pallas_call(
        _ada_kernel,
        grid=(depth, n // tn),
        in_specs=[
            pl.BlockSpec((bsz, d), lambda l, j: (0, 0)),
            pl.BlockSpec((1, d, tn), lambda l, j: (l, 0, j)),
            pl.BlockSpec((1, 1, tn), lambda l, j: (l, 0, j)),
        ],
        out_specs=pl.BlockSpec((1, bsz, tn), lambda l, j: (l, 0, j)),
        out_shape=jax.ShapeDtypeStruct((depth, bsz, n), F32),
        compiler_params=_cparams("arbitrary", "arbitrary"),
        name="ada_mod",
    )(c, ada_w, ada_b.reshape(depth, 1, n))


def _even_in_kernel(x_ref, g_ref, sc_ref, sh_ref, w_ref, y_ref, q_ref, k_ref, v_ref):
    h = _modulated_rmsnorm(x_ref[...], g_ref[...], sc_ref[0], sh_ref[0]).astype(BF16)
    val = _dot(h, w_ref[:, 0:512])
    gate = _dot(h, w_ref[:, 512:1024])
    y_ref[...] = val * _sigmoid(gate)
    q_ref[...] = (_dot(h, w_ref[:, 1024:1536]) * (SB_HEAD_DIM ** -0.5)).astype(BF16)
    k_ref[...] = _dot(h, w_ref[:, 1536:2048]).astype(BF16)
    v_ref[...] = _dot(h, w_ref[:, 2048:2560]).astype(BF16)


def _even_in(x2d, g, sc, sh, w_in, seq):
    t, d = x2d.shape
    n = w_in.shape[1]
    per_b = seq // TM
    vec = pl.BlockSpec((1, 1, d), lambda i: (i // per_b, 0, 0))
    out = lambda dt: jax.ShapeDtypeStruct((t, 512), dt)
    return pl.pallas_call(
        _even_in_kernel,
        grid=(t // TM,),
        in_specs=[
            pl.BlockSpec((TM, d), lambda i: (i, 0)),
            pl.BlockSpec((1, d), lambda i: (0, 0)),
            vec, vec,
            pl.BlockSpec((d, n), lambda i: (0, 0)),
        ],
        out_specs=[pl.BlockSpec((TM, 512), lambda i: (i, 0))] * 4,
        out_shape=[out(F32), out(BF16), out(BF16), out(BF16)],
        compiler_params=_cparams("arbitrary"),
        name="even_in",
    )(x2d, g, sc, sh, w_in)


def _conv_kernel(y_ref, halo_ref, w_ref, b_ref, ng_ref, nb_ref, avg_ref, o_ref, buf, sh, cv):
    i = pl.program_id(1)
    buf[0:CONV_HALO, :] = jnp.where(i > 0, halo_ref[...], 0.0)
    buf[CONV_HALO:, :] = y_ref[...]
    sh_rows = sh.shape[1]
    for r in range(1, SUBLANES):
        sh[r - 1] = buf[r:r + sh_rows, :]
    first = CONV_HALO - (CONV_WIDTH - 1)
    for rc in range(TS_CONV // CONV_ROWS):
        r0 = rc * CONV_ROWS
        for lc in range(D_CONV // LANES):
            ls = slice(lc * LANES, (lc + 1) * LANES)
            acc = jnp.zeros((CONV_ROWS, LANES), F32)
            for j in range(CONV_WIDTH):
                whole, r = divmod(first + j, SUBLANES)
                s = r0 + whole * SUBLANES
                src = buf[s:s + CONV_ROWS, ls] if r == 0 else sh[r - 1, s:s + CONV_ROWS, ls]
                acc = acc + w_ref[j:j + 1, ls] * src
            cv[r0:r0 + CONV_ROWS, ls] = acc + b_ref[:, ls]
    y = cv[...]
    avg = avg_ref[...]
    d = y - _split_dot(y, avg)
    var = _split_dot(d * d, avg)
    yn = d * lax.rsqrt(var + LN_EPS) * ng_ref[...] + nb_ref[...]
    o_ref[...] = (yn * _sigmoid(yn)).astype(BF16)


def _conv(y2d, conv_w, conv_b, norm_g, norm_b, bsz, seq):
    t = y2d.shape[0]
    per_b = seq // TS_CONV
    ratio = TS_CONV // CONV_HALO
    gid = jnp.arange(D_CONV) // CONV_GROUP_SIZE
    avg = jnp.where(gid[:, None] == gid[None, :], 1.0 / CONV_GROUP_SIZE, 0.0).astype(BF16)
    w_pad = jnp.pad(conv_w, ((0, 32 - CONV_WIDTH), (0, 0)))
    row = lambda a: a.reshape(1, D_CONV)
    full = lambda shape: pl.BlockSpec(shape, lambda b, i: (0, 0))
    return pl.pallas_call(
        _conv_kernel,
        grid=(bsz, per_b),
        in_specs=[
            pl.BlockSpec((TS_CONV, D_CONV), lambda b, i: (b * per_b + i, 0)),
            pl.BlockSpec((CONV_HALO, D_CONV),
                         lambda b, i: (jnp.maximum((b * per_b + i) * ratio - 1, 0), 0)),
            full((32, D_CONV)), full((1, D_CONV)), full((1, D_CONV)), full((1, D_CONV)),
            full((D_CONV, D_CONV)),
        ],
        out_specs=pl.BlockSpec((TS_CONV, D_CONV), lambda b, i: (b * per_b + i, 0)),
        out_shape=jax.ShapeDtypeStruct((t, D_CONV), BF16),
        scratch_shapes=[pltpu.VMEM((TS_CONV + CONV_HALO, D_CONV), F32),
                        pltpu.VMEM((SUBLANES - 1, TS_CONV + CONV_HALO - SUBLANES, D_CONV), F32),
                        pltpu.VMEM((TS_CONV, D_CONV), F32)],
        compiler_params=_cparams("arbitrary", "arbitrary"),
        name="conv_module",
    )(y2d, y2d, w_pad, row(conv_b), row(norm_g), row(norm_b), avg)


def _sb_kernel(q_ref, k_ref, v_ref, u_ref, o_ref, acc_ref, cr_ref):
    for pair in range(q_ref.shape[1] // LANES):
        _sb_head_pair(slice(pair * LANES, (pair + 1) * LANES),
                      q_ref, k_ref, v_ref, u_ref, o_ref, acc_ref, cr_ref)


def _sb_head_pair(lanes, q_ref, k_ref, v_ref, u_ref, o_ref, acc_ref, cr_ref):
    qi = pl.program_id(2)
    q = q_ref[:, lanes]
    lane = lax.broadcasted_iota(jnp.int32, (TQ, LANES), 1)
    row = lax.broadcasted_iota(jnp.int32, (TQ, TK), 0)
    col = lax.broadcasted_iota(jnp.int32, (TQ, TK), 1)
    qm = [jnp.where(lane < SB_HEAD_DIM, q, jnp.zeros_like(q)),
          jnp.where(lane >= SB_HEAD_DIM, q, jnp.zeros_like(q))]
    acc_ref[...] = jnp.zeros_like(acc_ref)
    cr_ref[...] = jnp.zeros_like(cr_ref)
    per_q = TQ // TK

    def key_blocks(first_kb, count, n_diagonal):
        for hd in range(2):
            cr = cr_ref[hd]
            acc = acc_ref[hd]
            for step in range(count):
                kb = first_kb - step
                ks = pl.multiple_of(kb * TK, TK)
                z = _dot_nt(qm[hd], k_ref[pl.ds(ks, TK), lanes])
                l1p = jnp.log(1.0 + jnp.exp(-jnp.abs(z)))
                log_keep = -jnp.maximum(z, 0.0) - l1p
                if step < n_diagonal:
                    strict = (col + (kb * TK - qi * TQ)) < row
                    log_keep = jnp.where(strict, log_keep, 0.0)
                later = _split_dot(log_keep, u_ref[...]) + cr
                a = jnp.exp(jnp.minimum(z, 0.0) - l1p + later)
                if step < n_diagonal:
                    a = jnp.where(strict, a, 0.0)
                acc = acc + _dot(a.astype(BF16), v_ref[pl.ds(ks, TK), lanes])
                cr = cr + jnp.sum(log_keep, axis=-1, keepdims=True)
            cr_ref[hd] = cr
            acc_ref[hd] = acc

    def alive():
        return (jnp.max(cr_ref[...]) > SB_DEAD).astype(jnp.int32)

    last_diag = qi * per_q + per_q - 1

    @pl.when(qi == 0)
    def _():
        key_blocks(last_diag, per_q, per_q)

    @pl.when(qi > 0)
    def _():
        key_blocks(last_diag, 2 * per_q, per_q)

    def cond(carry):
        kb, live = carry
        return jnp.logical_and(kb >= 0, live > 0)

    def body(carry):
        kb, _ = carry
        key_blocks(kb, 1, 0)
        return kb - 1, alive()

    lax.while_loop(cond, body, (last_diag - 2 * per_q, alive()))
    o_ref[:, lanes] = jnp.where(lane < SB_HEAD_DIM, acc_ref[0], acc_ref[1]).astype(BF16)


def _sb_attention(q, k, v, bsz, seq):
    t = q.shape[0]
    nq = seq // TQ
    idx = jnp.arange(TK)
    upper = (idx[:, None] > idx[None, :]).astype(BF16)
    width = SB_PAIRS_PER_STEP * LANES
    kv = pl.BlockSpec((seq, width), lambda b, p, i: (b, p))
    return pl.pallas_call(
        _sb_kernel,
        grid=(bsz, SB_HEADS // (2 * SB_PAIRS_PER_STEP), nq),
        in_specs=[
            pl.BlockSpec((TQ, width), lambda b, p, i: (b * nq + i, p)),
            kv, kv,
            pl.BlockSpec((TK, TK), lambda b, p, i: (0, 0)),
        ],
        out_specs=pl.BlockSpec((TQ, width), lambda b, p, i: (b * nq + i, p)),
        out_shape=jax.ShapeDtypeStruct((t, SB_HEADS * SB_HEAD_DIM), BF16),
        scratch_shapes=[pltpu.VMEM((2, TQ, LANES), F32), pltpu.VMEM((2, TQ, 1), F32)],
        compiler_params=_cparams("arbitrary", "arbitrary", "arbitrary"),
        name="stick_breaking",
    )(q, k, v, upper)


def _first_of(vals, target):
    idx = jnp.full(target.shape, len(vals) - 1, jnp.int32)
    for j in range(len(vals) - 2, -1, -1):
        idx = jnp.where(vals[j] == target, j, idx)
    return idx


def _top2(vals):
    m1 = functools.reduce(jnp.maximum, vals)
    i1 = _first_of(vals, m1)
    rest = [jnp.where(i1 == j, -jnp.inf, v) for j, v in enumerate(vals)]
    m2 = functools.reduce(jnp.maximum, rest)
    i2 = _first_of(rest, m2)
    return m1, m2, i1, i2


def _pick(idx, vals):
    out = vals[-1]
    for j in range(len(vals) - 2, -1, -1):
        out = jnp.where(idx == j, vals[j], out)
    return out


def _out_router_kernel(expert_rows, x_ref, a_ref, b_ref, w_ref, g1_ref, ng_ref, sc_ref, sh_ref,
                       rw_ref, rb_ref, tri_ref,
                       x1_ref, hp_ref, pos_ref, wt_ref, cnt_ref, sched_ref, member):
    half = a_ref.shape[1]
    mix = _dot(a_ref[...], w_ref[0:half, :]) + _dot(b_ref[...], w_ref[half:, :])
    x1 = x_ref[...] + g1_ref[0] * mix
    x1_ref[...] = x1
    h = _modulated_rmsnorm(x1, ng_ref[...], sc_ref[0], sh_ref[0])
    hb = h.astype(BF16)
    _to_slabs(hp_ref, h)

    hl = (h - hb.astype(F32)).astype(BF16)
    both = _dot_nt(rw_ref[...], hb)
    logits = (both[0:N_EXPERTS, :] + both[N_EXPERTS:, :]
              + _dot_nt(rw_ref[0:N_EXPERTS, :], hl))
    ex = jnp.exp(logits - jnp.max(logits, axis=0, keepdims=True))
    scores = ex / jnp.sum(ex, axis=0, keepdims=True)
    sel = scores + rb_ref[...]
    srow = [scores[e:e + 1, :] for e in range(N_EXPERTS)]
    lrow = [sel[e:e + 1, :] for e in range(N_EXPERTS)]
    tops = [_top2(lrow[g * EXPERTS_PER_GROUP:(g + 1) * EXPERTS_PER_GROUP])
            for g in range(N_EXPERT_GROUPS)]
    gscore = [tp[0] + tp[1] for tp in tops]
    gidx = _first_of(gscore, functools.reduce(jnp.maximum, gscore))
    e1 = gidx * EXPERTS_PER_GROUP + _pick(gidx, [tp[2] for tp in tops])
    e2 = gidx * EXPERTS_PER_GROUP + _pick(gidx, [tp[3] for tp in tops])
    w1 = _pick(e1, srow)
    w2 = _pick(e2, srow)
    tot = w1 + w2
    wt_ref[0:1, :] = w1 / tot
    wt_ref[1:2, :] = w2 / tot

    @pl.when(pl.program_id(0) == 0)
    def _():
        cnt_ref[...] = jnp.zeros_like(cnt_ref)

    for e in range(N_EXPERTS):
        member[e:e + 1, :] = jnp.where(jnp.logical_or(e1 == e, e2 == e), 1.0, 0.0)
    m = member[...]
    incl = _dot(m.astype(BF16), tri_ref[...])
    excl = incl - m + cnt_ref[:, 0:1]
    cnt_ref[...] = cnt_ref[...] + incl[:, TM - 1:TM]
    erow = [excl[e:e + 1, :] for e in range(N_EXPERTS)]
    pos_ref[0:1, :] = (e1 * expert_rows + _pick(e1, erow).astype(jnp.int32)) * SUBLANES
    pos_ref[1:2, :] = (e2 * expert_rows + _pick(e2, erow).astype(jnp.int32)) * SUBLANES

    @pl.when(pl.program_id(0) == pl.num_programs(0) - 1)
    def _():
        _tile_schedule(cnt_ref[...], expert_rows, sched_ref)


def _tile_schedule(cnt, n_tok, sched_ref):
    blocks_per_expert = n_tok // TM_MOE
    tiles = jnp.floor((cnt + (TM_MOE - 1)) * (1.0 / TM_MOE))
    e_row = lax.broadcasted_iota(jnp.int32, (N_EXPERTS, N_EXPERTS), 0)
    e_col = lax.broadcasted_iota(jnp.int32, (N_EXPERTS, N_EXPERTS), 1)
    lower = jnp.where(e_col <= e_row, 1.0, 0.0).astype(BF16)
    ends = _dot(lower, tiles.astype(BF16))
    starts = ends - tiles
    e_id = lax.broadcasted_iota(jnp.int32, cnt.shape, 0).astype(F32)
    j = lax.broadcasted_iota(jnp.int32, cnt.shape, 1).astype(F32)
    total = ends[N_EXPERTS - 1:N_EXPERTS, :]
    e_j = jnp.minimum(jnp.sum(jnp.where(j >= ends, 1.0, 0.0), axis=0, keepdims=True),
                      float(N_EXPERTS - 1))
    mine = e_id == e_j
    start_j = jnp.sum(jnp.where(mine, starts, 0.0), axis=0, keepdims=True)
    cnt_j = jnp.sum(jnp.where(mine, cnt, 0.0), axis=0, keepdims=True)
    last_e = jnp.max(jnp.where(tiles > 0.0, e_id, 0.0), axis=0, keepdims=True)
    j_row = j[0:1, :]
    local = j_row - start_j
    used = j_row < total
    sched_ref[...] = jnp.zeros_like(sched_ref)
    sched_ref[0:1, :] = jnp.where(used, e_j, last_e).astype(jnp.int32)
    sched_ref[1:2, :] = jnp.where(used, e_j * blocks_per_expert + local,
                                  float(N_EXPERTS * blocks_per_expert)).astype(jnp.int32)
    sched_ref[2:3, :] = jnp.where(used, jnp.clip(cnt_j - local * TM_MOE, 0.0, float(TM_MOE)),
                                  0.0).astype(jnp.int32)


def _out_router(x2d, a, b, w_out, g1, ng, sc, sh, rw_stack, rb, seq):
    t, d = x2d.shape
    per_b = seq // TM
    vec = pl.BlockSpec((1, 1, d), lambda i: (i // per_b, 0, 0))
    full = lambda shape: pl.BlockSpec(shape, lambda i: (0, 0))
    half = a.shape[1]
    idx = jnp.arange(TM)
    tri = (idx[:, None] <= idx[None, :]).astype(BF16)
    assert d == SUBLANES * LANES and 2 * t // TM_MOE + N_EXPERTS <= LANES
    return pl.pallas_call(
        functools.partial(_out_router_kernel, t),
        grid=(t // TM,),
        in_specs=[
            pl.BlockSpec((TM, d), lambda i: (i, 0)),
            pl.BlockSpec((TM, half), lambda i: (i, 0)),
            pl.BlockSpec((TM, half), lambda i: (i, 0)),
            full((2 * half, d)),
            vec, full((1, d)), vec, vec,
            full((2 * N_EXPERTS, d)), full((N_EXPERTS, 1)), full((TM, TM)),
        ],
        out_specs=[
            pl.BlockSpec((TM, d), lambda i: (i, 0)),
            pl.BlockSpec((TM * SUBLANES, LANES), lambda i: (i, 0)),
            pl.BlockSpec((2, TM), lambda i: (0, i)),
            pl.BlockSpec((2, TM), lambda i: (0, i)),
            pl.BlockSpec((N_EXPERTS, LANES), lambda i: (0, 0)),
            pl.BlockSpec((SUBLANES, LANES), lambda i: (0, 0)),
        ],
        out_shape=[
            jax.ShapeDtypeStruct((t, d), F32),
            jax.ShapeDtypeStruct((t * SUBLANES, LANES), F32),
            jax.ShapeDtypeStruct((2, t), jnp.int32),
            jax.ShapeDtypeStruct((2, t), F32),
            jax.ShapeDtypeStruct((N_EXPERTS, LANES), F32),
            jax.ShapeDtypeStruct((SUBLANES, LANES), jnp.int32),
        ],
        scratch_shapes=[pltpu.VMEM((N_EXPERTS, TM), F32)],
        compiler_params=_cparams("arbitrary"),
        name="out_router",
    )(x2d, a, b, w_out, g1, ng, sc, sh, rw_stack, rb, tri)


def _to_slabs(ref, x):
    m = x.shape[0]
    for j in range(SUBLANES):
        ref[pl.ds(j, m, stride=SUBLANES), :] = x[:, j * LANES:(j + 1) * LANES]


def _from_slabs(ref):
    m = ref.shape[0] // SUBLANES
    return jnp.concatenate([ref[pl.ds(j, m, stride=SUBLANES), :] for j in range(SUBLANES)],
                           axis=-1)


def _slab(ref, offset):
    return ref.at[pl.ds(pl.multiple_of(offset, SUBLANES), SUBLANES), :]


def _dispatch_kernel(pos_ref, hp_ref, xs_ref, sem):
    n_tok = pos_ref.shape[0] // 2
    base = pl.program_id(0) * TM

    def issue(r, carry):
        for pick in range(2):
            pltpu.make_async_copy(_slab(hp_ref, r * SUBLANES),
                                  _slab(xs_ref, pos_ref[pick * n_tok + base + r]),
                                  sem).start(priority=pick)
        return carry

    lax.fori_loop(0, TM, issue, 0, unroll=8)
    for _ in range(2):
        pltpu.make_async_copy(hp_ref, xs_ref.at[pl.ds(0, TM * SUBLANES), :], sem).wait()


def _dispatch(pos_flat, hp, rows):
    t = hp.shape[0] // SUBLANES
    return pl.pallas_call(
        _dispatch_kernel,
        grid_spec=pltpu.PrefetchScalarGridSpec(
            num_scalar_prefetch=1,
            grid=(t // TM,),
            in_specs=[pl.BlockSpec((TM * SUBLANES, LANES), lambda i, pos: (i, 0))],
            out_specs=pl.BlockSpec(memory_space=pl.ANY),
            scratch_shapes=[pltpu.SemaphoreType.DMA],
        ),
        out_shape=jax.ShapeDtypeStruct((rows * SUBLANES, LANES), F32),
        compiler_params=_cparams("arbitrary"),
        name="moe_dispatch",
    )(pos_flat, hp)


def _expert_kernel(te_ref, tb_ref, tv_ref, xs_ref, wg_ref, wu_ref, wd_ref, ys_ref,
                   wg_bf, wu_bf, wd_bf):
    i = pl.program_id(0)
    expert = te_ref[i]
    valid = tv_ref[i]

    @pl.when(jnp.logical_or(i == 0, expert != te_ref[jnp.maximum(i - 1, 0)]))
    def _():
        wg_bf[...] = wg_ref[0].astype(BF16)
        wu_bf[...] = wu_ref[0].astype(BF16)
        wd_bf[...] = wd_ref[0].astype(BF16)

    @pl.when(valid > 0)
    def _():
        x = _from_slabs(xs_ref)
        row = lax.broadcasted_iota(jnp.int32, x.shape, 0)
        h = jnp.where(row < valid, x, 0.0).astype(BF16)
        gate = _dot(h, wg_bf[...])
        up = _dot(h, wu_bf[...])
        act = (gate * _sigmoid(gate) * up).astype(BF16)
        _to_slabs(ys_ref, _dot(act, wd_bf[...]))

    @pl.when(valid == 0)
    def _():
        ys_ref[...] = jnp.zeros_like(ys_ref)


def _experts(tile_expert, tile_block, tile_valid, xs, layer, wg, wu, wd):
    d = SUBLANES * LANES
    n_tiles = tile_expert.shape[0]
    wspec = lambda shape: pl.BlockSpec((None,) + shape, lambda i, te, tb, tv: (layer, te[i], 0, 0))
    rowspec = pl.BlockSpec((TM_MOE * SUBLANES, LANES), lambda i, te, tb, tv: (tb[i], 0))
    return pl.pallas_call(
        _expert_kernel,
        grid_spec=pltpu.PrefetchScalarGridSpec(
            num_scalar_prefetch=3,
            grid=(n_tiles,),
            in_specs=[
                rowspec,
                wspec((1, d, D_EXPERT)), wspec((1, d, D_EXPERT)), wspec((1, D_EXPERT, d)),
            ],
            out_specs=rowspec,
            scratch_shapes=[pltpu.VMEM((d, D_EXPERT), BF16), pltpu.VMEM((d, D_EXPERT), BF16),
                            pltpu.VMEM((D_EXPERT, d), BF16)],
        ),
        out_shape=jax.ShapeDtypeStruct(xs.shape, F32),
        compiler_params=_cparams("arbitrary"),
        name="moe_experts",
    )(tile_expert, tile_block, tile_valid, xs, wg, wu, wd)


def _combine_kernel(final, pos_ref, x1_ref, wt_ref, g2_ref, fn_ref, ys_ref, o_ref, buf, sems):
    n_tok = pos_ref.shape[0] // 2
    i = pl.program_id(0)
    n = pl.num_programs(0)

    def gather(tile, slot):
        base = tile * TM

        def issue(r, carry):
            for pick in range(2):
                pltpu.make_async_copy(_slab(ys_ref, pos_ref[pick * n_tok + base + r]),
                                      _slab(buf.at[slot, pick], r * SUBLANES),
                                      sems.at[slot]).start(priority=pick)
            return carry

        lax.fori_loop(0, TM, issue, 0, unroll=8)

    @pl.when(i == 0)
    def _():
        gather(0, 0)

    @pl.when(i + 1 < n)
    def _():
        gather(i + 1, (i + 1) % 2)

    slot = i % 2
    for pick in range(2):
        pltpu.make_async_copy(ys_ref.at[pl.ds(0, TM * SUBLANES), :], buf.at[slot, pick],
                              sems.at[slot]).wait()
    y = (wt_ref[:, 0:1] * _from_slabs(buf.at[slot, 0])
         + wt_ref[:, 1:2] * _from_slabs(buf.at[slot, 1]))
    x2 = x1_ref[...] + g2_ref[0] * y
    if final:
        ms = jnp.mean(x2 * x2, axis=-1, keepdims=True)
        x2 = x2 * lax.rsqrt(ms + RMS_EPS) * fn_ref[...]
    o_ref[...] = x2


def _combine(pos_flat, x1, wt, g2, fn, ys, seq, final):
    t, d = x1.shape
    per_b = seq // TM
    return pl.pallas_call(
        functools.partial(_combine_kernel, final),
        grid_spec=pltpu.PrefetchScalarGridSpec(
            num_scalar_prefetch=1,
            grid=(t // TM,),
            in_specs=[
                pl.BlockSpec((TM, d), lambda i, pos: (i, 0)),
                pl.BlockSpec((TM, 2), lambda i, pos: (i, 0)),
                pl.BlockSpec((1, 1, d), lambda i, pos: (i // per_b, 0, 0)),
                pl.BlockSpec((1, d), lambda i, pos: (0, 0)),
                pl.BlockSpec(memory_space=pl.ANY),
            ],
            out_specs=pl.BlockSpec((TM, d), lambda i, pos: (i, 0)),
            scratch_shapes=[pltpu.VMEM((2, 2, TM * SUBLANES, LANES), F32),
                            pltpu.SemaphoreType.DMA((2,))],
        ),
        out_shape=jax.ShapeDtypeStruct((t, d), F32),
        compiler_params=_cparams("arbitrary"),
        name="moe_combine_final" if final else "moe_combine",
    )(pos_flat, x1, wt, g2, fn, ys)


def _moe(x1, hp, pos, wt, sched, layer, wg, wu, wd, g2, fn, seq, final):
    t = x1.shape[0]
    n_tiles = 2 * t // TM_MOE + N_EXPERTS
    rows = N_EXPERTS * t + TM_MOE
    pos_flat = pos.reshape(-1)
    xs = _dispatch(pos_flat, hp, rows)
    ys = _experts(sched[0, :n_tiles], sched[1, :n_tiles], sched[2, :n_tiles], xs, layer,
                  wg, wu, wd)
    return _combine(pos_flat, x1, wt.T, g2, fn, ys, seq, final)


def _log_sigmoid(x):
    return jnp.minimum(x, 0.0) - jnp.log1p(jnp.exp(-jnp.abs(x)))


def _odd_in_kernel(x_ref, g_ref, sc_ref, sh_ref, w_ref, gw_ref, gb_ref,
                   q_ref, k_ref, v_ref, sr_ref, la_ref, up_ref):
    h = _modulated_rmsnorm(x_ref[...], g_ref[...], sc_ref[0], sh_ref[0]).astype(BF16)
    q_ref[...] = _dot(h, w_ref[:, 0:512]) * (GLA_DK ** -0.5)
    k_ref[...] = _dot(h, w_ref[:, 512:1024])
    v_ref[...] = _dot(h, w_ref[:, 1024:1536]).astype(BF16)
    r = _dot(h, w_ref[:, 1536:2048])
    sr_ref[...] = r * _sigmoid(r)
    up_ref[...] = _dot(h, w_ref[:, 2048:2560])
    a_low = _dot(h, w_ref[:, 2560:2688]).astype(BF16)
    la_ref[...] = _log_sigmoid(_dot(a_low, gw_ref[...]) + gb_ref[...]) * (1.0 / GLA_TAU)


def _odd_in(x2d, g, sc, sh, w_cat, gate_w, gate_b, seq):
    t, d = x2d.shape
    n = w_cat.shape[1]
    per_b = seq // TM
    vec = pl.BlockSpec((1, 1, d), lambda i: (i // per_b, 0, 0))
    full = lambda shape: pl.BlockSpec(shape, lambda i: (0, 0))
    out = lambda dt: jax.ShapeDtypeStruct((t, 512), dt)
    return pl.pallas_call(
        _odd_in_kernel,
        grid=(t // TM,),
        in_specs=[
            pl.BlockSpec((TM, d), lambda i: (i, 0)),
            full((1, d)), vec, vec, full((d, n)),
            full((LANES, 512)), full((1, 512)),
        ],
        out_specs=[pl.BlockSpec((TM, 512), lambda i: (i, 0))] * 6,
        out_shape=[out(F32), out(F32), out(BF16), out(F32), out(F32), out(F32)],
        compiler_params=_cparams("arbitrary"),
        name="odd_in",
    )(x2d, g, sc, sh, w_cat, gate_w, gate_b)


def _gla_kernel(q_ref, k_ref, la_ref, v_ref, sr_ref, g_ref, fwd_ref, rev_ref, o_ref, st_ref):
    group = GLA_CHUNK * GLA_UNROLL
    row = lax.broadcasted_iota(jnp.int32, (group, group), 0)
    col = lax.broadcasted_iota(jnp.int32, (group, group), 1)
    shift = GLA_CHUNK.bit_length() - 1
    keep = jnp.logical_and(jnp.right_shift(row, shift) == jnp.right_shift(col, shift), col <= row)

    @pl.when(pl.program_id(1) == 0)
    def _():
        st_ref[...] = jnp.zeros_like(st_ref)

    def chunk_group(n, carry):
        rows = pl.ds(pl.multiple_of(n * group, group), group)
        la = la_ref[rows, :]
        b_all = _split_dot_left(fwd_ref[...], la)
        r_all = _split_dot_left(rev_ref[...], la)
        for hd in range(GLA_HEADS):
            ls = slice(hd * LANES, (hd + 1) * LANES)
            b = b_all[:, ls]
            k = k_ref[rows, ls]
            v = v_ref[rows, ls]
            q_in = (q_ref[rows, ls] * jnp.exp(b)).astype(BF16)
            k_in = (k * jnp.exp(-b)).astype(BF16)
            k_dec = (k * jnp.exp(r_all[:, ls])).astype(BF16)
            scores = jnp.where(keep, _dot_nt(q_in, k_in), 0.0).astype(BF16)
            o_intra = _dot(scores, v)
            state_t = st_ref[hd]
            outs = []
            for c in range(GLA_UNROLL):
                cs = slice(c * GLA_CHUNK, (c + 1) * GLA_CHUNK)
                outs.append(o_intra[cs, :] + _dot_nt(q_in[cs, :], state_t.astype(BF16)))
                b_last = b[(c + 1) * GLA_CHUNK - 1:(c + 1) * GLA_CHUNK, :]
                state_t = state_t * jnp.exp(b_last) + _dot_tn(v[cs, :], k_dec[cs, :])
            st_ref[hd] = state_t
            o = jnp.concatenate(outs, axis=0)
            o = o * lax.rsqrt(jnp.mean(o * o, axis=-1, keepdims=True) + RMS_EPS)
            o_ref[rows, ls] = (o * g_ref[:, ls] * sr_ref[rows, ls]).astype(BF16)
        return carry

    lax.fori_loop(0, TS_GLA // group, chunk_group, 0)


def _gla(q, k, la, v, sr, norm_g, bsz, seq):
    t, width = q.shape
    per_b = seq // TS_GLA
    group = GLA_CHUNK * GLA_UNROLL
    idx = jnp.arange(group)
    same = (idx[:, None] // GLA_CHUNK) == (idx[None, :] // GLA_CHUNK)
    fwd = jnp.logical_and(same, idx[:, None] >= idx[None, :]).astype(BF16)
    rev = jnp.logical_and(same, idx[:, None] < idx[None, :]).astype(BF16)
    blk = pl.BlockSpec((TS_GLA, width), lambda b, i: (b * per_b + i, 0))
    tri_spec = pl.BlockSpec((group, group), lambda b, i: (0, 0))
    return pl.pallas_call(
        _gla_kernel,
        grid=(bsz, per_b),
        in_specs=[blk, blk, blk, blk, blk,
                  pl.BlockSpec((1, width), lambda b, i: (0, 0)),
                  tri_spec, tri_spec],
        out_specs=blk,
        out_shape=jax.ShapeDtypeStruct((t, width), BF16),
        scratch_shapes=[pltpu.VMEM((GLA_HEADS, GLA_DV, LANES), F32)],
        compiler_params=_cparams("arbitrary", "arbitrary"),
        name="gla",
    )(q, k, la, v, sr, norm_g.reshape(1, -1), fwd, rev)


def _pool_kernel(u_ref, halo_ref, w_ref, b_ref, s_ref, o_ref, buf):
    i = pl.program_id(1)
    buf[0:POOL_HALO, :] = jnp.where(i > 0, halo_ref[...], 0.0)
    buf[POOL_HALO:, :] = u_ref[...]
    pos = (i * TS_POOL + 1 + lax.broadcasted_iota(jnp.int32, (TS_POOL, 1), 0)).astype(F32)
    for gi, win in enumerate(POOL_WINDOWS):
        ls = slice(gi * LANES, (gi + 1) * LANES)
        tok = buf[POOL_HALO:, ls]
        tot = tok
        for back in range(1, win):
            tot = tot + buf[POOL_HALO - back:POOL_HALO - back + TS_POOL, ls]
        p = tot / jnp.minimum(pos, float(win)) - tok
        y = _dot(p.astype(BF16), w_ref[gi]) + b_ref[:, ls]
        o_ref[:, ls] = (y * s_ref[:, ls]).astype(BF16)


def _pool(u2d, pool_w, pool_b, pool_scale, bsz, seq):
    t = u2d.shape[0]
    per_b = seq // TS_POOL
    ratio = TS_POOL // POOL_HALO
    full = lambda shape: pl.BlockSpec(shape, lambda b, i: (0,) * len(shape))
    return pl.pallas_call(
        _pool_kernel,
        grid=(bsz, per_b),
        in_specs=[
            pl.BlockSpec((TS_POOL, D_POOL), lambda b, i: (b * per_b + i, 0)),
            pl.BlockSpec((POOL_HALO, D_POOL),
                         lambda b, i: (jnp.maximum((b * per_b + i) * ratio - 1, 0), 0)),
            full((len(POOL_WINDOWS), LANES, LANES)), full((1, D_POOL)), full((1, D_POOL)),
        ],
        out_specs=pl.BlockSpec((TS_POOL, D_POOL), lambda b, i: (b * per_b + i, 0)),
        out_shape=jax.ShapeDtypeStruct((t, D_POOL), BF16),
        scratch_shapes=[pltpu.VMEM((TS_POOL + POOL_HALO, D_POOL), F32)],
        compiler_params=_cparams("arbitrary", "arbitrary"),
        name="pool_mixer",
    )(u2d, u2d, pool_w.astype(BF16), pool_b.reshape(1, D_POOL), pool_scale.reshape(1, D_POOL))


def _pad_heads(w, heads, dim):
    lead = w.shape[:-1]
    w = w.reshape(lead + (heads, dim))
    w = jnp.pad(w, [(0, 0)] * len(lead) + [(0, 0), (0, LANES - dim)])
    return w.reshape(lead + (heads * LANES,))


def _odd_weights(w_in, gate_w, gate_b):
    dkt = GLA_HEADS * GLA_DK
    dvt = GLA_HEADS * GLA_DV
    o = [0, dkt, 2 * dkt, 2 * dkt + dvt, 2 * dkt + 2 * dvt, 2 * dkt + 2 * dvt + GLA_RANK]
    wq, wk, wv, wr, wa, wp = [w_in[:, o[j]:(o + [w_in.shape[1]])[j + 1]] for j in range(6)]
    w_cat = jnp.concatenate([
        _pad_heads(wq, GLA_HEADS, GLA_DK), _pad_heads(wk, GLA_HEADS, GLA_DK), wv, wr, wp,
        jnp.pad(wa, ((0, 0), (0, LANES - GLA_RANK)))], axis=1).astype(BF16)
    gw = jnp.pad(_pad_heads(gate_w, GLA_HEADS, GLA_DK), ((0, LANES - GLA_RANK), (0, 0))).astype(BF16)
    gb = _pad_heads(gate_b, GLA_HEADS, GLA_DK).reshape(1, -1)
    return w_cat, gw, gb


def kernel(x, c, ada_w, ada_b, norm_mix, norm_ffn, w_in_even, w_out_even, conv_w, conv_b, conv_norm_g, conv_norm_b, w_in_odd, w_out_odd, gla_gate_w, gla_gate_b, gla_norm_g, pool_w, pool_b, pool_scale, router_w, router_bias, moe_w_gate, moe_w_up, moe_w_down, final_norm):
    bsz, seq, d = x.shape
    t = bsz * seq
    depth = ada_w.shape[0]
    mod = _ada(c, ada_w, ada_b)
    rw_hi = router_w.T.astype(BF16)
    rw_lo = (router_w.T - rw_hi.astype(F32)).astype(BF16)
    rw_stack = jnp.concatenate([rw_hi, rw_lo], axis=0)
    rb =router_bias.reshape(N_EXPERTS, 1)
    fn = final_norm.reshape(1, d)
    x2d = x.reshape(t, d)
    for l in range(depth):
        sh1, sc1, g1, sh2, sc2, g2 = [mod[l, :, j * d:(j + 1) * d].reshape(bsz, 1, d)
                                      for j in range(6)]
        i = l // 2
        nm = norm_mix[l].reshape(1, d)
        if l % 2 == 0:
            y, q, k, v = _even_in(x2d, nm, sc1, sh1, w_in_even[i].astype(BF16), seq)
            a = _conv(y, conv_w[i], conv_b[i], conv_norm_g[i], conv_norm_b[i], bsz, seq)
            b = _sb_attention(q, k, v, bsz, seq)
            w_out = w_out_even[i]
        else:
            w_cat, gw, gb = _odd_weights(w_in_odd[i], gla_gate_w[i], gla_gate_b[i])
            q, k, v, sr, la, up = _odd_in(x2d, nm, sc1, sh1, w_cat, gw, gb, seq)
            a = _gla(q, k, la, v, sr, gla_norm_g[i], bsz, seq)
            b = _pool(up, pool_w[i], pool_b[i], pool_scale[i], bsz, seq)
            w_out = w_out_odd[i]
        x1, hp, pos, wt, _, sched = _out_router(x2d, a, b, w_out.astype(BF16), g1,
                                                norm_ffn[l].reshape(1, d), sc2, sh2,
                                                rw_stack, rb, seq)
        x2d = _moe(x1, hp, pos, wt, sched, l, moe_w_gate, moe_w_up, moe_w_down, g2, fn, seq,
                   final=(l == depth - 1))
    return x2d.reshape(bsz, seq, d)
```

```python
import functools

import jax
import jax.numpy as jnp
from jax import lax
from jax.experimental import pallas as pl
from jax.experimental.pallas import tpu as pltpu

F32 = jnp.float32
BF16 = jnp.bfloat16

D_MODEL = 1024
D_CONV = 512
CONV_WIDTH = 31
CONV_GROUP_SIZE = 64
SB_HEADS = 8
SB_HEAD_DIM = 64
GLA_HEADS = 4
GLA_DK = 64
GLA_DV = 128
GLA_RANK = 16
GLA_TAU = 16.0
GLA_CHUNK = 64
D_POOL = 512
POOL_WINDOWS = (2, 4, 8, 16)
N_EXPERTS = 16
N_EXPERT_GROUPS = 4
EXPERTS_PER_GROUP = 4
D_EXPERT = 512
RMS_EPS = 1e-6
LN_EPS = 1e-5

LANES = 128
SUBLANES = 8
VMEM_LIMIT = 56 * 1024 * 1024

TM = 512
TS_CONV = 256
CONV_HALO = 32
CONV_ROWS = 64
TS_GLA = 1024
GLA_UNROLL = 4
TS_POOL = 512
POOL_HALO = 16
TQ = 256
TK = 256
SB_PAIRS_PER_STEP = 2
TM_MOE = 512
SB_DEAD = -120.0


def _cparams(*sem):
    return pltpu.CompilerParams(dimension_semantics=sem, vmem_limit_bytes=VMEM_LIMIT)


def _sigmoid(x):
    return 1.0 / (1.0 + jnp.exp(-x))


def _dot(a, b):
    return jnp.dot(a, b, preferred_element_type=F32)


def _dot_nt(a, b, **kw):
    return lax.dot_general(a, b, (((1,), (1,)), ((), ())), preferred_element_type=F32, **kw)


def _dot_tn(a, b):
    return lax.dot_general(a, b, (((0,), (0,)), ((), ())), preferred_element_type=F32)


def _split_dot(x, m):
    hi = x.astype(BF16)
    lo = (x - hi.astype(F32)).astype(BF16)
    return _dot(hi, m) + _dot(lo, m)


def _split_dot_left(m, x):
    hi = x.astype(BF16)
    lo = (x - hi.astype(F32)).astype(BF16)
    return _dot(m, hi) + _dot(m, lo)


def _modulated_rmsnorm(x, g, sc, sh):
    ms = jnp.mean(x * x, axis=-1, keepdims=True)
    return (x * lax.rsqrt(ms + RMS_EPS) * g) * (1.0 + sc) + sh


def _ada_kernel(c_ref, w_ref, b_ref, o_ref):
    c = c_ref[...]
    cond = c * _sigmoid(c)
    o_ref[0] = jnp.dot(cond, w_ref[0], preferred_element_type=F32,
                       precision=lax.Precision.HIGHEST) + b_ref[0]


def _ada(c, ada_w, ada_b):
    depth, d, n = ada_w.shape
    bsz = c.shape[0]
    tn = 1536
    return pl.pallas_call(
        _ada_kernel,
        grid=(depth, n // tn),
        in_specs=[
            pl.BlockSpec((bsz, d), lambda l, j: (0, 0)),
            pl.BlockSpec((1, d, tn), lambda l, j: (l, 0, j)),
            pl.BlockSpec((1, 1, tn), lambda l, j: (l, 0, j)),
        ],
        out_specs=pl.BlockSpec((1, bsz, tn), lambda l, j: (l, 0, j)),
        out_shape=jax.ShapeDtypeStruct((depth, bsz, n), F32),
        compiler_params=_cparams("arbitrary", "arbitrary"),
        name="ada_mod",
    )(c, ada_w, ada_b.reshape(depth, 1, n))


def _even_in_kernel(x_ref, g_ref, sc_ref, sh_ref, w_ref, y_ref, q_ref, k_ref, v_ref):
    h = _modulated_rmsnorm(x_ref[...], g_ref[...], sc_ref[0], sh_ref[0]).astype(BF16)
    val = _dot(h, w_ref[:, 0:512])
    gate = _dot(h, w_ref[:, 512:1024])
    y_ref[...] = val * _sigmoid(gate)
    q_ref[...] = (_dot(h, w_ref[:, 1024:1536]) * (SB_HEAD_DIM ** -0.5)).astype(BF16)
    k_ref[...] = _dot(h, w_ref[:, 1536:2048]).astype(BF16)
    v_ref[...] = _dot(h, w_ref[:, 2048:2560]).astype(BF16)


def _even_in(x2d, g, sc, sh, w_in, seq):
    t, d = x2d.shape
    n = w_in.shape[1]
    per_b = seq // TM
    vec = pl.BlockSpec((1, 1, d), lambda i: (i // per_b, 0, 0))
    out = lambda dt: jax.ShapeDtypeStruct((t, 512), dt)
    return pl.pallas_call(
        _even_in_kernel,
        grid=(t // TM,),
        in_specs=[
            pl.BlockSpec((TM, d), lambda i: (i, 0)),
            pl.BlockSpec((1, d), lambda i: (0, 0)),
            vec, vec,
            pl.BlockSpec((d, n), lambda i: (0, 0)),
        ],
        out_specs=[pl.BlockSpec((TM, 512), lambda i: (i, 0))] * 4,
        out_shape=[out(F32), out(BF16), out(BF16), out(BF16)],
        compiler_params=_cparams("arbitrary"),
        name="even_in",
    )(x2d, g, sc, sh, w_in)


def _conv_kernel(y_ref, halo_ref, w_ref, b_ref, ng_ref, nb_ref, avg_ref, o_ref, buf, sh, cv):
    i = pl.program_id(1)
    buf[0:CONV_HALO, :] = jnp.where(i > 0, halo_ref[...], 0.0)
    buf[CONV_HALO:, :] = y_ref[...]
    sh_rows = sh.shape[1]
    for r in range(1, SUBLANES):
        sh[r - 1] = buf[r:r + sh_rows, :]
    first = CONV_HALO - (CONV_WIDTH - 1)
    for rc in range(TS_CONV // CONV_ROWS):
        r0 = rc * CONV_ROWS
        for lc in range(D_CONV // LANES):
            ls = slice(lc * LANES, (lc + 1) * LANES)
            acc = jnp.zeros((CONV_ROWS, LANES), F32)
            for j in range(CONV_WIDTH):
                whole, r = divmod(first + j, SUBLANES)
                s = r0 + whole * SUBLANES
                src = buf[s:s + CONV_ROWS, ls] if r == 0 else sh[r - 1, s:s + CONV_ROWS, ls]
                acc = acc + w_ref[j:j + 1, ls] * src
            cv[r0:r0 + CONV_ROWS, ls] = acc + b_ref[:, ls]
    y = cv[...]
    avg = avg_ref[...]
    d = y - _split_dot(y, avg)
    var = _split_dot(d * d, avg)
    yn = d * lax.rsqrt(var + LN_EPS) * ng_ref[...] + nb_ref[...]
    o_ref[...] = (yn * _sigmoid(yn)).astype(BF16)


def _conv(y2d, conv_w, conv_b, norm_g, norm_b, bsz, seq):
    t = y2d.shape[0]
    per_b = seq // TS_CONV
    ratio = TS_CONV // CONV_HALO
    gid = jnp.arange(D_CONV) // CONV_GROUP_SIZE
    avg = jnp.where(gid[:, None] == gid[None, :], 1.0 / CONV_GROUP_SIZE, 0.0).astype(BF16)
    w_pad = jnp.pad(conv_w, ((0, 32 - CONV_WIDTH), (0, 0)))
    row = lambda a: a.reshape(1, D_CONV)
    full = lambda shape: pl.BlockSpec(shape, lambda b, i: (0, 0))
    return pl.pallas_call(
        _conv_kernel,
        grid=(bsz, per_b),
        in_specs=[
            pl.BlockSpec((TS_CONV, D_CONV), lambda b, i: (b * per_b + i, 0)),
            pl.BlockSpec((CONV_HALO, D_CONV),
                         lambda b, i: (jnp.maximum((b * per_b + i) * ratio - 1, 0), 0)),
            full((32, D_CONV)), full((1, D_CONV)), full((1, D_CONV)), full((1, D_CONV)),
            full((D_CONV, D_CONV)),
        ],
        out_specs=pl.BlockSpec((TS_CONV, D_CONV), lambda b, i: (b * per_b + i, 0)),
        out_shape=jax.ShapeDtypeStruct((t, D_CONV), BF16),
        scratch_shapes=[pltpu.VMEM((TS_CONV + CONV_HALO, D_CONV), F32),
                        pltpu.VMEM((SUBLANES - 1, TS_CONV + CONV_HALO - SUBLANES, D_CONV), F32),
                        pltpu.VMEM((TS_CONV, D_CONV), F32)],
        compiler_params=_cparams("arbitrary", "arbitrary"),
        name="conv_module",
    )(y2d, y2d, w_pad, row(conv_b), row(norm_g), row(norm_b), avg)


def _sb_kernel(q_ref, k_ref, v_ref, u_ref, o_ref, acc_ref, cr_ref):
    for pair in range(q_ref.shape[1] // LANES):
        _sb_head_pair(slice(pair * LANES, (pair + 1) * LANES),
                      q_ref, k_ref, v_ref, u_ref, o_ref, acc_ref, cr_ref)


def _sb_head_pair(lanes, q_ref, k_ref, v_ref, u_ref, o_ref, acc_ref, cr_ref):
    qi = pl.program_id(2)
    q = q_ref[:, lanes]
    lane = lax.broadcasted_iota(jnp.int32, (TQ, LANES), 1)
    row = lax.broadcasted_iota(jnp.int32, (TQ, TK), 0)
    col = lax.broadcasted_iota(jnp.int32, (TQ, TK), 1)
    qm = [jnp.where(lane < SB_HEAD_DIM, q, jnp.zeros_like(q)),
          jnp.where(lane >= SB_HEAD_DIM, q, jnp.zeros_like(q))]
    acc_ref[...] = jnp.zeros_like(acc_ref)
    cr_ref[...] = jnp.zeros_like(cr_ref)
    per_q = TQ // TK

    def key_blocks(first_kb, count, n_diagonal):
        stage1 = []
        for hd in range(2):
            for step in range(count):
                kb = first_kb - step
                ks = pl.multiple_of(kb * TK, TK)
                z = _dot_nt(qm[hd], k_ref[pl.ds(ks, TK), lanes])
                l1p = jnp.log(1.0 + jnp.exp(-jnp.abs(z)))
                log_keep = -jnp.maximum(z, 0.0) - l1p
                strict = None
                if step < n_diagonal:
                    strict = (col + (kb * TK - qi * TQ)) < row
                    log_keep = jnp.where(strict, log_keep, 0.0)
                stage1.append((hd, ks, strict, jnp.minimum(z, 0.0) - l1p, log_keep))
        cr = [cr_ref[0], cr_ref[1]]
        stage2 = []
        for hd, ks, strict, log_sig, log_keep in stage1:
            stage2.append((hd, ks, strict, log_sig, _split_dot(log_keep, u_ref[...]) + cr[hd]))
            cr[hd] = cr[hd] + jnp.sum(log_keep, axis=-1, keepdims=True)
        acc = [acc_ref[0], acc_ref[1]]
        for hd, ks, strict, log_sig, later in stage2:
            a = jnp.exp(log_sig + later)
            if strict is not None:
                a = jnp.where(strict, a, 0.0)
            acc[hd] = acc[hd] + _dot(a.astype(BF16), v_ref[pl.ds(ks, TK), lanes])
        for hd in range(2):
            cr_ref[hd] = cr[hd]
            acc_ref[hd] = acc[hd]

    def alive():
        return (jnp.max(cr_ref[...]) > SB_DEAD).astype(jnp.int32)

    last_diag = qi * per_q + per_q - 1

    @pl.when(qi == 0)
    def _():
        key_blocks(last_diag, per_q, per_q)

    @pl.when(qi > 0)
    def _():
        key_blocks(last_diag, 2 * per_q, per_q)

    def cond(carry):
        kb, live = carry
        return jnp.logical_and(kb >= 0, live > 0)

    def body(carry):
        kb, _ = carry
        key_blocks(kb, 1, 0)
        return kb - 1, alive()

    lax.while_loop(cond, body, (last_diag - 2 * per_q, alive()))
    o_ref[:, lanes] = jnp.where(lane < SB_HEAD_DIM, acc_ref[0], acc_ref[1]).astype(BF16)


def _sb_attention(q, k, v, bsz, seq):
    t = q.shape[0]
    nq = seq // TQ
    idx = jnp.arange(TK)
    upper = (idx[:, None] > idx[None, :]).astype(BF16)
    width = SB_PAIRS_PER_STEP * LANES
    kv = pl.BlockSpec((seq, width), lambda b, p, i: (b, p))
    return pl.pallas_call(
        _sb_kernel,
        grid=(bsz, SB_HEADS // (2 * SB_PAIRS_PER_STEP), nq),
        in_specs=[
            pl.BlockSpec((TQ, width), lambda b, p, i: (b * nq + i, p)),
            kv, kv,
            pl.BlockSpec((TK, TK), lambda b, p, i: (0, 0)),
        ],
        out_specs=pl.BlockSpec((TQ, width), lambda b, p, i: (b * nq + i, p)),
        out_shape=jax.ShapeDtypeStruct((t, SB_HEADS * SB_HEAD_DIM), BF16),
        scratch_shapes=[pltpu.VMEM((2, TQ, LANES), F32), pltpu.VMEM((2, TQ, 1), F32)],
        compiler_params=_cparams("arbitrary", "arbitrary", "arbitrary"),
        name="stick_breaking",
    )(q, k, v, upper)


def _first_of(vals, target):
    idx = jnp.full(target.shape, len(vals) - 1, jnp.int32)
    for j in range(len(vals) - 2, -1, -1):
        idx = jnp.where(vals[j] == target, j, idx)
    return idx


def _top2(vals):
    m1 = functools.reduce(jnp.maximum, vals)
    i1 = _first_of(vals, m1)
    rest = [jnp.where(i1 == j, -jnp.inf, v) for j, v in enumerate(vals)]
    m2 = functools.reduce(jnp.maximum, rest)
    i2 = _first_of(rest, m2)
    return m1, m2, i1, i2


def _pick(idx, vals):
    out = vals[-1]
    for j in range(len(vals) - 2, -1, -1):
        out = jnp.where(idx == j, vals[j], out)
    return out


def _out_router_kernel(expert_rows, x_ref, a_ref, b_ref, w_ref, g1_ref, ng_ref, sc_ref, sh_ref,
                       rw_ref, rb_ref, tri_ref,
                       x1_ref, hp_ref, pos_ref, wt_ref, cnt_ref, sched_ref, member):
    half = a_ref.shape[1]
    mix = _dot(a_ref[...], w_ref[0:half, :]) + _dot(b_ref[...], w_ref[half:, :])
    x1 = x_ref[...] + g1_ref[0] * mix
    x1_ref[...] = x1
    h = _modulated_rmsnorm(x1, ng_ref[...], sc_ref[0], sh_ref[0])
    hb = h.astype(BF16)
    _to_slabs(hp_ref, h)

    hl = (h - hb.astype(F32)).astype(BF16)
    both = _dot_nt(rw_ref[...], hb)
    logits = (both[0:N_EXPERTS, :] + both[N_EXPERTS:, :]
              + _dot_nt(rw_ref[0:N_EXPERTS, :], hl))
    ex = jnp.exp(logits - jnp.max(logits, axis=0, keepdims=True))
    scores = ex / jnp.sum(ex, axis=0, keepdims=True)
    sel = scores + rb_ref[...]
    srow = [scores[e:e + 1, :] for e in range(N_EXPERTS)]
    lrow = [sel[e:e + 1, :] for e in range(N_EXPERTS)]
    tops = [_top2(lrow[g * EXPERTS_PER_GROUP:(g + 1) * EXPERTS_PER_GROUP])
            for g in range(N_EXPERT_GROUPS)]
    gscore = [tp[0] + tp[1] for tp in tops]
    gidx = _first_of(gscore, functools.reduce(jnp.maximum, gscore))
    e1 = gidx * EXPERTS_PER_GROUP + _pick(gidx, [tp[2] for tp in tops])
    e2 = gidx * EXPERTS_PER_GROUP + _pick(gidx, [tp[3] for tp in tops])
    w1 = _pick(e1, srow)
    w2 = _pick(e2, srow)
    tot = w1 + w2
    wt_ref[0:1, :] = w1 / tot
    wt_ref[1:2, :] = w2 / tot

    @pl.when(pl.program_id(0) == 0)
    def _():
        cnt_ref[...] = jnp.zeros_like(cnt_ref)

    for e in range(N_EXPERTS):
        member[e:e + 1, :] = jnp.where(jnp.logical_or(e1 == e, e2 == e), 1.0, 0.0)
    m = member[...]
    incl = _dot(m.astype(BF16), tri_ref[...])
    excl = incl - m + cnt_ref[:, 0:1]
    cnt_ref[...] = cnt_ref[...] + incl[:, TM - 1:TM]
    erow = [excl[e:e + 1, :] for e in range(N_EXPERTS)]
    pos_ref[0:1, :] = (e1 * expert_rows + _pick(e1, erow).astype(jnp.int32)) * SUBLANES
    pos_ref[1:2, :] = (e2 * expert_rows + _pick(e2, erow).astype(jnp.int32)) * SUBLANES

    @pl.when(pl.program_id(0) == pl.num_programs(0) - 1)
    def _():
        _tile_schedule(cnt_ref[...], expert_rows, sched_ref)


def _tile_schedule(cnt, n_tok, sched_ref):
    blocks_per_expert = n_tok // TM_MOE
    tiles = jnp.floor((cnt + (TM_MOE - 1)) * (1.0 / TM_MOE))
    e_row = lax.broadcasted_iota(jnp.int32, (N_EXPERTS, N_EXPERTS), 0)
    e_col = lax.broadcasted_iota(jnp.int32, (N_EXPERTS, N_EXPERTS), 1)
    lower = jnp.where(e_col <= e_row, 1.0, 0.0).astype(BF16)
    ends = _dot(lower, tiles.astype(BF16))
    starts = ends - tiles
    e_id = lax.broadcasted_iota(jnp.int32, cnt.shape, 0).astype(F32)
    j = lax.broadcasted_iota(jnp.int32, cnt.shape, 1).astype(F32)
    total = ends[N_EXPERTS - 1:N_EXPERTS, :]
    e_j = jnp.minimum(jnp.sum(jnp.where(j >= ends, 1.0, 0.0), axis=0, keepdims=True),
                      float(N_EXPERTS - 1))
    mine = e_id == e_j
    start_j = jnp.sum(jnp.where(mine, starts, 0.0), axis=0, keepdims=True)
    cnt_j = jnp.sum(jnp.where(mine, cnt, 0.0), axis=0, keepdims=True)
    last_e = jnp.max(jnp.where(tiles > 0.0, e_id, 0.0), axis=0, keepdims=True)
    j_row = j[0:1, :]
    local = j_row - start_j
    used = j_row < total
    sched_ref[...] = jnp.zeros_like(sched_ref)
    sched_ref[0:1, :] = jnp.where(used, e_j, last_e).astype(jnp.int32)
    sched_ref[1:2, :] = jnp.where(used, e_j * blocks_per_expert + local,
                                  float(N_EXPERTS * blocks_per_expert)).astype(jnp.int32)
    sched_ref[2:3, :] = jnp.where(used, jnp.clip(cnt_j - local * TM_MOE, 0.0, float(TM_MOE)),
                                  0.0).astype(jnp.int32)


def _out_router(x2d, a, b, w_out, g1, ng, sc, sh, rw_stack, rb, seq):
    t, d = x2d.shape
    per_b = seq // TM
    vec = pl.BlockSpec((1, 1, d), lambda i: (i // per_b, 0, 0))
    full = lambda shape: pl.BlockSpec(shape, lambda i: (0, 0))
    half = a.shape[1]
    idx = jnp.arange(TM)
    tri = (idx[:, None] <= idx[None, :]).astype(BF16)
    assert d == SUBLANES * LANES and 2 * t // TM_MOE + N_EXPERTS <= LANES
    return pl.pallas_call(
        functools.partial(_out_router_kernel, t),
        grid=(t // TM,),
        in_specs=[
            pl.BlockSpec((TM, d), lambda i: (i, 0)),
            pl.BlockSpec((TM, half), lambda i: (i, 0)),
            pl.BlockSpec((TM, half), lambda i: (i, 0)),
            full((2 * half, d)),
            vec, full((1, d)), vec, vec,
            full((2 * N_EXPERTS, d)), full((N_EXPERTS, 1)), full((TM, TM)),
        ],
        out_specs=[
            pl.BlockSpec((TM, d), lambda i: (i, 0)),
            pl.BlockSpec((TM * SUBLANES, LANES), lambda i: (i, 0)),
            pl.BlockSpec((2, TM), lambda i: (0, i)),
            pl.BlockSpec((2, TM), lambda i: (0, i)),
            pl.BlockSpec((N_EXPERTS, LANES), lambda i: (0, 0)),
            pl.BlockSpec((SUBLANES, LANES), lambda i: (0, 0)),
        ],
        out_shape=[
            jax.ShapeDtypeStruct((t, d), F32),
            jax.ShapeDtypeStruct((t * SUBLANES, LANES), F32),
            jax.ShapeDtypeStruct((2, t), jnp.int32),
            jax.ShapeDtypeStruct((2, t), F32),
            jax.ShapeDtypeStruct((N_EXPERTS, LANES), F32),
            jax.ShapeDtypeStruct((SUBLANES, LANES), jnp.int32),
        ],
        scratch_shapes=[pltpu.VMEM((N_EXPERTS, TM), F32)],
        compiler_params=_cparams("arbitrary"),
        name="out_router",
    )(x2d, a, b, w_out, g1, ng, sc, sh, rw_stack, rb, tri)


def _to_slabs(ref, x):
    m = x.shape[0]
    for j in range(SUBLANES):
        ref[pl.ds(j, m, stride=SUBLANES), :] = x[:, j * LANES:(j + 1) * LANES]


def _from_slabs(ref):
    m = ref.shape[0] // SUBLANES
    return jnp.concatenate([ref[pl.ds(j, m, stride=SUBLANES), :] for j in range(SUBLANES)],
                           axis=-1)


def _slab(ref, offset):
    return ref.at[pl.ds(pl.multiple_of(offset, SUBLANES), SUBLANES), :]


def _dispatch_kernel(pos_ref, hp_ref, xs_ref, sem):
    n_tok = pos_ref.shape[0] // 2
    base = pl.program_id(0) * TM

    def issue(r, carry):
        for pick in range(2):
            pltpu.make_async_copy(_slab(hp_ref, r * SUBLANES),
                                  _slab(xs_ref, pos_ref[pick * n_tok + base + r]),
                                  sem).start(priority=pick)
        return carry

    lax.fori_loop(0, TM, issue, 0, unroll=8)
    for _ in range(2):
        pltpu.make_async_copy(hp_ref, xs_ref.at[pl.ds(0, TM * SUBLANES), :], sem).wait()


def _dispatch(pos_flat, hp, rows):
    t = hp.shape[0] // SUBLANES
    return pl.pallas_call(
        _dispatch_kernel,
        grid_spec=pltpu.PrefetchScalarGridSpec(
            num_scalar_prefetch=1,
            grid=(t // TM,),
            in_specs=[pl.BlockSpec((TM * SUBLANES, LANES), lambda i, pos: (i, 0))],
            out_specs=pl.BlockSpec(memory_space=pl.ANY),
            scratch_shapes=[pltpu.SemaphoreType.DMA],
        ),
        out_shape=jax.ShapeDtypeStruct((rows * SUBLANES, LANES), F32),
        compiler_params=_cparams("arbitrary"),
        name="moe_dispatch",
    )(pos_flat, hp)


def _expert_kernel(te_ref, tb_ref, tv_ref, xs_ref, wg_ref, wu_ref, wd_ref, ys_ref,
                   wg_bf, wu_bf, wd_bf):
    i = pl.program_id(0)
    expert = te_ref[i]
    valid = tv_ref[i]

    @pl.when(jnp.logical_or(i == 0, expert != te_ref[jnp.maximum(i - 1, 0)]))
    def _():
        wg_bf[...] = wg_ref[0].astype(BF16)
        wu_bf[...] = wu_ref[0].astype(BF16)
        wd_bf[...] = wd_ref[0].astype(BF16)

    @pl.when(valid > 0)
    def _():
        x = _from_slabs(xs_ref)
        row = lax.broadcasted_iota(jnp.int32, x.shape, 0)
        h = jnp.where(row < valid, x, 0.0).astype(BF16)
        gate = _dot(h, wg_bf[...])
        up = _dot(h, wu_bf[...])
        act = (gate * _sigmoid(gate) * up).astype(BF16)
        _to_slabs(ys_ref, _dot(act, wd_bf[...]))

    @pl.when(valid == 0)
    def _():
        ys_ref[...] = jnp.zeros_like(ys_ref)


def _experts(tile_expert, tile_block, tile_valid, xs, layer, wg, wu, wd):
    d = SUBLANES * LANES
    n_tiles = tile_expert.shape[0]
    wspec = lambda shape: pl.BlockSpec((None,) + shape, lambda i, te, tb, tv: (layer, te[i], 0, 0))
    rowspec = pl.BlockSpec((TM_MOE * SUBLANES, LANES), lambda i, te, tb, tv: (tb[i], 0))
    return pl.pallas_call(
        _expert_kernel,
        grid_spec=pltpu.PrefetchScalarGridSpec(
            num_scalar_prefetch=3,
            grid=(n_tiles,),
            in_specs=[
                rowspec,
                wspec((1, d, D_EXPERT)), wspec((1, d, D_EXPERT)), wspec((1, D_EXPERT, d)),
            ],
            out_specs=rowspec,
            scratch_shapes=[pltpu.VMEM((d, D_EXPERT), BF16), pltpu.VMEM((d, D_EXPERT), BF16),
                            pltpu.VMEM((D_EXPERT, d), BF16)],
        ),
        out_shape=jax.ShapeDtypeStruct(xs.shape, F32),
        compiler_params=_cparams("arbitrary"),
        name="moe_experts",
    )(tile_expert, tile_block, tile_valid, xs, wg, wu, wd)


def _combine_kernel(final, pos_ref, x1_ref, wt_ref, g2_ref, fn_ref, ys_ref, o_ref, buf, sems):
    n_tok = pos_ref.shape[0] // 2
    i = pl.program_id(0)
    n = pl.num_programs(0)

    def gather(tile, slot):
        base = tile * TM

        def issue(r, carry):
            for pick in range(2):
                pltpu.make_async_copy(_slab(ys_ref, pos_ref[pick * n_tok + base + r]),
                                      _slab(buf.at[slot, pick], r * SUBLANES),
                                      sems.at[slot]).start(priority=pick)
            return carry

        lax.fori_loop(0, TM, issue, 0, unroll=8)

    @pl.when(i == 0)
    def _():
        gather(0, 0)

    @pl.when(i + 1 < n)
    def _():
        gather(i + 1, (i + 1) % 2)

    slot = i % 2
    for pick in range(2):
        pltpu.make_async_copy(ys_ref.at[pl.ds(0, TM * SUBLANES), :], buf.at[slot, pick],
                              sems.at[slot]).wait()
    y = (wt_ref[:, 0:1] * _from_slabs(buf.at[slot, 0])
         + wt_ref[:, 1:2] * _from_slabs(buf.at[slot, 1]))
    x2 = x1_ref[...] + g2_ref[0] * y
    if final:
        ms = jnp.mean(x2 * x2, axis=-1, keepdims=True)
        x2 = x2 * lax.rsqrt(ms + RMS_EPS) * fn_ref[...]
    o_ref[...] = x2


def _combine(pos_flat, x1, wt, g2, fn, ys, seq, final):
    t, d = x1.shape
    per_b = seq // TM
    return pl.pallas_call(
        functools.partial(_combine_kernel, final),
        grid_spec=pltpu.PrefetchScalarGridSpec(
            num_scalar_prefetch=1,
            grid=(t // TM,),
            in_specs=[
                pl.BlockSpec((TM, d), lambda i, pos: (i, 0)),
                pl.BlockSpec((TM, 2), lambda i, pos: (i, 0)),
                pl.BlockSpec((1, 1, d), lambda i, pos: (i // per_b, 0, 0)),
                pl.BlockSpec((1, d), lambda i, pos: (0, 0)),
                pl.BlockSpec(memory_space=pl.ANY),
            ],
            out_specs=pl.BlockSpec((TM, d), lambda i, pos: (i, 0)),
            scratch_shapes=[pltpu.VMEM((2, 2, TM * SUBLANES, LANES), F32),
                            pltpu.SemaphoreType.DMA((2,))],
        ),
        out_shape=jax.ShapeDtypeStruct((t, d), F32),
        compiler_params=_cparams("arbitrary"),
        name="moe_combine_final" if final else "moe_combine",
    )(pos_flat, x1, wt, g2, fn, ys)


def _moe(x1, hp, pos, wt, sched, layer, wg, wu, wd, g2, fn, seq, final):
    t = x1.shape[0]
    n_tiles = 2 * t // TM_MOE + N_EXPERTS
    rows = N_EXPERTS * t + TM_MOE
    pos_flat = pos.reshape(-1)
    xs = _dispatch(pos_flat, hp, rows)
    ys = _experts(sched[0, :n_tiles], sched[1, :n_tiles], sched[2, :n_tiles], xs, layer,
                  wg, wu, wd)
    return _combine(pos_flat, x1, wt.T, g2, fn, ys, seq, final)


def _log_sigmoid(x):
    return jnp.minimum(x, 0.0) - jnp.log1p(jnp.exp(-jnp.abs(x)))


def _odd_in_kernel(x_ref, g_ref, sc_ref, sh_ref, w_ref, gw_ref, gb_ref,
                   q_ref, k_ref, v_ref, sr_ref, la_ref, up_ref):
    h = _modulated_rmsnorm(x_ref[...], g_ref[...], sc_ref[0], sh_ref[0]).astype(BF16)
    q_ref[...] = _dot(h, w_ref[:, 0:512]) * (GLA_DK ** -0.5)
    k_ref[...] = _dot(h, w_ref[:, 512:1024])
    v_ref[...] = _dot(h, w_ref[:, 1024:1536]).astype(BF16)
    r = _dot(h, w_ref[:, 1536:2048])
    sr_ref[...] = r * _sigmoid(r)
    up_ref[...] = _dot(h, w_ref[:, 2048:2560])
    a_low = _dot(h, w_ref[:, 2560:2688]).astype(BF16)
    la_ref[...] = _log_sigmoid(_dot(a_low, gw_ref[...]) + gb_ref[...]) * (1.0 / GLA_TAU)


def _odd_in(x2d, g, sc, sh, w_cat, gate_w, gate_b, seq):
    t, d = x2d.shape
    n = w_cat.shape[1]
    per_b = seq // TM
    vec = pl.BlockSpec((1, 1, d), lambda i: (i // per_b, 0, 0))
    full = lambda shape: pl.BlockSpec(shape, lambda i: (0, 0))
    out = lambda dt: jax.ShapeDtypeStruct((t, 512), dt)
    return pl.pallas_call(
        _odd_in_kernel,
        grid=(t // TM,),
        in_specs=[
            pl.BlockSpec((TM, d), lambda i: (i, 0)),
            full((1, d)), vec, vec, full((d, n)),
            full((LANES, 512)), full((1, 512)),
        ],
        out_specs=[pl.BlockSpec((TM, 512), lambda i: (i, 0))] * 6,
        out_shape=[out(F32), out(F32), out(BF16), out(F32), out(F32), out(F32)],
        compiler_params=_cparams("arbitrary"),
        name="odd_in",
    )(x2d, g, sc, sh, w_cat, gate_w, gate_b)


def _gla_kernel(q_ref, k_ref, la_ref, v_ref, sr_ref, g_ref, fwd_ref, rev_ref, o_ref, st_ref):
    group = GLA_CHUNK * GLA_UNROLL
    row = lax.broadcasted_iota(jnp.int32, (group, group), 0)
    col = lax.broadcasted_iota(jnp.int32, (group, group), 1)
    shift = GLA_CHUNK.bit_length() - 1
    keep = jnp.logical_and(jnp.right_shift(row, shift) == jnp.right_shift(col, shift), col <= row)

    @pl.when(pl.program_id(1) == 0)
    def _():
        st_ref[...] = jnp.zeros_like(st_ref)

    def chunk_group(n, carry):
        rows = pl.ds(pl.multiple_of(n * group, group), group)
        la = la_ref[rows, :]
        b_all = _split_dot_left(fwd_ref[...], la)
        r_all = _split_dot_left(rev_ref[...], la)
        for hd in range(GLA_HEADS):
            ls = slice(hd * LANES, (hd + 1) * LANES)
            b = b_all[:, ls]
            k = k_ref[rows, ls]
            v = v_ref[rows, ls]
            q_in = (q_ref[rows, ls] * jnp.exp(b)).astype(BF16)
            k_in = (k * jnp.exp(-b)).astype(BF16)
            k_dec = (k * jnp.exp(r_all[:, ls])).astype(BF16)
            scores = jnp.where(keep, _dot_nt(q_in, k_in), 0.0).astype(BF16)
            o_intra = _dot(scores, v)
            state_t = st_ref[hd]
            outs = []
            for c in range(GLA_UNROLL):
                cs = slice(c * GLA_CHUNK, (c + 1) * GLA_CHUNK)
                outs.append(o_intra[cs, :] + _dot_nt(q_in[cs, :], state_t.astype(BF16)))
                b_last = b[(c + 1) * GLA_CHUNK - 1:(c + 1) * GLA_CHUNK, :]
                state_t = state_t * jnp.exp(b_last) + _dot_tn(v[cs, :], k_dec[cs, :])
            st_ref[hd] = state_t
            o = jnp.concatenate(outs, axis=0)
            o = o * lax.rsqrt(jnp.mean(o * o, axis=-1, keepdims=True) + RMS_EPS)
            o_ref[rows, ls] = (o * g_ref[:, ls] * sr_ref[rows, ls]).astype(BF16)
        return carry

    lax.fori_loop(0, TS_GLA // group, chunk_group, 0)


def _gla(q, k, la, v, sr, norm_g, bsz, seq):
    t, width = q.shape
    per_b = seq // TS_GLA
    group = GLA_CHUNK * GLA_UNROLL
    idx = jnp.arange(group)
    same = (idx[:, None] // GLA_CHUNK) == (idx[None, :] // GLA_CHUNK)
    fwd = jnp.logical_and(same, idx[:, None] >= idx[None, :]).astype(BF16)
    rev = jnp.logical_and(same, idx[:, None] < idx[None, :]).astype(BF16)
    blk = pl.BlockSpec((TS_GLA, width), lambda b, i: (b * per_b + i, 0))
    tri_spec = pl.BlockSpec((group, group), lambda b, i: (0, 0))
    return pl.pallas_call(
        _gla_kernel,
        grid=(bsz, per_b),
        in_specs=[blk, blk, blk, blk, blk,
                  pl.BlockSpec((1, width), lambda b, i: (0, 0)),
                  tri_spec, tri_spec],
        out_specs=blk,
        out_shape=jax.ShapeDtypeStruct((t, width), BF16),
        scratch_shapes=[pltpu.VMEM((GLA_HEADS, GLA_DV, LANES), F32)],
        compiler_params=_cparams("arbitrary", "arbitrary"),
        name="gla",
    )(q, k, la, v, sr, norm_g.reshape(1, -1), fwd, rev)


def _pool_kernel(u_ref, halo_ref, w_ref, b_ref, s_ref, o_ref, buf):
    i = pl.program_id(1)
    buf[0:POOL_HALO, :] = jnp.where(i > 0, halo_ref[...], 0.0)
    buf[POOL_HALO:, :] = u_ref[...]
    pos = (i * TS_POOL + 1 + lax.broadcasted_iota(jnp.int32, (TS_POOL, 1), 0)).astype(F32)
    for gi, win in enumerate(POOL_WINDOWS):
        ls = slice(gi * LANES, (gi + 1) * LANES)
        tok = buf[POOL_HALO:, ls]
        tot = tok
        for back in range(1, win):
            tot = tot + buf[POOL_HALO - back:POOL_HALO - back + TS_POOL, ls]
        p = tot / jnp.minimum(pos, float(win)) - tok
        y = _dot(p.astype(BF16), w_ref[gi]) + b_ref[:, ls]
        o_ref[:, ls] = (y * s_ref[:, ls]).astype(BF16)


def _pool(u2d, pool_w, pool_b, pool_scale, bsz, seq):
    t = u2d.shape[0]
    per_b = seq // TS_POOL
    ratio = TS_POOL // POOL_HALO
    full = lambda shape: pl.BlockSpec(shape, lambda b, i: (0,) * len(shape))
    return pl.pallas_call(
        _pool_kernel,
        grid=(bsz, per_b),
        in_specs=[
            pl.BlockSpec((TS_POOL, D_POOL), lambda b, i: (b * per_b + i, 0)),
            pl.BlockSpec((POOL_HALO, D_POOL),
                         lambda b, i: (jnp.maximum((b * per_b + i) * ratio - 1, 0), 0)),
            full((len(POOL_WINDOWS), LANES, LANES)), full((1, D_POOL)), full((1, D_POOL)),
        ],
        out_specs=pl.BlockSpec((TS_POOL, D_POOL), lambda b, i: (b * per_b + i, 0)),
        out_shape=jax.ShapeDtypeStruct((t, D_POOL), BF16),
        scratch_shapes=[pltpu.VMEM((TS_POOL + POOL_HALO, D_POOL), F32)],
        compiler_params=_cparams("arbitrary", "arbitrary"),
        name="pool_mixer",
    )(u2d, u2d, pool_w.astype(BF16), pool_b.reshape(1, D_POOL), pool_scale.reshape(1, D_POOL))


def _pad_heads(w, heads, dim):
    lead = w.shape[:-1]
    w = w.reshape(lead + (heads, dim))
    w = jnp.pad(w, [(0, 0)] * len(lead) + [(0, 0), (0, LANES - dim)])
    return w.reshape(lead + (heads * LANES,))


def _odd_weights(w_in, gate_w, gate_b):
    dkt = GLA_HEADS * GLA_DK
    dvt = GLA_HEADS * GLA_DV
    o = [0, dkt, 2 * dkt, 2 * dkt + dvt, 2 * dkt + 2 * dvt, 2 * dkt + 2 * dvt + GLA_RANK]
    wq, wk, wv, wr, wa, wp = [w_in[:, o[j]:(o + [w_in.shape[1]])[j + 1]] for j in range(6)]
    w_cat = jnp.concatenate([
        _pad_heads(wq, GLA_HEADS, GLA_DK), _pad_heads(wk, GLA_HEADS, GLA_DK), wv, wr, wp,
        jnp.pad(wa, ((0, 0), (0, LANES - GLA_RANK)))], axis=1).astype(BF16)
    gw = jnp.pad(_pad_heads(gate_w, GLA_HEADS, GLA_DK), ((0, LANES - GLA_RANK), (0, 0))).astype(BF16)
    gb = _pad_heads(gate_b, GLA_HEADS, GLA_DK).reshape(1, -1)
    return w_cat, gw, gb


def kernel(x, c, ada_w, ada_b, norm_mix, norm_ffn, w_in_even, w_out_even, conv_w, conv_b, conv_norm_g, conv_norm_b, w_in_odd, w_out_odd, gla_gate_w, gla_gate_b, gla_norm_g, pool_w, pool_b, pool_scale, router_w, router_bias, moe_w_gate, moe_w_up, moe_w_down, final_norm):
    bsz, seq, d = x.shape
    t = bsz * seq
    depth = ada_w.shape[0]
    mod = _ada(c, ada_w, ada_b)
    rw_hi = router_w.T.astype(BF16)
    rw_lo = (router_w.T - rw_hi.astype(F32)).astype(BF16)
    rw_stack = jnp.concatenate([rw_hi, rw_lo], axis=0)
    rb =router_bias.reshape(N_EXPERTS, 1)
    fn = final_norm.reshape(1, d)
    x2d = x.reshape(t, d)
    for l in range(depth):
        sh1, sc1, g1, sh2, sc2, g2 = [mod[l, :, j * d:(j + 1) * d].reshape(bsz, 1, d)
                                      for j in range(6)]
        i = l // 2
        nm = norm_mix[l].reshape(1, d)
        if l % 2 == 0:
            y, q, k, v = _even_in(x2d, nm, sc1, sh1, w_in_even[i].astype(BF16), seq)
            a = _conv(y, conv_w[i], conv_b[i], conv_norm_g[i], conv_norm_b[i], bsz, seq)
            b = _sb_attention(q, k, v, bsz, seq)
            w_out = w_out_even[i]
        else:
            w_cat, gw, gb = _odd_weights(w_in_odd[i], gla_gate_w[i], gla_gate_b[i])
            q, k, v, sr, la, up = _odd_in(x2d, nm, sc1, sh1, w_cat, gw, gb, seq)
            a = _gla(q, k, la, v, sr, gla_norm_g[i], bsz, seq)
            b = _pool(up, pool_w[i], pool_b[i], pool_scale[i], bsz, seq)
            w_out = w_out_odd[i]
        x1, hp, pos, wt, _, sched = _out_router(x2d, a, b, w_out.astype(BF16), g1,
                                                norm_ffn[l].reshape(1, d), sc2, sh2,
                                                rw_stack, rb, seq)
        x2d = _moe(x1, hp, pos, wt, sched, l, moe_w_gate, moe_w_up, moe_w_down, g2, fn, seq,
                   final=(l == depth - 1))
    return x2d.reshape(bsz, seq, d)
```

```python
import functools

import jax
import jax.numpy as jnp
from jax import lax
from jax.experimental import pallas as pl
from jax.experimental.pallas import tpu as pltpu

F32 = jnp.float32
BF16 = jnp.bfloat16

D_MODEL = 1024
D_CONV = 512
CONV_WIDTH = 31
CONV_GROUP_SIZE = 64
SB_HEADS = 8
SB_HEAD_DIM = 64
GLA_HEADS = 4
GLA_DK = 64
GLA_DV = 128
GLA_RANK = 16
GLA_TAU = 16.0
GLA_CHUNK = 64
D_POOL = 512
POOL_WINDOWS = (2, 4, 8, 16)
N_EXPERTS = 16
N_EXPERT_GROUPS = 4
EXPERTS_PER_GROUP = 4
D_EXPERT = 512
RMS_EPS = 1e-6
LN_EPS = 1e-5

LANES = 128
SUBLANES = 8
VMEM_LIMIT = 56 * 1024 * 1024

TM = 512
TS_CONV = 256
CONV_HALO = 32
CONV_ROWS = 64
TS_GLA = 1024
GLA_UNROLL = 4
TS_POOL = 512
POOL_HALO = 16
TQ = 256
TK = 256
SB_PAIRS_PER_STEP = 2
TM_MOE = 512
SB_DEAD = -120.0


def _cparams(*sem):
    return pltpu.CompilerParams(dimension_semantics=sem, vmem_limit_bytes=VMEM_LIMIT)


def _sigmoid(x):
    return 1.0 / (1.0 + jnp.exp(-x))


def _dot(a, b):
    return jnp.dot(a, b, preferred_element_type=F32)


def _dot_nt(a, b, **kw):
    return lax.dot_general(a, b, (((1,), (1,)), ((), ())), preferred_element_type=F32, **kw)


def _dot_tn(a, b):
    return lax.dot_general(a, b, (((0,), (0,)), ((), ())), preferred_element_type=F32)


def _split_dot(x, m):
    hi = x.astype(BF16)
    lo = (x - hi.astype(F32)).astype(BF16)
    return _dot(hi, m) + _dot(lo, m)


def _split_dot_left(m, x):
    hi = x.astype(BF16)
    lo = (x - hi.astype(F32)).astype(BF16)
    return _dot(m, hi) + _dot(m, lo)


def _modulated_rmsnorm(x, g, sc, sh):
    ms = jnp.mean(x * x, axis=-1, keepdims=True)
    return (x * lax.rsqrt(ms + RMS_EPS) * g) * (1.0 + sc) + sh


def _ada_kernel(c_ref, w_ref, b_ref, o_ref):
    c = c_ref[...]
    cond = c * _sigmoid(c)
    o_ref[0] = jnp.dot(cond, w_ref[0], preferred_element_type=F32,
                       precision=lax.Precision.HIGHEST) + b_ref[0]


def _ada(c, ada_w, ada_b):
    depth, d, n = ada_w.shape
    bsz = c.shape[0]
    tn = 1536
    return pl.pallas_call(
        _ada_kernel,
        grid=(depth, n // tn),
        in_specs=[
            pl.BlockSpec((bsz, d), lambda l, j: (0, 0)),
            pl.BlockSpec((1, d, tn), lambda l, j: (l, 0, j)),
            pl.BlockSpec((1, 1, tn), lambda l, j: (l, 0, j)),
        ],
        out_specs=pl.BlockSpec((1, bsz, tn), lambda l, j: (l, 0, j)),
        out_shape=jax.ShapeDtypeStruct((depth, bsz, n), F32),
        compiler_params=_cparams("arbitrary", "arbitrary"),
        name="ada_mod",
    )(c, ada_w, ada_b.reshape(depth, 1, n))


def _even_in_kernel(x_ref, g_ref, sc_ref, sh_ref, w_ref, y_ref, q_ref, k_ref, v_ref):
    h = _modulated_rmsnorm(x_ref[...], g_ref[...], sc_ref[0], sh_ref[0]).astype(BF16)
    val = _dot(h, w_ref[:, 0:512])
    gate = _dot(h, w_ref[:, 512:1024])
    y_ref[...] = val * _sigmoid(gate)
    q_ref[...] = (_dot(h, w_ref[:, 1024:1536]) * (SB_HEAD_DIM ** -0.5)).astype(BF16)
    k_ref[...] = _dot(h, w_ref[:, 1536:2048]).astype(BF16)
    v_ref[...] = _dot(h, w_ref[:, 2048:2560]).astype(BF16)


def _even_in(x2d, g, sc, sh, w_in, seq):
    t, d = x2d.shape
    n = w_in.shape[1]
    per_b = seq // TM
    vec = pl.BlockSpec((1, 1, d), lambda i: (i // per_b, 0, 0))
    out = lambda dt: jax.ShapeDtypeStruct((t, 512), dt)
    return pl.pallas_call(
        _even_in_kernel,
        grid=(t // TM,),
        in_specs=[
            pl.BlockSpec((TM, d), lambda i: (i, 0)),
            pl.BlockSpec((1, d), lambda i: (0, 0)),
            vec, vec,
            pl.BlockSpec((d, n), lambda i: (0, 0)),
        ],
        out_specs=[pl.BlockSpec((TM, 512), lambda i: (i, 0))] * 4,
        out_shape=[out(F32), out(BF16), out(BF16), out(BF16)],
        compiler_params=_cparams("arbitrary"),
        name="even_in",
    )(x2d, g, sc, sh, w_in)


def _conv_kernel(y_ref, halo_ref, w_ref, b_ref, ng_ref, nb_ref, avg_ref, o_ref, buf, sh, cv):
    i = pl.program_id(1)
    buf[0:CONV_HALO, :] = jnp.where(i > 0, halo_ref[...], 0.0)
    buf[CONV_HALO:, :] = y_ref[...]
    sh_rows = sh.shape[1]
    for r in range(1, SUBLANES):
        sh[r - 1] = buf[r:r + sh_rows, :]
    first = CONV_HALO - (CONV_WIDTH - 1)
    for rc in range(TS_CONV // CONV_ROWS):
        r0 = rc * CONV_ROWS
        for lc in range(D_CONV // LANES):
            ls = slice(lc * LANES, (lc + 1) * LANES)
            acc = jnp.zeros((CONV_ROWS, LANES), F32)
            for j in range(CONV_WIDTH):
                whole, r = divmod(first + j, SUBLANES)
                s = r0 + whole * SUBLANES
                src = buf[s:s + CONV_ROWS, ls] if r == 0 else sh[r - 1, s:s + CONV_ROWS, ls]
                acc = acc + w_ref[j:j + 1, ls] * src
            cv[r0:r0 + CONV_ROWS, ls] = acc + b_ref[:, ls]
    y = cv[...]
    avg = avg_ref[...]
    d = y - _split_dot(y, avg)
    var = _split_dot(d * d, avg)
    yn = d * lax.rsqrt(var + LN_EPS) * ng_ref[...] + nb_ref[...]
    o_ref[...] = (yn * _sigmoid(yn)).astype(BF16)


def _conv(y2d, conv_w, conv_b, norm_g, norm_b, bsz, seq):
    t = y2d.shape[0]
    per_b = seq // TS_CONV
    ratio = TS_CONV // CONV_HALO
    gid = jnp.arange(D_CONV) // CONV_GROUP_SIZE
    avg = jnp.where(gid[:, None] == gid[None, :], 1.0 / CONV_GROUP_SIZE, 0.0).astype(BF16)
    w_pad = jnp.pad(conv_w, ((0, 32 - CONV_WIDTH), (0, 0)))
    row = lambda a: a.reshape(1, D_CONV)
    full = lambda shape: pl.BlockSpec(shape, lambda b, i: (0, 0))
    return pl.pallas_call(
        _conv_kernel,
        grid=(bsz, per_b),
        in_specs=[
            pl.BlockSpec((TS_CONV, D_CONV), lambda b, i: (b * per_b + i, 0)),
            pl.BlockSpec((CONV_HALO, D_CONV),
                         lambda b, i: (jnp.maximum((b * per_b + i) * ratio - 1, 0), 0)),
            full((32, D_CONV)), full((1, D_CONV)), full((1, D_CONV)), full((1, D_CONV)),
            full((D_CONV, D_CONV)),
        ],
        out_specs=pl.BlockSpec((TS_CONV, D_CONV), lambda b, i: (b * per_b + i, 0)),
        out_shape=jax.ShapeDtypeStruct((t, D_CONV), BF16),
        scratch_shapes=[pltpu.VMEM((TS_CONV + CONV_HALO, D_CONV), F32),
                        pltpu.VMEM((SUBLANES - 1, TS_CONV + CONV_HALO - SUBLANES, D_CONV), F32),
                        pltpu.VMEM((TS_CONV, D_CONV), F32)],
        compiler_params=_cparams("arbitrary", "arbitrary"),
        name="conv_module",
    )(y2d, y2d, w_pad, row(conv_b), row(norm_g), row(norm_b), avg)


def _sb_kernel(q_ref, k_ref, v_ref, u_ref, o_ref, acc_ref, cr_ref):
    for pair in range(q_ref.shape[1] // LANES):
        _sb_head_pair(slice(pair * LANES, (pair + 1) * LANES),
                      q_ref, k_ref, v_ref, u_ref, o_ref, acc_ref, cr_ref)


def _sb_head_pair(lanes, q_ref, k_ref, v_ref, u_ref, o_ref, acc_ref, cr_ref):
    qi = pl.program_id(2)
    q = q_ref[:, lanes]
    lane = lax.broadcasted_iota(jnp.int32, (TQ, LANES), 1)
    row = lax.broadcasted_iota(jnp.int32, (TQ, TK), 0)
    col = lax.broadcasted_iota(jnp.int32, (TQ, TK), 1)
    qm = [jnp.where(lane < SB_HEAD_DIM, q, jnp.zeros_like(q)),
          jnp.where(lane >= SB_HEAD_DIM, q, jnp.zeros_like(q))]
    acc_ref[...] = jnp.zeros_like(acc_ref)
    cr_ref[...] = jnp.zeros_like(cr_ref)
    per_q = TQ // TK

    def key_blocks(first_kb, count, n_diagonal):
        stage1 = []
        for hd in range(2):
            for step in range(count):
                kb = first_kb - step
                ks = pl.multiple_of(kb * TK, TK)
                z = _dot_nt(qm[hd], k_ref[pl.ds(ks, TK), lanes])
                l1p = jnp.log(1.0 + jnp.exp(-jnp.abs(z)))
                log_keep = -jnp.maximum(z, 0.0) - l1p
                strict = None
                if step < n_diagonal:
                    strict = (col + (kb * TK - qi * TQ)) < row
                    log_keep = jnp.where(strict, log_keep, 0.0)
                stage1.append((hd, ks, strict, jnp.minimum(z, 0.0) - l1p, log_keep))
        cr = [cr_ref[0], cr_ref[1]]
        stage2 = []
        for hd, ks, strict, log_sig, log_keep in stage1:
            stage2.append((hd, ks, strict, log_sig, _split_dot(log_keep, u_ref[...]) + cr[hd]))
            cr[hd] = cr[hd] + jnp.sum(log_keep, axis=-1, keepdims=True)
        acc = [acc_ref[0], acc_ref[1]]
        for hd, ks, strict, log_sig, later in stage2:
            a = jnp.exp(log_sig + later)
            if strict is not None:
                a = jnp.where(strict, a, 0.0)
            acc[hd] = acc[hd] + _dot(a.astype(BF16), v_ref[pl.ds(ks, TK), lanes])
        for hd in range(2):
            cr_ref[hd] = cr[hd]
            acc_ref[hd] = acc[hd]

    def alive():
        return (jnp.max(cr_ref[...]) > SB_DEAD).astype(jnp.int32)

    last_diag = qi * per_q + per_q - 1

    @pl.when(qi == 0)
    def _():
        key_blocks(last_diag, per_q, per_q)

    @pl.when(qi > 0)
    def _():
        key_blocks(last_diag, 2 * per_q, per_q)

    def cond(carry):
        kb, live = carry
        return jnp.logical_and(kb >= 0, live > 0)

    def body(carry):
        kb, _ = carry
        key_blocks(kb, 1, 0)
        return kb - 1, alive()

    lax.while_loop(cond, body, (last_diag - 2 * per_q, alive()))
    o_ref[:, lanes] = jnp.where(lane < SB_HEAD_DIM, acc_ref[0], acc_ref[1]).astype(BF16)


def _sb_attention(q, k, v, bsz, seq):
    t = q.shape[0]
    nq = seq // TQ
    idx = jnp.arange(TK)
    upper = (idx[:, None] > idx[None, :]).astype(BF16)
    width = SB_PAIRS_PER_STEP * LANES
    kv = pl.BlockSpec((seq, width), lambda b, p, i: (b, p))
    return pl.pallas_call(
        _sb_kernel,
        grid=(bsz, SB_HEADS // (2 * SB_PAIRS_PER_STEP), nq),
        in_specs=[
            pl.BlockSpec((TQ, width), lambda b, p, i: (b * nq + i, p)),
            kv, kv,
            pl.BlockSpec((TK, TK), lambda b, p, i: (0, 0)),
        ],
        out_specs=pl.BlockSpec((TQ, width), lambda b, p, i: (b * nq + i, p)),
        out_shape=jax.ShapeDtypeStruct((t, SB_HEADS * SB_HEAD_DIM), BF16),
        scratch_shapes=[pltpu.VMEM((2, TQ, LANES), F32), pltpu.VMEM((2, TQ, 1), F32)],
        compiler_params=_cparams("arbitrary", "arbitrary", "arbitrary"),
        name="stick_breaking",
    )(q, k, v, upper)


def _first_of(vals, target):
    idx = jnp.full(target.shape, len(vals) - 1, jnp.int32)
    for j in range(len(vals) - 2, -1, -1):
        idx = jnp.where(vals[j] == target, j, idx)
    return idx


def _top2(vals):
    m1 = functools.reduce(jnp.maximum, vals)
    i1 = _first_of(vals, m1)
    rest = [jnp.where(i1 == j, -jnp.inf, v) for j, v in enumerate(vals)]
    m2 = functools.reduce(jnp.maximum, rest)
    i2 = _first_of(rest, m2)
    return m1, m2, i1, i2


def _pick(idx, vals):
    out = vals[-1]
    for j in range(len(vals) - 2, -1, -1):
        out = jnp.where(idx == j, vals[j], out)
    return out


def _out_router_kernel(expert_rows, x_ref, a_ref, b_ref, w_ref, g1_ref, ng_ref, sc_ref, sh_ref,
                       rw_ref, rb_ref, tri_ref,
                       x1_ref, hp_ref, pos_ref, wt_ref, cnt_ref, sched_ref, member):
    half = a_ref.shape[1]
    mix = _dot(a_ref[...], w_ref[0:half, :]) + _dot(b_ref[...], w_ref[half:, :])
    x1 = x_ref[...] + g1_ref[0] * mix
    x1_ref[...] = x1
    h = _modulated_rmsnorm(x1, ng_ref[...], sc_ref[0], sh_ref[0])
    hb = h.astype(BF16)
    _to_slabs(hp_ref, h)

    hl = (h - hb.astype(F32)).astype(BF16)
    both = _dot_nt(rw_ref[...], hb)
    logits = (both[0:N_EXPERTS, :] + both[N_EXPERTS:, :]
              + _dot_nt(rw_ref[0:N_EXPERTS, :], hl))
    ex = jnp.exp(logits - jnp.max(logits, axis=0, keepdims=True))
    scores = ex / jnp.sum(ex, axis=0, keepdims=True)
    sel = scores + rb_ref[...]
    srow = [scores[e:e + 1, :] for e in range(N_EXPERTS)]
    lrow = [sel[e:e + 1, :] for e in range(N_EXPERTS)]
    tops = [_top2(lrow[g * EXPERTS_PER_GROUP:(g + 1) * EXPERTS_PER_GROUP])
            for g in range(N_EXPERT_GROUPS)]
    gscore = [tp[0] + tp[1] for tp in tops]
    gidx = _first_of(gscore, functools.reduce(jnp.maximum, gscore))
    e1 = gidx * EXPERTS_PER_GROUP + _pick(gidx, [tp[2] for tp in tops])
    e2 = gidx * EXPERTS_PER_GROUP + _pick(gidx, [tp[3] for tp in tops])
    w1 = _pick(e1, srow)
    w2 = _pick(e2, srow)
    tot = w1 + w2
    wt_ref[0:1, :] = w1 / tot
    wt_ref[1:2, :] = w2 / tot

    @pl.when(pl.program_id(0) == 0)
    def _():
        cnt_ref[...] = jnp.zeros_like(cnt_ref)

    for e in range(N_EXPERTS):
        member[e:e + 1, :] = jnp.where(jnp.logical_or(e1 == e, e2 == e), 1.0, 0.0)
    m = member[...]
    incl = _dot(m.astype(BF16), tri_ref[...])
    excl = incl - m + cnt_ref[:, 0:1]
    cnt_ref[...] = cnt_ref[...] + incl[:, TM - 1:TM]
    erow = [excl[e:e + 1, :] for e in range(N_EXPERTS)]
    pos_ref[0:1, :] = (e1 * expert_rows + _pick(e1, erow).astype(jnp.int32)) * SUBLANES
    pos_ref[1:2, :] = (e2 * expert_rows + _pick(e2, erow).astype(jnp.int32)) * SUBLANES

    @pl.when(pl.program_id(0) == pl.num_programs(0) - 1)
    def _():
        _tile_schedule(cnt_ref[...], expert_rows, sched_ref)


def _tile_schedule(cnt, n_tok, sched_ref):
    blocks_per_expert = n_tok // TM_MOE
    tiles = jnp.floor((cnt + (TM_MOE - 1)) * (1.0 / TM_MOE))
    e_row = lax.broadcasted_iota(jnp.int32, (N_EXPERTS, N_EXPERTS), 0)
    e_col = lax.broadcasted_iota(jnp.int32, (N_EXPERTS, N_EXPERTS), 1)
    lower = jnp.where(e_col <= e_row, 1.0, 0.0).astype(BF16)
    ends = _dot(lower, tiles.astype(BF16))
    starts = ends - tiles
    e_id = lax.broadcasted_iota(jnp.int32, cnt.shape, 0).astype(F32)
    j = lax.broadcasted_iota(jnp.int32, cnt.shape, 1).astype(F32)
    total = ends[N_EXPERTS - 1:N_EXPERTS, :]
    e_j = jnp.minimum(jnp.sum(jnp.where(j >= ends, 1.0, 0.0), axis=0, keepdims=True),
                      float(N_EXPERTS - 1))
    mine = e_id == e_j
    start_j = jnp.sum(jnp.where(mine, starts, 0.0), axis=0, keepdims=True)
    cnt_j = jnp.sum(jnp.where(mine, cnt, 0.0), axis=0, keepdims=True)
    last_e = jnp.max(jnp.where(tiles > 0.0, e_id, 0.0), axis=0, keepdims=True)
    j_row = j[0:1, :]
    local = j_row - start_j
    used = j_row < total
    sched_ref[...] = jnp.zeros_like(sched_ref)
    sched_ref[0:1, :] = jnp.where(used, e_j, last_e).astype(jnp.int32)
    sched_ref[1:2, :] = jnp.where(used, e_j * blocks_per_expert + local,
                                  float(N_EXPERTS * blocks_per_expert)).astype(jnp.int32)
    sched_ref[2:3, :] = jnp.where(used, jnp.clip(cnt_j - local * TM_MOE, 0.0, float(TM_MOE)),
                                  0.0).astype(jnp.int32)


def _out_router(x2d, a, b, w_out, g1, ng, sc, sh, rw_stack, rb, seq):
    t, d = x2d.shape
    per_b = seq // TM
    vec = pl.BlockSpec((1, 1, d), lambda i: (i // per_b, 0, 0))
    full = lambda shape: pl.BlockSpec(shape, lambda i: (0, 0))
    half = a.shape[1]
    idx = jnp.arange(TM)
    tri = (idx[:, None] <= idx[None, :]).astype(BF16)
    assert d == SUBLANES * LANES and 2 * t // TM_MOE + N_EXPERTS <= LANES
    return pl.pallas_call(
        functools.partial(_out_router_kernel, t),
        grid=(t // TM,),
        in_specs=[
            pl.BlockSpec((TM, d), lambda i: (i, 0)),
            pl.BlockSpec((TM, half), lambda i: (i, 0)),
            pl.BlockSpec((TM, half), lambda i: (i, 0)),
            full((2 * half, d)),
            vec, full((1, d)), vec, vec,
            full((2 * N_EXPERTS, d)), full((N_EXPERTS, 1)), full((TM, TM)),
        ],
        out_specs=[
            pl.BlockSpec((TM, d), lambda i: (i, 0)),
            pl.BlockSpec((TM * SUBLANES, LANES), lambda i: (i, 0)),
            pl.BlockSpec((2, TM), lambda i: (0, i)),
            pl.BlockSpec((2, TM), lambda i: (0, i)),
            pl.BlockSpec((N_EXPERTS, LANES), lambda i: (0, 0)),
            pl.BlockSpec((SUBLANES, LANES), lambda i: (0, 0)),
        ],
        out_shape=[
            jax.ShapeDtypeStruct((t, d), F32),
            jax.ShapeDtypeStruct((t * SUBLANES, LANES), F32),
            jax.ShapeDtypeStruct((2, t), jnp.int32),
            jax.ShapeDtypeStruct((2, t), F32),
            jax.ShapeDtypeStruct((N_EXPERTS, LANES), F32),
            jax.ShapeDtypeStruct((SUBLANES, LANES), jnp.int32),
        ],
        scratch_shapes=[pltpu.VMEM((N_EXPERTS, TM), F32)],
        compiler_params=_cparams("arbitrary"),
        name="out_router",
    )(x2d, a, b, w_out, g1, ng, sc, sh, rw_stack, rb, tri)


def _to_slabs(ref, x):
    m = x.shape[0]
    for j in range(SUBLANES):
        ref[pl.ds(j, m, stride=SUBLANES), :] = x[:, j * LANES:(j + 1) * LANES]


def _from_slabs(ref):
    m = ref.shape[0] // SUBLANES
    return jnp.concatenate([ref[pl.ds(j, m, stride=SUBLANES), :] for j in range(SUBLANES)],
                           axis=-1)


def _slab(ref, offset):
    return ref.at[pl.ds(pl.multiple_of(offset, SUBLANES), SUBLANES), :]


def _dispatch_kernel(pos_ref, hp_ref, xs_ref, sem):
    n_tok = pos_ref.shape[0] // 2
    base = pl.program_id(0) * TM

    def issue(r, carry):
        for pick in range(2):
            pltpu.make_async_copy(_slab(hp_ref, r * SUBLANES),
                                  _slab(xs_ref, pos_ref[pick * n_tok + base + r]),
                                  sem).start(priority=pick)
        return carry

    lax.fori_loop(0, TM, issue, 0, unroll=8)
    for _ in range(2):
        pltpu.make_async_copy(hp_ref, xs_ref.at[pl.ds(0, TM * SUBLANES), :], sem).wait()


def _dispatch(pos_flat, hp, rows):
    t = hp.shape[0] // SUBLANES
    return pl.pallas_call(
        _dispatch_kernel,
        grid_spec=pltpu.PrefetchScalarGridSpec(
            num_scalar_prefetch=1,
            grid=(t // TM,),
            in_specs=[pl.BlockSpec((TM * SUBLANES, LANES), lambda i, pos: (i, 0))],
            out_specs=pl.BlockSpec(memory_space=pl.ANY),
            scratch_shapes=[pltpu.SemaphoreType.DMA],
        ),
        out_shape=jax.ShapeDtypeStruct((rows * SUBLANES, LANES), F32),
        compiler_params=_cparams("arbitrary"),
        name="moe_dispatch",
    )(pos_flat, hp)


def _expert_kernel(te_ref, tb_ref, tv_ref, xs_ref, wg_ref, wu_ref, wd_ref, ys_ref,
                   wg_bf, wu_bf, wd_bf):
    i = pl.program_id(0)
    expert = te_ref[i]
    valid = tv_ref[i]

    @pl.when(jnp.logical_or(i == 0, expert != te_ref[jnp.maximum(i - 1, 0)]))
    def _():
        wg_bf[...] = wg_ref[0].astype(BF16)
        wu_bf[...] = wu_ref[0].astype(BF16)
        wd_bf[...] = wd_ref[0].astype(BF16)

    @pl.when(valid > 0)
    def _():
        x = _from_slabs(xs_ref)
        row = lax.broadcasted_iota(jnp.int32, x.shape, 0)
        h = jnp.where(row < valid, x, 0.0).astype(BF16)
        gate = _dot(h, wg_bf[...])
        up = _dot(h, wu_bf[...])
        act = (gate * _sigmoid(gate) * up).astype(BF16)
        _to_slabs(ys_ref, _dot(act, wd_bf[...]))

    @pl.when(valid == 0)
    def _():
        ys_ref[...] = jnp.zeros_like(ys_ref)


def _experts(tile_expert, tile_block, tile_valid, xs, layer, wg, wu, wd):
    d = SUBLANES * LANES
    n_tiles = tile_expert.shape[0]
    wspec = lambda shape: pl.BlockSpec((None,) + shape, lambda i, te, tb, tv: (layer, te[i], 0, 0))
    rowspec = pl.BlockSpec((TM_MOE * SUBLANES, LANES), lambda i, te, tb, tv: (tb[i], 0))
    return pl.pallas_call(
        _expert_kernel,
        grid_spec=pltpu.PrefetchScalarGridSpec(
            num_scalar_prefetch=3,
            grid=(n_tiles,),
            in_specs=[
                rowspec,
                wspec((1, d, D_EXPERT)), wspec((1, d, D_EXPERT)), wspec((1, D_EXPERT, d)),
            ],
            out_specs=rowspec,
            scratch_shapes=[pltpu.VMEM((d, D_EXPERT), BF16), pltpu.VMEM((d, D_EXPERT), BF16),
                            pltpu.VMEM((D_EXPERT, d), BF16)],
        ),
        out_shape=jax.ShapeDtypeStruct(xs.shape, F32),
        compiler_params=_cparams("arbitrary"),
        name="moe_experts",
    )(tile_expert, tile_block, tile_valid, xs, wg, wu, wd)


def _combine_kernel(final, pos_ref, x1_ref, wt_ref, g2_ref, fn_ref, ys_ref, o_ref, buf, sems):
    n_tok = pos_ref.shape[0] // 2
    i = pl.program_id(0)
    n = pl.num_programs(0)

    def gather(tile, slot):
        base = tile * TM

        def issue(r, carry):
            for pick in range(2):
                pltpu.make_async_copy(_slab(ys_ref, pos_ref[pick * n_tok + base + r]),
                                      _slab(buf.at[slot, pick], r * SUBLANES),
                                      sems.at[slot]).start(priority=pick)
            return carry

        lax.fori_loop(0, TM, issue, 0, unroll=8)

    @pl.when(i == 0)
    def _():
        gather(0, 0)

    @pl.when(i + 1 < n)
    def _():
        gather(i + 1, (i + 1) % 2)

    slot = i % 2
    for pick in range(2):
        pltpu.make_async_copy(ys_ref.at[pl.ds(0, TM * SUBLANES), :], buf.at[slot, pick],
                              sems.at[slot]).wait()
    y = (wt_ref[:, 0:1] * _from_slabs(buf.at[slot, 0])
         + wt_ref[:, 1:2] * _from_slabs(buf.at[slot, 1]))
    x2 = x1_ref[...] + g2_ref[0] * y
    if final:
        ms = jnp.mean(x2 * x2, axis=-1, keepdims=True)
        x2 = x2 * lax.rsqrt(ms + RMS_EPS) * fn_ref[...]
    o_ref[...] = x2


def _combine(pos_flat, x1, wt, g2, fn, ys, seq, final):
    t, d = x1.shape
    per_b = seq // TM
    return pl.pallas_call(
        functools.partial(_combine_kernel, final),
        grid_spec=pltpu.PrefetchScalarGridSpec(
            num_scalar_prefetch=1,
            grid=(t // TM,),
            in_specs=[
                pl.BlockSpec((TM, d), lambda i, pos: (i, 0)),
                pl.BlockSpec((TM, 2), lambda i, pos: (i, 0)),
                pl.BlockSpec((1, 1, d), lambda i, pos: (i // per_b, 0, 0)),
                pl.BlockSpec((1, d), lambda i, pos: (0, 0)),
                pl.BlockSpec(memory_space=pl.ANY),
            ],
            out_specs=pl.BlockSpec((TM, d), lambda i, pos: (i, 0)),
            scratch_shapes=[pltpu.VMEM((2, 2, TM * SUBLANES, LANES), F32),
                            pltpu.SemaphoreType.DMA((2,))],
        ),
        out_shape=jax.ShapeDtypeStruct((t, d), F32),
        compiler_params=_cparams("arbitrary"),
        name="moe_combine_final" if final else "moe_combine",
    )(pos_flat, x1, wt, g2, fn, ys)


def _moe(x1, hp, pos, wt, sched, layer, wg, wu, wd, g2, fn, seq, final):
    t = x1.shape[0]
    n_tiles = 2 * t // TM_MOE + N_EXPERTS
    rows = N_EXPERTS * t + TM_MOE
    pos_flat = pos.reshape(-1)
    xs = _dispatch(pos_flat, hp, rows)
    ys = _experts(sched[0, :n_tiles], sched[1, :n_tiles], sched[2, :n_tiles], xs, layer,
                  wg, wu, wd)
    return _combine(pos_flat, x1, wt.T, g2, fn, ys, seq, final)


def _log_sigmoid(x):
    return jnp.minimum(x, 0.0) - jnp.log1p(jnp.exp(-jnp.abs(x)))


def _odd_in_kernel(x_ref, g_ref, sc_ref, sh_ref, w_ref, gw_ref, gb_ref,
                   q_ref, k_ref, v_ref, sr_ref, la_ref, up_ref):
    h = _modulated_rmsnorm(x_ref[...], g_ref[...], sc_ref[0], sh_ref[0]).astype(BF16)
    q_ref[...] = _dot(h, w_ref[:, 0:512]) * (GLA_DK ** -0.5)
    k_ref[...] = _dot(h, w_ref[:, 512:1024])
    v_ref[...] = _dot(h, w_ref[:, 1024:1536]).astype(BF16)
    r = _dot(h, w_ref[:, 1536:2048])
    sr_ref[...] = r * _sigmoid(r)
    up_ref[...] = _dot(h, w_ref[:, 2048:2560])
    a_low = _dot(h, w_ref[:, 2560:2688]).astype(BF16)
    la_ref[...] = _log_sigmoid(_dot(a_low, gw_ref[...]) + gb_ref[...]) * (1.0 / GLA_TAU)


def _odd_in(x2d, g, sc, sh, w_cat, gate_w, gate_b, seq):
    t, d = x2d.shape
    n = w_cat.shape[1]
    per_b = seq // TM
    vec = pl.BlockSpec((1, 1, d), lambda i: (i // per_b, 0, 0))
    full = lambda shape: pl.BlockSpec(shape, lambda i: (0, 0))
    out = lambda dt: jax.ShapeDtypeStruct((t, 512), dt)
    return pl.pallas_call(
        _odd_in_kernel,
        grid=(t // TM,),
        in_specs=[
            pl.BlockSpec((TM, d), lambda i: (i, 0)),
            full((1, d)), vec, vec, full((d, n)),
            full((LANES, 512)), full((1, 512)),
        ],
        out_specs=[pl.BlockSpec((TM, 512), lambda i: (i, 0))] * 6,
        out_shape=[out(F32), out(F32), out(BF16), out(F32), out(F32), out(F32)],
        compiler_params=_cparams("arbitrary"),
        name="odd_in",
    )(x2d, g, sc, sh, w_cat, gate_w, gate_b)


def _gla_kernel(q_ref, k_ref, la_ref, v_ref, sr_ref, g_ref, fwd_ref, rev_ref, o_ref, st_ref):
    group = GLA_CHUNK * GLA_UNROLL
    row = lax.broadcasted_iota(jnp.int32, (group, group), 0)
    col = lax.broadcasted_iota(jnp.int32, (group, group), 1)
    shift = GLA_CHUNK.bit_length() - 1
    keep = jnp.logical_and(jnp.right_shift(row, shift) == jnp.right_shift(col, shift), col <= row)

    @pl.when(pl.program_id(1) == 0)
    def _():
        st_ref[...] = jnp.zeros_like(st_ref)

    def chunk_group(n, carry):
        rows = pl.ds(pl.multiple_of(n * group, group), group)
        la = la_ref[rows, :]
        b_all = _split_dot_left(fwd_ref[...], la)
        r_all = _split_dot_left(rev_ref[...], la)
        heads = []
        for hd in range(GLA_HEADS):
            ls = slice(hd * LANES, (hd + 1) * LANES)
            b = b_all[:, ls]
            k = k_ref[rows, ls]
            q_in = (q_ref[rows, ls] * jnp.exp(b)).astype(BF16)
            k_in = (k * jnp.exp(-b)).astype(BF16)
            k_dec = (k * jnp.exp(r_all[:, ls])).astype(BF16)
            scores = jnp.where(keep, _dot_nt(q_in, k_in), 0.0).astype(BF16)
            heads.append((ls, b, q_in, k_dec, scores))
        intra = [_dot(scores, v_ref[rows, ls]) for ls, _, _, _, scores in heads]
        states = [st_ref[hd] for hd in range(GLA_HEADS)]
        outs = [[] for _ in range(GLA_HEADS)]
        for c in range(GLA_UNROLL):
            cs = slice(c * GLA_CHUNK, (c + 1) * GLA_CHUNK)
            for hd, (ls, b, q_in, k_dec, _) in enumerate(heads):
                outs[hd].append(intra[hd][cs, :] + _dot_nt(q_in[cs, :], states[hd].astype(BF16)))
                b_last = b[(c + 1) * GLA_CHUNK - 1:(c + 1) * GLA_CHUNK, :]
                v = v_ref[pl.ds(pl.multiple_of(n * group, group) + c * GLA_CHUNK, GLA_CHUNK), ls]
                states[hd] = states[hd] * jnp.exp(b_last) + _dot_tn(v, k_dec[cs, :])
        for hd, (ls, _, _, _, _) in enumerate(heads):
            st_ref[hd] = states[hd]
            o = jnp.concatenate(outs[hd], axis=0)
            o = o * lax.rsqrt(jnp.mean(o * o, axis=-1, keepdims=True) + RMS_EPS)
            o_ref[rows, ls] = (o * g_ref[:, ls] * sr_ref[rows, ls]).astype(BF16)
        return carry

    lax.fori_loop(0, TS_GLA // group, chunk_group, 0)


def _gla(q, k, la, v, sr, norm_g, bsz, seq):
    t, width = q.shape
    per_b = seq // TS_GLA
    group = GLA_CHUNK * GLA_UNROLL
    idx = jnp.arange(group)
    same = (idx[:, None] // GLA_CHUNK) == (idx[None, :] // GLA_CHUNK)
    fwd = jnp.logical_and(same, idx[:, None] >= idx[None, :]).astype(BF16)
    rev = jnp.logical_and(same, idx[:, None] < idx[None, :]).astype(BF16)
    blk = pl.BlockSpec((TS_GLA, width), lambda b, i: (b * per_b + i, 0))
    tri_spec = pl.BlockSpec((group, group), lambda b, i: (0, 0))
    return pl.pallas_call(
        _gla_kernel,
        grid=(bsz, per_b),
        in_specs=[blk, blk, blk, blk, blk,
                  pl.BlockSpec((1, width), lambda b, i: (0, 0)),
                  tri_spec, tri_spec],
        out_specs=blk,
        out_shape=jax.ShapeDtypeStruct((t, width), BF16),
        scratch_shapes=[pltpu.VMEM((GLA_HEADS, GLA_DV, LANES), F32)],
        compiler_params=_cparams("arbitrary", "arbitrary"),
        name="gla",
    )(q, k, la, v, sr, norm_g.reshape(1, -1), fwd, rev)


def _pool_kernel(u_ref, halo_ref, w_ref, b_ref, s_ref, o_ref, buf):
    i = pl.program_id(1)
    buf[0:POOL_HALO, :] = jnp.where(i > 0, halo_ref[...], 0.0)
    buf[POOL_HALO:, :] = u_ref[...]
    pos = (i * TS_POOL + 1 + lax.broadcasted_iota(jnp.int32, (TS_POOL, 1), 0)).astype(F32)
    for gi, win in enumerate(POOL_WINDOWS):
        ls = slice(gi * LANES, (gi + 1) * LANES)
        tok = buf[POOL_HALO:, ls]
        tot = tok
        for back in range(1, win):
            tot = tot + buf[POOL_HALO - back:POOL_HALO - back + TS_POOL, ls]
        p = tot / jnp.minimum(pos, float(win)) - tok
        y = _dot(p.astype(BF16), w_ref[gi]) + b_ref[:, ls]
        o_ref[:, ls] = (y * s_ref[:, ls]).astype(BF16)


def _pool(u2d, pool_w, pool_b, pool_scale, bsz, seq):
    t = u2d.shape[0]
    per_b = seq // TS_POOL
    ratio = TS_POOL // POOL_HALO
    full = lambda shape: pl.BlockSpec(shape, lambda b, i: (0,) * len(shape))
    return pl.pallas_call(
        _pool_kernel,
        grid=(bsz, per_b),
        in_specs=[
            pl.BlockSpec((TS_POOL, D_POOL), lambda b, i: (b * per_b + i, 0)),
            pl.BlockSpec((POOL_HALO, D_POOL),
                         lambda b, i: (jnp.maximum((b * per_b + i) * ratio - 1, 0), 0)),
            full((len(POOL_WINDOWS), LANES, LANES)), full((1, D_POOL)), full((1, D_POOL)),
        ],
        out_specs=pl.BlockSpec((TS_POOL, D_POOL), lambda b, i: (b * per_b + i, 0)),
        out_shape=jax.ShapeDtypeStruct((t, D_POOL), BF16),
        scratch_shapes=[pltpu.VMEM((TS_POOL + POOL_HALO, D_POOL), F32)],
        compiler_params=_cparams("arbitrary", "arbitrary"),
        name="pool_mixer",
    )(u2d, u2d, pool_w.astype(BF16), pool_b.reshape(1, D_POOL), pool_scale.reshape(1, D_POOL))


def _pad_heads(w, heads, dim):
    lead = w.shape[:-1]
    w = w.reshape(lead + (heads, dim))
    w = jnp.pad(w, [(0, 0)] * len(lead) + [(0, 0), (0, LANES - dim)])
    return w.reshape(lead + (heads * LANES,))


def _odd_weights(w_in, gate_w, gate_b):
    dkt = GLA_HEADS * GLA_DK
    dvt = GLA_HEADS * GLA_DV
    o = [0, dkt, 2 * dkt, 2 * dkt + dvt, 2 * dkt + 2 * dvt, 2 * dkt + 2 * dvt + GLA_RANK]
    wq, wk, wv, wr, wa, wp = [w_in[:, o[j]:(o + [w_in.shape[1]])[j + 1]] for j in range(6)]
    w_cat = jnp.concatenate([
        _pad_heads(wq, GLA_HEADS, GLA_DK), _pad_heads(wk, GLA_HEADS, GLA_DK), wv, wr, wp,
        jnp.pad(wa, ((0, 0), (0, LANES - GLA_RANK)))], axis=1).astype(BF16)
    gw = jnp.pad(_pad_heads(gate_w, GLA_HEADS, GLA_DK), ((0, LANES - GLA_RANK), (0, 0))).astype(BF16)
    gb = _pad_heads(gate_b, GLA_HEADS, GLA_DK).reshape(1, -1)
    return w_cat, gw, gb


def kernel(x, c, ada_w, ada_b, norm_mix, norm_ffn, w_in_even, w_out_even, conv_w, conv_b, conv_norm_g, conv_norm_b, w_in_odd, w_out_odd, gla_gate_w, gla_gate_b, gla_norm_g, pool_w, pool_b, pool_scale, router_w, router_bias, moe_w_gate, moe_w_up, moe_w_down, final_norm):
    bsz, seq, d = x.shape
    t = bsz * seq
    depth = ada_w.shape[0]
    mod = _ada(c, ada_w, ada_b)
    rw_hi = router_w.T.astype(BF16)
    rw_lo = (router_w.T - rw_hi.astype(F32)).astype(BF16)
    rw_stack = jnp.concatenate([rw_hi, rw_lo], axis=0)
    rb =router_bias.reshape(N_EXPERTS, 1)
    fn = final_norm.reshape(1, d)
    x2d = x.reshape(t, d)
    for l in range(depth):
        sh1, sc1, g1, sh2, sc2, g2 = [mod[l, :, j * d:(j + 1) * d].reshape(bsz, 1, d)
                                      for j in range(6)]
        i = l // 2
        nm = norm_mix[l].reshape(1, d)
        if l % 2 == 0:
            y, q, k, v = _even_in(x2d, nm, sc1, sh1, w_in_even[i].astype(BF16), seq)
            a = _conv(y, conv_w[i], conv_b[i], conv_norm_g[i], conv_norm_b[i], bsz, seq)
            b = _sb_attention(q, k, v, bsz, seq)
            w_out = w_out_even[i]
        else:
            w_cat, gw, gb = _odd_weights(w_in_odd[i], gla_gate_w[i], gla_gate_b[i])
            q, k, v, sr, la, up = _odd_in(x2d, nm, sc1, sh1, w_cat, gw, gb, seq)
            a = _gla(q, k, la, v, sr, gla_norm_g[i], bsz, seq)
            b = _pool(up, pool_w[i], pool_b[i], pool_scale[i], bsz, seq)
            w_out = w_out_odd[i]
        x1, hp, pos, wt, _, sched = _out_router(x2d, a, b, w_out.astype(BF16), g1,
                                                norm_ffn[l].reshape(1, d), sc2, sh2,
                                                rw_stack, rb, seq)
        x2d = _moe(x1, hp, pos, wt, sched, l, moe_w_gate, moe_w_up, moe_w_down, g2, fn, seq,
                   final=(l == depth - 1))
    return x2d.reshape(bsz, seq, d)
```

```python
import functools

import jax
import jax.numpy as jnp
from jax import lax
from jax.experimental import pallas as pl
from jax.experimental.pallas import tpu as pltpu

F32 = jnp.float32
BF16 = jnp.bfloat16

D_MODEL = 1024
D_CONV = 512
CONV_WIDTH = 31
CONV_GROUP_SIZE = 64
SB_HEADS = 8
SB_HEAD_DIM = 64
GLA_HEADS = 4
GLA_DK = 64
GLA_DV = 128
GLA_RANK = 16
GLA_TAU = 16.0
GLA_CHUNK = 64
D_POOL = 512
POOL_WINDOWS = (2, 4, 8, 16)
N_EXPERTS = 16
N_EXPERT_GROUPS = 4
EXPERTS_PER_GROUP = 4
D_EXPERT = 512
RMS_EPS = 1e-6
LN_EPS = 1e-5

LANES = 128
SUBLANES = 8
VMEM_LIMIT = 56 * 1024 * 1024

TM = 512
TS_CONV = 256
CONV_HALO = 32
CONV_ROWS = 64
TS_GLA = 1024
GLA_UNROLL = 4
TS_POOL = 512
POOL_HALO = 16
TQ = 256
TK = 256
SB_PAIRS_PER_STEP = 2
TM_MOE = 512
SB_DEAD = -120.0


def _cparams(*sem):
    return pltpu.CompilerParams(dimension_semantics=sem, vmem_limit_bytes=VMEM_LIMIT)


def _sigmoid(x):
    return 1.0 / (1.0 + jnp.exp(-x))


def _dot(a, b):
    return jnp.dot(a, b, preferred_element_type=F32)


def _dot_nt(a, b, **kw):
    return lax.dot_general(a, b, (((1,), (1,)), ((), ())), preferred_element_type=F32, **kw)


def _dot_tn(a, b):
    return lax.dot_general(a, b, (((0,), (0,)), ((), ())), preferred_element_type=F32)


def _split_dot(x, m):
    hi = x.astype(BF16)
    lo = (x - hi.astype(F32)).astype(BF16)
    return _dot(hi, m) + _dot(lo, m)


def _split_dot_left(m, x):
    hi = x.astype(BF16)
    lo = (x - hi.astype(F32)).astype(BF16)
    return _dot(m, hi) + _dot(m, lo)


def _modulated_rmsnorm(x, g, sc, sh):
    ms = jnp.mean(x * x, axis=-1, keepdims=True)
    return (x * lax.rsqrt(ms + RMS_EPS) * g) * (1.0 + sc) + sh


def _ada_kernel(c_ref, w_ref, b_ref, o_ref):
    c = c_ref[...]
    cond = c * _sigmoid(c)
    o_ref[0] = jnp.dot(cond, w_ref[0], preferred_element_type=F32,
                       precision=lax.Precision.HIGHEST) + b_ref[0]


def _ada(c, ada_w, ada_b):
    depth, d, n = ada_w.shape
    bsz = c.shape[0]
    tn = 1536
    return pl.pallas_call(
        _ada_kernel,
        grid=(depth, n // tn),
        in_specs=[
            pl.BlockSpec((bsz, d), lambda l, j: (0, 0)),
            pl.BlockSpec((1, d, tn), lambda l, j: (l, 0, j)),
            pl.BlockSpec((1, 1, tn), lambda l, j: (l, 0, j)),
        ],
        out_specs=pl.BlockSpec((1, bsz, tn), lambda l, j: (l, 0, j)),
        out_shape=jax.ShapeDtypeStruct((depth, bsz, n), F32),
        compiler_params=_cparams("arbitrary", "arbitrary"),
        name="ada_mod",
    )(c, ada_w, ada_b.reshape(depth, 1, n))


def _even_in_kernel(x_ref, g_ref, sc_ref, sh_ref, w_ref, y_ref, q_ref, k_ref, v_ref):
    h = _modulated_rmsnorm(x_ref[...], g_ref[...], sc_ref[0], sh_ref[0]).astype(BF16)
    val = _dot(h, w_ref[:, 0:512])
    gate = _dot(h, w_ref[:, 512:1024])
    y_ref[...] = val * _sigmoid(gate)
    q_ref[...] = (_dot(h, w_ref[:, 1024:1536]) * (SB_HEAD_DIM ** -0.5)).astype(BF16)
    k_ref[...] = _dot(h, w_ref[:, 1536:2048]).astype(BF16)
    v_ref[...] = _dot(h, w_ref[:, 2048:2560]).astype(BF16)


def _even_in(x2d, g, sc, sh, w_in, seq):
    t, d = x2d.shape
    n = w_in.shape[1]
    per_b = seq // TM
    vec = pl.BlockSpec((1, 1, d), lambda i: (i // per_b, 0, 0))
    out = lambda dt: jax.ShapeDtypeStruct((t, 512), dt)
    return pl.pallas_call(
        _even_in_kernel,
        grid=(t // TM,),
        in_specs=[
            pl.BlockSpec((TM, d), lambda i: (i, 0)),
            pl.BlockSpec((1, d), lambda i: (0, 0)),
            vec, vec,
            pl.BlockSpec((d, n), lambda i: (0, 0)),
        ],
        out_specs=[pl.BlockSpec((TM, 512), lambda i: (i, 0))] * 4,
        out_shape=[out(F32), out(BF16), out(BF16), out(BF16)],
        compiler_params=_cparams("arbitrary"),
        name="even_in",
    )(x2d, g, sc, sh, w_in)


def _conv_kernel(y_ref, halo_ref, w_ref, b_ref, ng_ref, nb_ref, avg_ref, o_ref, buf, sh, cv):
    i = pl.program_id(1)
    buf[0:CONV_HALO, :] = jnp.where(i > 0, halo_ref[...], 0.0)
    buf[CONV_HALO:, :] = y_ref[...]
    sh_rows = sh.shape[1]
    for r in range(1, SUBLANES):
        sh[r - 1] = buf[r:r + sh_rows, :]
    first = CONV_HALO - (CONV_WIDTH - 1)
    for rc in range(TS_CONV // CONV_ROWS):
        r0 = rc * CONV_ROWS
        for lc in range(D_CONV // LANES):
            ls = slice(lc * LANES, (lc + 1) * LANES)
            acc = jnp.zeros((CONV_ROWS, LANES), F32)
            for j in range(CONV_WIDTH):
                whole, r = divmod(first + j, SUBLANES)
                s = r0 + whole * SUBLANES
                src = buf[s:s + CONV_ROWS, ls] if r == 0 else sh[r - 1, s:s + CONV_ROWS, ls]
                acc = acc + w_ref[j:j + 1, ls] * src
            cv[r0:r0 + CONV_ROWS, ls] = acc + b_ref[:, ls]
    y = cv[...]
    avg = avg_ref[...]
    d = y - _split_dot(y, avg)
    var = _split_dot(d * d, avg)
    yn = d * lax.rsqrt(var + LN_EPS) * ng_ref[...] + nb_ref[...]
    o_ref[...] = (yn * _sigmoid(yn)).astype(BF16)


def _conv(y2d, conv_w, conv_b, norm_g, norm_b, bsz, seq):
    t = y2d.shape[0]
    per_b = seq // TS_CONV
    ratio = TS_CONV // CONV_HALO
    gid = jnp.arange(D_CONV) // CONV_GROUP_SIZE
    avg = jnp.where(gid[:, None] == gid[None, :], 1.0 / CONV_GROUP_SIZE, 0.0).astype(BF16)
    w_pad = jnp.pad(conv_w, ((0, 32 - CONV_WIDTH), (0, 0)))
    row = lambda a: a.reshape(1, D_CONV)
    full = lambda shape: pl.BlockSpec(shape, lambda b, i: (0, 0))
    return pl.pallas_call(
        _conv_kernel,
        grid=(bsz, per_b),
        in_specs=[
            pl.BlockSpec((TS_CONV, D_CONV), lambda b, i: (b * per_b + i, 0)),
            pl.BlockSpec((CONV_HALO, D_CONV),
                         lambda b, i: (jnp.maximum((b * per_b + i) * ratio - 1, 0), 0)),
            full((32, D_CONV)), full((1, D_CONV)), full((1, D_CONV)), full((1, D_CONV)),
            full((D_CONV, D_CONV)),
        ],
        out_specs=pl.BlockSpec((TS_CONV, D_CONV), lambda b, i: (b * per_b + i, 0)),
        out_shape=jax.ShapeDtypeStruct((t, D_CONV), BF16),
        scratch_shapes=[pltpu.VMEM((TS_CONV + CONV_HALO, D_CONV), F32),
                        pltpu.VMEM((SUBLANES - 1, TS_CONV + CONV_HALO - SUBLANES, D_CONV), F32),
                        pltpu.VMEM((TS_CONV, D_CONV), F32)],
        compiler_params=_cparams("arbitrary", "arbitrary"),
        name="conv_module",
    )(y2d, y2d, w_pad, row(conv_b), row(norm_g), row(norm_b), avg)


def _sb_kernel(q_ref, k_ref, v_ref, u_ref, o_ref, acc_ref, cr_ref):
    for pair in range(q_ref.shape[1] // LANES):
        _sb_head_pair(slice(pair * LANES, (pair + 1) * LANES),
                      q_ref, k_ref, v_ref, u_ref, o_ref, acc_ref, cr_ref)


def _sb_head_pair(lanes, q_ref, k_ref, v_ref, u_ref, o_ref, acc_ref, cr_ref):
    qi = pl.program_id(2)
    q = q_ref[:, lanes]
    lane = lax.broadcasted_iota(jnp.int32, (TQ, LANES), 1)
    row = lax.broadcasted_iota(jnp.int32, (TQ, TK), 0)
    col = lax.broadcasted_iota(jnp.int32, (TQ, TK), 1)
    qm = [jnp.where(lane < SB_HEAD_DIM, q, jnp.zeros_like(q)),
          jnp.where(lane >= SB_HEAD_DIM, q, jnp.zeros_like(q))]
    acc_ref[...] = jnp.zeros_like(acc_ref)
    cr_ref[...] = jnp.zeros_like(cr_ref)
    per_q = TQ // TK

    def key_blocks(first_kb, count, n_diagonal):
        stage1 = []
        for hd in range(2):
            for step in range(count):
                kb = first_kb - step
                ks = pl.multiple_of(kb * TK, TK)
                z = _dot_nt(qm[hd], k_ref[pl.ds(ks, TK), lanes])
                l1p = jnp.log(1.0 + jnp.exp(-jnp.abs(z)))
                log_keep = -jnp.maximum(z, 0.0) - l1p
                strict = None
                if step < n_diagonal:
                    strict = (col + (kb * TK - qi * TQ)) < row
                    log_keep = jnp.where(strict, log_keep, 0.0)
                stage1.append((hd, ks, strict, jnp.minimum(z, 0.0) - l1p, log_keep))
        cr = [cr_ref[0], cr_ref[1]]
        stage2 = []
        for hd, ks, strict, log_sig, log_keep in stage1:
            stage2.append((hd, ks, strict, log_sig, _split_dot(log_keep, u_ref[...]) + cr[hd]))
            cr[hd] = cr[hd] + jnp.sum(log_keep, axis=-1, keepdims=True)
        acc = [acc_ref[0], acc_ref[1]]
        for hd, ks, strict, log_sig, later in stage2:
            a = jnp.exp(log_sig + later)
            if strict is not None:
                a = jnp.where(strict, a, 0.0)
            acc[hd] = acc[hd] + _dot(a.astype(BF16), v_ref[pl.ds(ks, TK), lanes])
        for hd in range(2):
            cr_ref[hd] = cr[hd]
            acc_ref[hd] = acc[hd]

    def alive():
        return (jnp.max(cr_ref[...]) > SB_DEAD).astype(jnp.int32)

    last_diag = qi * per_q + per_q - 1

    @pl.when(qi == 0)
    def _():
        key_blocks(last_diag, per_q, per_q)

    @pl.when(qi > 0)
    def _():
        key_blocks(last_diag, 2 * per_q, per_q)

    def cond(carry):
        kb, live = carry
        return jnp.logical_and(kb >= 0, live > 0)

    def body(carry):
        kb, _ = carry
        key_blocks(kb, 1, 0)
        return kb - 1, alive()

    lax.while_loop(cond, body, (last_diag - 2 * per_q, alive()))
    o_ref[:, lanes] = jnp.where(lane < SB_HEAD_DIM, acc_ref[0], acc_ref[1]).astype(BF16)


def _sb_attention(q, k, v, bsz, seq):
    t = q.shape[0]
    nq = seq // TQ
    idx = jnp.arange(TK)
    upper = (idx[:, None] > idx[None, :]).astype(BF16)
    width = SB_PAIRS_PER_STEP * LANES
    kv = pl.BlockSpec((seq, width), lambda b, p, i: (b, p))
    return pl.pallas_call(
        _sb_kernel,
        grid=(bsz, SB_HEADS // (2 * SB_PAIRS_PER_STEP), nq),
        in_specs=[
            pl.BlockSpec((TQ, width), lambda b, p, i: (b * nq + i, p)),
            kv, kv,
            pl.BlockSpec((TK, TK), lambda b, p, i: (0, 0)),
        ],
        out_specs=pl.BlockSpec((TQ, width), lambda b, p, i: (b * nq + i, p)),
        out_shape=jax.ShapeDtypeStruct((t, SB_HEADS * SB_HEAD_DIM), BF16),
        scratch_shapes=[pltpu.VMEM((2, TQ, LANES), F32), pltpu.VMEM((2, TQ, 1), F32)],
        compiler_params=_cparams("arbitrary", "arbitrary", "arbitrary"),
        name="stick_breaking",
    )(q, k, v, upper)


def _first_of(vals, target):
    idx = jnp.full(target.shape, len(vals) - 1, jnp.int32)
    for j in range(len(vals) - 2, -1, -1):
        idx = jnp.where(vals[j] == target, j, idx)
    return idx


def _top2(vals):
    m1 = functools.reduce(jnp.maximum, vals)
    i1 = _first_of(vals, m1)
    rest = [jnp.where(i1 == j, -jnp.inf, v) for j, v in enumerate(vals)]
    m2 = functools.reduce(jnp.maximum, rest)
    i2 = _first_of(rest, m2)
    return m1, m2, i1, i2


def _pick(idx, vals):
    out = vals[-1]
    for j in range(len(vals) - 2, -1, -1):
        out = jnp.where(idx == j, vals[j], out)
    return out


def _out_router_kernel(expert_rows, x_ref, a_ref, b_ref, w_ref, g1_ref, ng_ref, sc_ref, sh_ref,
                       rw_ref, rb_ref, tri_ref,
                       x1_ref, hp_ref, pos_ref, wt_ref, cnt_ref, sched_ref, member):
    half = a_ref.shape[1]
    mix = _dot(a_ref[...], w_ref[0:half, :]) + _dot(b_ref[...], w_ref[half:, :])
    x1 = x_ref[...] + g1_ref[0] * mix
    x1_ref[...] = x1
    h = _modulated_rmsnorm(x1, ng_ref[...], sc_ref[0], sh_ref[0])
    hb = h.astype(BF16)
    _to_slabs(hp_ref, h)

    hl = (h - hb.astype(F32)).astype(BF16)
    both = _dot_nt(rw_ref[...], hb)
    logits = (both[0:N_EXPERTS, :] + both[N_EXPERTS:, :]
              + _dot_nt(rw_ref[0:N_EXPERTS, :], hl))
    ex = jnp.exp(logits - jnp.max(logits, axis=0, keepdims=True))
    scores = ex / jnp.sum(ex, axis=0, keepdims=True)
    sel = scores + rb_ref[...]
    srow = [scores[e:e + 1, :] for e in range(N_EXPERTS)]
    lrow = [sel[e:e + 1, :] for e in range(N_EXPERTS)]
    tops = [_top2(lrow[g * EXPERTS_PER_GROUP:(g + 1) * EXPERTS_PER_GROUP])
            for g in range(N_EXPERT_GROUPS)]
    gscore = [tp[0] + tp[1] for tp in tops]
    gidx = _first_of(gscore, functools.reduce(jnp.maximum, gscore))
    e1 = gidx * EXPERTS_PER_GROUP + _pick(gidx, [tp[2] for tp in tops])
    e2 = gidx * EXPERTS_PER_GROUP + _pick(gidx, [tp[3] for tp in tops])
    w1 = _pick(e1, srow)
    w2 = _pick(e2, srow)
    tot = w1 + w2
    wt_ref[0:1, :] = w1 / tot
    wt_ref[1:2, :] = w2 / tot

    @pl.when(pl.program_id(0) == 0)
    def _():
        cnt_ref[...] = jnp.zeros_like(cnt_ref)

    for e in range(N_EXPERTS):
        member[e:e + 1, :] = jnp.where(jnp.logical_or(e1 == e, e2 == e), 1.0, 0.0)
    m = member[...]
    incl = _dot(m.astype(BF16), tri_ref[...])
    excl = incl - m + cnt_ref[:, 0:1]
    cnt_ref[...] = cnt_ref[...] + incl[:, TM - 1:TM]
    erow = [excl[e:e + 1, :] for e in range(N_EXPERTS)]
    pos_ref[0:1, :] = (e1 * expert_rows + _pick(e1, erow).astype(jnp.int32)) * SUBLANES
    pos_ref[1:2, :] = (e2 * expert_rows + _pick(e2, erow).astype(jnp.int32)) * SUBLANES

    @pl.when(pl.program_id(0) == pl.num_programs(0) - 1)
    def _():
        _tile_schedule(cnt_ref[...], expert_rows, sched_ref)


def _tile_schedule(cnt, n_tok, sched_ref):
    blocks_per_expert = n_tok // TM_MOE
    tiles = jnp.floor((cnt + (TM_MOE - 1)) * (1.0 / TM_MOE))
    e_row = lax.broadcasted_iota(jnp.int32, (N_EXPERTS, N_EXPERTS), 0)
    e_col = lax.broadcasted_iota(jnp.int32, (N_EXPERTS, N_EXPERTS), 1)
    lower = jnp.where(e_col <= e_row, 1.0, 0.0).astype(BF16)
    ends = _dot(lower, tiles.astype(BF16))
    starts = ends - tiles
    e_id = lax.broadcasted_iota(jnp.int32, cnt.shape, 0).astype(F32)
    j = lax.broadcasted_iota(jnp.int32, cnt.shape, 1).astype(F32)
    total = ends[N_EXPERTS - 1:N_EXPERTS, :]
    e_j = jnp.minimum(jnp.sum(jnp.where(j >= ends, 1.0, 0.0), axis=0, keepdims=True),
                      float(N_EXPERTS - 1))
    mine = e_id == e_j
    start_j = jnp.sum(jnp.where(mine, starts, 0.0), axis=0, keepdims=True)
    cnt_j = jnp.sum(jnp.where(mine, cnt, 0.0), axis=0, keepdims=True)
    last_e = jnp.max(jnp.where(tiles > 0.0, e_id, 0.0), axis=0, keepdims=True)
    j_row = j[0:1, :]
    local = j_row - start_j
    used = j_row < total
    sched_ref[...] = jnp.zeros_like(sched_ref)
    sched_ref[0:1, :] = jnp.where(used, e_j, last_e).astype(jnp.int32)
    sched_ref[1:2, :] = jnp.where(used, e_j * blocks_per_expert + local,
                                  float(N_EXPERTS * blocks_per_expert)).astype(jnp.int32)
    sched_ref[2:3, :] = jnp.where(used, jnp.clip(cnt_j - local * TM_MOE, 0.0, float(TM_MOE)),
                                  0.0).astype(jnp.int32)


def _out_router(x2d, a, b, w_out, g1, ng, sc, sh, rw_stack, rb, seq):
    t, d = x2d.shape
    per_b = seq // TM
    vec = pl.BlockSpec((1, 1, d), lambda i: (i // per_b, 0, 0))
    full = lambda shape: pl.BlockSpec(shape, lambda i: (0, 0))
    half = a.shape[1]
    idx = jnp.arange(TM)
    tri = (idx[:, None] <= idx[None, :]).astype(BF16)
    assert d == SUBLANES * LANES and 2 * t // TM_MOE + N_EXPERTS <= LANES
    return pl.pallas_call(
        functools.partial(_out_router_kernel, t),
        grid=(t // TM,),
        in_specs=[
            pl.BlockSpec((TM, d), lambda i: (i, 0)),
            pl.BlockSpec((TM, half), lambda i: (i, 0)),
            pl.BlockSpec((TM, half), lambda i: (i, 0)),
            full((2 * half, d)),
            vec, full((1, d)), vec, vec,
            full((2 * N_EXPERTS, d)), full((N_EXPERTS, 1)), full((TM, TM)),
        ],
        out_specs=[
            pl.BlockSpec((TM, d), lambda i: (i, 0)),
            pl.BlockSpec((TM * SUBLANES, LANES), lambda i: (i, 0)),
            pl.BlockSpec((2, TM), lambda i: (0, i)),
            pl.BlockSpec((2, TM), lambda i: (0, i)),
            pl.BlockSpec((N_EXPERTS, LANES), lambda i: (0, 0)),
            pl.BlockSpec((SUBLANES, LANES), lambda i: (0, 0)),
        ],
        out_shape=[
            jax.ShapeDtypeStruct((t, d), F32),
            jax.ShapeDtypeStruct((t * SUBLANES, LANES), F32),
            jax.ShapeDtypeStruct((2, t), jnp.int32),
            jax.ShapeDtypeStruct((2, t), F32),
            jax.ShapeDtypeStruct((N_EXPERTS, LANES), F32),
            jax.ShapeDtypeStruct((SUBLANES, LANES), jnp.int32),
        ],
        scratch_shapes=[pltpu.VMEM((N_EXPERTS, TM), F32)],
        compiler_params=_cparams("arbitrary"),
        name="out_router",
    )(x2d, a, b, w_out, g1, ng, sc, sh, rw_stack, rb, tri)


def _to_slabs(ref, x):
    m = x.shape[0]
    for j in range(SUBLANES):
        ref[pl.ds(j, m, stride=SUBLANES), :] = x[:, j * LANES:(j + 1) * LANES]


def _from_slabs(ref):
    m = ref.shape[0] // SUBLANES
    return jnp.concatenate([ref[pl.ds(j, m, stride=SUBLANES), :] for j in range(SUBLANES)],
                           axis=-1)


def _slab(ref, offset):
    return ref.at[pl.ds(pl.multiple_of(offset, SUBLANES), SUBLANES), :]


def _dispatch_kernel(pos_ref, hp_ref, xs_ref, sem):
    n_tok = pos_ref.shape[0] // 2
    base = pl.program_id(0) * TM

    def issue(r, carry):
        for pick in range(2):
            pltpu.make_async_copy(_slab(hp_ref, r * SUBLANES),
                                  _slab(xs_ref, pos_ref[pick * n_tok + base + r]),
                                  sem).start(priority=pick)
        return carry

    lax.fori_loop(0, TM, issue, 0, unroll=8)
    for _ in range(2):
        pltpu.make_async_copy(hp_ref, xs_ref.at[pl.ds(0, TM * SUBLANES), :], sem).wait()


def _dispatch(pos_flat, hp, rows):
    t = hp.shape[0] // SUBLANES
    return pl.pallas_call(
        _dispatch_kernel,
        grid_spec=pltpu.PrefetchScalarGridSpec(
            num_scalar_prefetch=1,
            grid=(t // TM,),
            in_specs=[pl.BlockSpec((TM * SUBLANES, LANES), lambda i, pos: (i, 0))],
            out_specs=pl.BlockSpec(memory_space=pl.ANY),
            scratch_shapes=[pltpu.SemaphoreType.DMA],
        ),
        out_shape=jax.ShapeDtypeStruct((rows * SUBLANES, LANES), F32),
        compiler_params=_cparams("arbitrary"),
        name="moe_dispatch",
    )(pos_flat, hp)


def _expert_kernel(te_ref, tb_ref, tv_ref, xs_ref, wg_ref, wu_ref, wd_ref, ys_ref,
                   wg_bf, wu_bf, wd_bf):
    i = pl.program_id(0)
    expert = te_ref[i]
    valid = tv_ref[i]

    @pl.when(jnp.logical_or(i == 0, expert != te_ref[jnp.maximum(i - 1, 0)]))
    def _():
        wg_bf[...] = wg_ref[0].astype(BF16)
        wu_bf[...] = wu_ref[0].astype(BF16)
        wd_bf[...] = wd_ref[0].astype(BF16)

    @pl.when(valid > 0)
    def _():
        x = _from_slabs(xs_ref)
        row = lax.broadcasted_iota(jnp.int32, x.shape, 0)
        h = jnp.where(row < valid, x, 0.0).astype(BF16)
        gate = _dot(h, wg_bf[...])
        up = _dot(h, wu_bf[...])
        act = (gate * _sigmoid(gate) * up).astype(BF16)
        _to_slabs(ys_ref, _dot(act, wd_bf[...]))

    @pl.when(valid == 0)
    def _():
        ys_ref[...] = jnp.zeros_like(ys_ref)


def _experts(tile_expert, tile_block, tile_valid, xs, layer, wg, wu, wd):
    d = SUBLANES * LANES
    n_tiles = tile_expert.shape[0]
    wspec = lambda shape: pl.BlockSpec((None,) + shape, lambda i, te, tb, tv: (layer, te[i], 0, 0))
    rowspec = pl.BlockSpec((TM_MOE * SUBLANES, LANES), lambda i, te, tb, tv: (tb[i], 0))
    return pl.pallas_call(
        _expert_kernel,
        grid_spec=pltpu.PrefetchScalarGridSpec(
            num_scalar_prefetch=3,
            grid=(n_tiles,),
            in_specs=[
                rowspec,
                wspec((1, d, D_EXPERT)), wspec((1, d, D_EXPERT)), wspec((1, D_EXPERT, d)),
            ],
            out_specs=rowspec,
            scratch_shapes=[pltpu.VMEM((d, D_EXPERT), BF16), pltpu.VMEM((d, D_EXPERT), BF16),
                            pltpu.VMEM((D_EXPERT, d), BF16)],
        ),
        out_shape=jax.ShapeDtypeStruct(xs.shape, F32),
        compiler_params=_cparams("arbitrary"),
        name="moe_experts",
    )(tile_expert, tile_block, tile_valid, xs, wg, wu, wd)


def _combine_kernel(final, pos_ref, x1_ref, wt_ref, g2_ref, fn_ref, ys_ref, o_ref, buf, sems):
    n_tok = pos_ref.shape[0] // 2
    i = pl.program_id(0)
    n = pl.num_programs(0)

    def gather(tile, slot):
        base = tile * TM

        def issue(r, carry):
            for pick in range(2):
                pltpu.make_async_copy(_slab(ys_ref, pos_ref[pick * n_tok + base + r]),
                                      _slab(buf.at[slot, pick], r * SUBLANES),
                                      sems.at[slot]).start(priority=pick)
            return carry

        lax.fori_loop(0, TM, issue, 0, unroll=8)

    @pl.when(i == 0)
    def _():
        gather(0, 0)

    @pl.when(i + 1 < n)
    def _():
        gather(i + 1, (i + 1) % 2)

    slot = i % 2
    for pick in range(2):
        pltpu.make_async_copy(ys_ref.at[pl.ds(0, TM * SUBLANES), :], buf.at[slot, pick],
                              sems.at[slot]).wait()
    y = (wt_ref[:, 0:1] * _from_slabs(buf.at[slot, 0])
         + wt_ref[:, 1:2] * _from_slabs(buf.at[slot, 1]))
    x2 = x1_ref[...] + g2_ref[0] * y
    if final:
        ms = jnp.mean(x2 * x2, axis=-1, keepdims=True)
        x2 = x2 * lax.rsqrt(ms + RMS_EPS) * fn_ref[...]
    o_ref[...] = x2


def _combine(pos_flat, x1, wt, g2, fn, ys, seq, final):
    t, d = x1.shape
    per_b = seq // TM
    return pl.pallas_call(
        functools.partial(_combine_kernel, final),
        grid_spec=pltpu.PrefetchScalarGridSpec(
            num_scalar_prefetch=1,
            grid=(t // TM,),
            in_specs=[
                pl.BlockSpec((TM, d), lambda i, pos: (i, 0)),
                pl.BlockSpec((TM, 2), lambda i, pos: (i, 0)),
                pl.BlockSpec((1, 1, d), lambda i, pos: (i // per_b, 0, 0)),
                pl.BlockSpec((1, d), lambda i, pos: (0, 0)),
                pl.BlockSpec(memory_space=pl.ANY),
            ],
            out_specs=pl.BlockSpec((TM, d), lambda i, pos: (i, 0)),
            scratch_shapes=[pltpu.VMEM((2, 2, TM * SUBLANES, LANES), F32),
                            pltpu.SemaphoreType.DMA((2,))],
        ),
        out_shape=jax.ShapeDtypeStruct((t, d), F32),
        compiler_params=_cparams("arbitrary"),
        name="moe_combine_final" if final else "moe_combine",
    )(pos_flat, x1, wt, g2, fn, ys)


def _moe(x1, hp, pos, wt, sched, layer, wg, wu, wd, g2, fn, seq, final):
    t = x1.shape[0]
    n_tiles = 2 * t // TM_MOE + N_EXPERTS
    rows = N_EXPERTS * t + TM_MOE
    pos_flat = pos.reshape(-1)
    xs = _dispatch(pos_flat, hp, rows)
    ys = _experts(sched[0, :n_tiles], sched[1, :n_tiles], sched[2, :n_tiles], xs, layer,
                  wg, wu, wd)
    return _combine(pos_flat, x1, wt.T, g2, fn, ys, seq, final)


def _log_sigmoid(x):
    return jnp.minimum(x, 0.0) - jnp.log1p(jnp.exp(-jnp.abs(x)))


def _odd_in_kernel(x_ref, g_ref, sc_ref, sh_ref, w_ref, gw_ref, gb_ref,
                   q_ref, k_ref, v_ref, sr_ref, la_ref, up_ref):
    h = _modulated_rmsnorm(x_ref[...], g_ref[...], sc_ref[0], sh_ref[0]).astype(BF16)
    a_low = _dot(h, w_ref[:, 2560:2688]).astype(BF16)
    gate = _dot(a_low, gw_ref[...]) + gb_ref[...]
    r = _dot(h, w_ref[:, 1536:2048])
    q = _dot(h, w_ref[:, 0:512])
    k = _dot(h, w_ref[:, 512:1024])
    v = _dot(h, w_ref[:, 1024:1536])
    up = _dot(h, w_ref[:, 2048:2560])
    la_ref[...] = _log_sigmoid(gate) * (1.0 / GLA_TAU)
    sr_ref[...] = r * _sigmoid(r)
    q_ref[...] = q * (GLA_DK ** -0.5)
    k_ref[...] = k
    v_ref[...] = v.astype(BF16)
    up_ref[...] = up


def _odd_in(x2d, g, sc, sh, w_cat, gate_w, gate_b, seq):
    t, d = x2d.shape
    n = w_cat.shape[1]
    per_b = seq // TM
    vec = pl.BlockSpec((1, 1, d), lambda i: (i // per_b, 0, 0))
    full = lambda shape: pl.BlockSpec(shape, lambda i: (0, 0))
    out = lambda dt: jax.ShapeDtypeStruct((t, 512), dt)
    return pl.pallas_call(
        _odd_in_kernel,
        grid=(t // TM,),
        in_specs=[
            pl.BlockSpec((TM, d), lambda i: (i, 0)),
            full((1, d)), vec, vec, full((d, n)),
            full((LANES, 512)), full((1, 512)),
        ],
        out_specs=[pl.BlockSpec((TM, 512), lambda i: (i, 0))] * 6,
        out_shape=[out(F32), out(F32), out(BF16), out(F32), out(F32), out(F32)],
        compiler_params=_cparams("arbitrary"),
        name="odd_in",
    )(x2d, g, sc, sh, w_cat, gate_w, gate_b)


def _gla_kernel(q_ref, k_ref, la_ref, v_ref, sr_ref, g_ref, fwd_ref, rev_ref, o_ref, st_ref):
    group = GLA_CHUNK * GLA_UNROLL
    row = lax.broadcasted_iota(jnp.int32, (group, group), 0)
    col = lax.broadcasted_iota(jnp.int32, (group, group), 1)
    shift = GLA_CHUNK.bit_length() - 1
    keep = jnp.logical_and(jnp.right_shift(row, shift) == jnp.right_shift(col, shift), col <= row)

    @pl.when(pl.program_id(1) == 0)
    def _():
        st_ref[...] = jnp.zeros_like(st_ref)

    def chunk_group(n, carry):
        rows = pl.ds(pl.multiple_of(n * group, group), group)
        la = la_ref[rows, :]
        b_all = _split_dot_left(fwd_ref[...], la)
        r_all = _split_dot_left(rev_ref[...], la)
        heads = []
        for hd in range(GLA_HEADS):
            ls = slice(hd * LANES, (hd + 1) * LANES)
            b = b_all[:, ls]
            k = k_ref[rows, ls]
            q_in = (q_ref[rows, ls] * jnp.exp(b)).astype(BF16)
            k_in = (k * jnp.exp(-b)).astype(BF16)
            k_dec = (k * jnp.exp(r_all[:, ls])).astype(BF16)
            scores = jnp.where(keep, _dot_nt(q_in, k_in), 0.0).astype(BF16)
            heads.append((ls, b, q_in, k_dec, scores))
        intra = [_dot(scores, v_ref[rows, ls]) for ls, _, _, _, scores in heads]
        states = [st_ref[hd] for hd in range(GLA_HEADS)]
        outs = [[] for _ in range(GLA_HEADS)]
        for c in range(GLA_UNROLL):
            cs = slice(c * GLA_CHUNK, (c + 1) * GLA_CHUNK)
            for hd, (ls, b, q_in, k_dec, _) in enumerate(heads):
                outs[hd].append(intra[hd][cs, :] + _dot_nt(q_in[cs, :], states[hd].astype(BF16)))
                b_last = b[(c + 1) * GLA_CHUNK - 1:(c + 1) * GLA_CHUNK, :]
                v = v_ref[pl.ds(pl.multiple_of(n * group, group) + c * GLA_CHUNK, GLA_CHUNK), ls]
                states[hd] = states[hd] * jnp.exp(b_last) + _dot_tn(v, k_dec[cs, :])
        for hd, (ls, _, _, _, _) in enumerate(heads):
            st_ref[hd] = states[hd]
            o = jnp.concatenate(outs[hd], axis=0)
            o = o * lax.rsqrt(jnp.mean(o * o, axis=-1, keepdims=True) + RMS_EPS)
            o_ref[rows, ls] = (o * g_ref[:, ls] * sr_ref[rows, ls]).astype(BF16)
        return carry

    lax.fori_loop(0, TS_GLA // group, chunk_group, 0)


def _gla(q, k, la, v, sr, norm_g, bsz, seq):
    t, width = q.shape
    per_b = seq // TS_GLA
    group = GLA_CHUNK * GLA_UNROLL
    idx = jnp.arange(group)
    same = (idx[:, None] // GLA_CHUNK) == (idx[None, :] // GLA_CHUNK)
    fwd = jnp.logical_and(same, idx[:, None] >= idx[None, :]).astype(BF16)
    rev = jnp.logical_and(same, idx[:, None] < idx[None, :]).astype(BF16)
    blk = pl.BlockSpec((TS_GLA, width), lambda b, i: (b * per_b + i, 0))
    tri_spec = pl.BlockSpec((group, group), lambda b, i: (0, 0))
    return pl.pallas_call(
        _gla_kernel,
        grid=(bsz, per_b),
        in_specs=[blk, blk, blk, blk, blk,
                  pl.BlockSpec((1, width), lambda b, i: (0, 0)),
                  tri_spec, tri_spec],
        out_specs=blk,
        out_shape=jax.ShapeDtypeStruct((t, width), BF16),
        scratch_shapes=[pltpu.VMEM((GLA_HEADS, GLA_DV, LANES), F32)],
        compiler_params=_cparams("arbitrary", "arbitrary"),
        name="gla",
    )(q, k, la, v, sr, norm_g.reshape(1, -1), fwd, rev)


def _pool_kernel(u_ref, halo_ref, w_ref, b_ref, s_ref, o_ref, buf):
    i = pl.program_id(1)
    buf[0:POOL_HALO, :] = jnp.where(i > 0, halo_ref[...], 0.0)
    buf[POOL_HALO:, :] = u_ref[...]
    pos = (i * TS_POOL + 1 + lax.broadcasted_iota(jnp.int32, (TS_POOL, 1), 0)).astype(F32)
    for gi, win in enumerate(POOL_WINDOWS):
        ls = slice(gi * LANES, (gi + 1) * LANES)
        tok = buf[POOL_HALO:, ls]
        tot = tok
        for back in range(1, win):
            tot = tot + buf[POOL_HALO - back:POOL_HALO - back + TS_POOL, ls]
        p = tot / jnp.minimum(pos, float(win)) - tok
        y = _dot(p.astype(BF16), w_ref[gi]) + b_ref[:, ls]
        o_ref[:, ls] = (y * s_ref[:, ls]).astype(BF16)


def _pool(u2d, pool_w, pool_b, pool_scale, bsz, seq):
    t = u2d.shape[0]
    per_b = seq // TS_POOL
    ratio = TS_POOL // POOL_HALO
    full = lambda shape: pl.BlockSpec(shape, lambda b, i: (0,) * len(shape))
    return pl.pallas_call(
        _pool_kernel,
        grid=(bsz, per_b),
        in_specs=[
            pl.BlockSpec((TS_POOL, D_POOL), lambda b, i: (b * per_b + i, 0)),
            pl.BlockSpec((POOL_HALO, D_POOL),
                         lambda b, i: (jnp.maximum((b * per_b + i) * ratio - 1, 0), 0)),
            full((len(POOL_WINDOWS), LANES, LANES)), full((1, D_POOL)), full((1, D_POOL)),
        ],
        out_specs=pl.BlockSpec((TS_POOL, D_POOL), lambda b, i: (b * per_b + i, 0)),
        out_shape=jax.ShapeDtypeStruct((t, D_POOL), BF16),
        scratch_shapes=[pltpu.VMEM((TS_POOL + POOL_HALO, D_POOL), F32)],
        compiler_params=_cparams("arbitrary", "arbitrary"),
        name="pool_mixer",
    )(u2d, u2d, pool_w.astype(BF16), pool_b.reshape(1, D_POOL), pool_scale.reshape(1, D_POOL))


def _pad_heads(w, heads, dim):
    lead = w.shape[:-1]
    w = w.reshape(lead + (heads, dim))
    w = jnp.pad(w, [(0, 0)] * len(lead) + [(0, 0), (0, LANES - dim)])
    return w.reshape(lead + (heads * LANES,))


def _odd_weights(w_in, gate_w, gate_b):
    dkt = GLA_HEADS * GLA_DK
    dvt = GLA_HEADS * GLA_DV
    o = [0, dkt, 2 * dkt, 2 * dkt + dvt, 2 * dkt + 2 * dvt, 2 * dkt + 2 * dvt + GLA_RANK]
    wq, wk, wv, wr, wa, wp = [w_in[:, o[j]:(o + [w_in.shape[1]])[j + 1]] for j in range(6)]
    w_cat = jnp.concatenate([
        _pad_heads(wq, GLA_HEADS, GLA_DK), _pad_heads(wk, GLA_HEADS, GLA_DK), wv, wr, wp,
        jnp.pad(wa, ((0, 0), (0, LANES - GLA_RANK)))], axis=1).astype(BF16)
    gw = jnp.pad(_pad_heads(gate_w, GLA_HEADS, GLA_DK), ((0, LANES - GLA_RANK), (0, 0))).astype(BF16)
    gb = _pad_heads(gate_b, GLA_HEADS, GLA_DK).reshape(1, -1)
    return w_cat, gw, gb


def kernel(x, c, ada_w, ada_b, norm_mix, norm_ffn, w_in_even, w_out_even, conv_w, conv_b, conv_norm_g, conv_norm_b, w_in_odd, w_out_odd, gla_gate_w, gla_gate_b, gla_norm_g, pool_w, pool_b, pool_scale, router_w, router_bias, moe_w_gate, moe_w_up, moe_w_down, final_norm):
    bsz, seq, d = x.shape
    t = bsz * seq
    depth = ada_w.shape[0]
    mod = _ada(c, ada_w, ada_b)
    rw_hi = router_w.T.astype(BF16)
    rw_lo = (router_w.T - rw_hi.astype(F32)).astype(BF16)
    rw_stack = jnp.concatenate([rw_hi, rw_lo], axis=0)
    rb =router_bias.reshape(N_EXPERTS, 1)
    fn = final_norm.reshape(1, d)
    x2d = x.reshape(t, d)
    for l in range(depth):
        sh1, sc1, g1, sh2, sc2, g2 = [mod[l, :, j * d:(j + 1) * d].reshape(bsz, 1, d)
                                      for j in range(6)]
        i = l // 2
        nm = norm_mix[l].reshape(1, d)
        if l % 2 == 0:
            y, q, k, v = _even_in(x2d, nm, sc1, sh1, w_in_even[i].astype(BF16), seq)
            a = _conv(y, conv_w[i], conv_b[i], conv_norm_g[i], conv_norm_b[i], bsz, seq)
            b = _sb_attention(q, k, v, bsz, seq)
            w_out = w_out_even[i]
        else:
            w_cat, gw, gb = _odd_weights(w_in_odd[i], gla_gate_w[i], gla_gate_b[i])
            q, k, v, sr, la, up = _odd_in(x2d, nm, sc1, sh1, w_cat, gw, gb, seq)
            a = _gla(q, k, la, v, sr, gla_norm_g[i], bsz, seq)
            b = _pool(up, pool_w[i], pool_b[i], pool_scale[i], bsz, seq)
            w_out = w_out_odd[i]
        x1, hp, pos, wt, _, sched = _out_router(x2d, a, b, w_out.astype(BF16), g1,
                                                norm_ffn[l].reshape(1, d), sc2, sh2,
                                                rw_stack, rb, seq)
        x2d = _moe(x1, hp, pos, wt, sched, l, moe_w_gate, moe_w_up, moe_w_down, g2, fn, seq,
                   final=(l == depth - 1))
    return x2d.reshape(bsz, seq, d)
```
